```python
import math
import jax
import jax.numpy as jnp
from jax import lax
import numpy as np

D_MODEL = 1024
BATCH = 32
SEQ = 256
DEPTH = 4
DEC_BATCH = 8
DEC_SEQ = 2048
PAST_LEN = 512

GRID_W = 64
N_MIXERS = 4
N_LAYERS_A = (DEPTH + 3) // 4
N_LAYERS_B = (DEPTH + 2) // 4
N_LAYERS_C = (DEPTH + 1) // 4
N_LAYERS_D = DEPTH // 4
N_DENSE = (DEPTH + 1) // 2
N_MOE = DEPTH // 2
EPS = 1e-6
RG_WIDTH = D_MODEL
RG_HEADS = 4
RG_BW = RG_WIDTH // RG_HEADS
RG_CONV = 4
RG_C = 8.0
HG_HEADS = 8
HG_DK = D_MODEL // HG_HEADS
HG_DV = D_MODEL // HG_HEADS
HG_CHUNK = 32
HY_ORDER = 2
HY_CONV = 3
HY_EMB = 33
HY_FW = 64
HY_DECAY_MIN = 3.07
HY_DECAY_MAX = 15.35
DA_HEADS = 8
DA_DH = D_MODEL // (2 * DA_HEADS)
ROPE_BASE = 10000.0
Q_BLOCK = 128
FF_DIM = 2816
MOE_E = 8
MOE_K = 2
MOE_FF = 3584

kernel_name = 'hybrid_diffusion_backbone_step'


def rms_norm(x, g):
    xf = x.astype(jnp.float32)
    y = xf * lax.rsqrt(jnp.mean(xf * xf, axis=-1, keepdims=True) + EPS)
    return (y * g.astype(jnp.float32)).astype(x.dtype)


def modulation(cvec, w, b):
    m = jax.nn.silu(cvec) @ w + b
    return m.reshape(cvec.shape[0], 1, 6, D_MODEL)


def modulate(x, g, m, k):
    return rms_norm(x, g) * (1 + m[:, :, k + 1]) + m[:, :, k]


def dwconv(u, w, b):
    kw = w.shape[0]
    pl = (kw - 1) // 2
    y = lax.conv_general_dilated(u, w[:, None, :].astype(u.dtype), window_strides=(1,),
                                 padding=[(pl, kw - 1 - pl)],
                                 dimension_numbers=('NWC', 'WIO', 'NWC'),
                                 feature_group_count=u.shape[-1])
    return y + b.astype(u.dtype)


def linear_scan(a, bx, h0, reverse):
    def comb(e1, e2):
        a1, b1 = e1
        a2, b2 = e2
        return a1 * a2, a2 * b1 + b2
    acum, bcum = lax.associative_scan(comb, (a, bx), reverse=reverse, axis=1)
    return acum * h0[:, None, :] + bcum


def rglru_mixer(u, h0, w_in, conv_w, conv_b, w_a, b_a, w_x, b_x, lam, w_out):
    f32 = jnp.float32
    bn, L, _ = u.shape
    gate, xr = jnp.split(u @ w_in, 2, axis=-1)
    xc = dwconv(xr, conv_w, conv_b)
    xh = xc.reshape(bn, L, RG_HEADS, RG_BW)
    r = jax.nn.sigmoid((jnp.einsum('blhi,dhij->dblhj', xh, w_a).reshape(2, bn, L, RG_WIDTH)
                        + b_a[:, None, None, :]).astype(f32))
    ig = jax.nn.sigmoid((jnp.einsum('blhi,dhij->dblhj', xh, w_x).reshape(2, bn, L, RG_WIDTH)
                         + b_x[:, None, None, :]).astype(f32))
    log_a = -RG_C * jax.nn.softplus(-lam.astype(f32))[:, None, None, :] * r
    a = jnp.exp(log_a)
    bx = jnp.sqrt(-jnp.expm1(2.0 * log_a)) * ig * xc.astype(f32)[None]
    h0 = h0.astype(f32)
    h_f = linear_scan(a[0], bx[0], h0[:, 0], False)
    h_b = linear_scan(a[1], bx[1], h0[:, 1], True)
    y = (jax.nn.gelu(gate.astype(f32)) * (h_f + h_b)).astype(u.dtype) @ w_out
    return y, jnp.stack([h_f[:, -1], h_b[:, 0]], axis=1)


def gla_chunked(q, k, v, log_f, s0):
    bn, nh, L, _ = q.shape
    n = L // HG_CHUNK
    def rs(t):
        return t.reshape(bn, nh, n, HG_CHUNK, t.shape[-1])
    q, k, v, log_f = rs(q), rs(k), rs(v), rs(log_f)
    b = jnp.cumsum(log_f, axis=3)
    b_last = b[:, :, :, -1:, :]
    q_i = q * jnp.exp(b)
    k_i = k * jnp.exp(-b)
    k_s = k * jnp.exp(b_last - b)
    mask = jnp.tril(jnp.ones((HG_CHUNK, HG_CHUNK), dtype=bool))
    att = jnp.where(mask, jnp.einsum('bhncd,bhnsd->bhncs', q_i, k_i), 0.0)
    o = jnp.einsum('bhncs,bhnsv->bhncv', att, v)
    ds = jnp.einsum('bhncd,bhncv->bhndv', k_s, v)
    decay = jnp.exp(b_last[:, :, :, 0, :])
    def step(s, xs):
        g, d = xs
        return g[..., None] * s + d, s
    s_fin, s_prev = lax.scan(step, s0, (jnp.moveaxis(decay, 2, 0), jnp.moveaxis(ds, 2, 0)))
    o = o + jnp.einsum('bhncd,nbhdv->bhncv', q_i, s_prev)
    return o.reshape(bn, nh, L, v.shape[-1]), s_fin


def hgrn2_mixer(u, s0, lb, w_in, norm_g, w_out):
    bn, L, _ = u.shape
    hk, hv = HG_HEADS * HG_DK, HG_HEADS * HG_DV
    q, f_fw, f_bw, v, g = jnp.split(u @ w_in, [hk, 2 * hk, 3 * hk, 3 * hk + hv], axis=-1)
    def heads(t, d):
        return jnp.moveaxis(t.astype(jnp.float32).reshape(bn, L, HG_HEADS, d), 2, 1)
    qh, vh = heads(q, HG_DK), heads(v, HG_DV)
    s0 = s0.astype(jnp.float32)
    outs, finals = [], []
    for d, (fz, rev) in enumerate(((f_fw, False), (f_bw, True))):
        lbd = lb[d].reshape(HG_HEADS, 1, HG_DK)
        f = lbd + (1.0 - lbd) * jax.nn.sigmoid(heads(fz, HG_DK))
        args = (qh, 1.0 - f, vh, jnp.log(f))
        if rev:
            args = tuple(jnp.flip(t, axis=2) for t in args)
        o, s = gla_chunked(args[0], args[1], args[2], args[3], s0[:, d])
        outs.append(jnp.flip(o, axis=2) if rev else o)
        finals.append(s)
    o = rms_norm(outs[0] + outs[1], norm_g) * jax.nn.silu(heads(g, HG_DV))
    y = jnp.moveaxis(o, 1, 2).reshape(bn, L, hv).astype(u.dtype) @ w_out
    return y, jnp.stack(finals, axis=1)


def hyena_filters(L, w1, b1, w2, b2, w3, freq, decay):
    f32 = jnp.float32
    t_idx = jnp.arange(L, dtype=f32)
    t = t_idx / (L - 1)
    bands = (HY_EMB - 1) // 2
    fr = jnp.linspace(1e-4, bands - 1, bands, dtype=f32)
    ang = (2.0 * math.pi * t_idx / L)[:, None] * fr[None, :]
    z = jnp.concatenate([t[:, None], jnp.cos(ang), -jnp.sin(ang)], axis=-1)
    fq = freq.astype(f32)
    h = jnp.sin(fq * (z @ w1.astype(f32) + b1.astype(f32)))
    h = jnp.sin(fq * (h @ w2.astype(f32) + b2.astype(f32)))
    h = (h @ w3.astype(f32)) * jnp.exp(-t[:, None] * jnp.abs(decay.astype(f32)))
    return jnp.moveaxis(h.reshape(L, HY_ORDER, 2, D_MODEL), 0, 2)


def bidir_fftconv(z, hf, hb, bias):
    L = z.shape[1]
    k_full = jnp.concatenate([hf, jnp.zeros((1, hf.shape[1]), jnp.float32), hb[:0:-1]], axis=0)
    k_full = k_full * lax.rsqrt(jnp.sum(k_full * k_full, axis=0, keepdims=True) + EPS)
    kf = jnp.fft.rfft(k_full, axis=0)
    zf = jnp.fft.rfft(z, n=2 * L, axis=1)
    y = jnp.fft.irfft(zf * kf[None], n=2 * L, axis=1)[:, :L]
    return y + z * bias.astype(jnp.float32)


def hyena_mixer(u, w_in, b_in, conv_w, conv_b, f_w1, f_b1, f_w2, f_b2, f_w3, freq, decay, bias, w_out, b_out):
    p = dwconv(u @ w_in + b_in, conv_w, conv_b).astype(jnp.float32)
    v, x1, x2 = jnp.split(p, 3, axis=-1)
    filt = hyena_filters(u.shape[1], f_w1, f_b1, f_w2, f_b2, f_w3, freq, decay)
    z = x1 * bidir_fftconv(v, filt[0, 0], filt[0, 1], bias[0])
    z = x2 * bidir_fftconv(z, filt[1, 0], filt[1, 1], bias[1])
    return z.astype(u.dtype) @ w_out + b_out


def diff_attn_project(u, w_in, q_g, k_g):
    bn, L, _ = u.shape
    q, k, v = jnp.split(u @ w_in, 3, axis=-1)
    q = rms_norm(q.reshape(bn, L, DA_HEADS, 2, DA_DH), q_g)
    k = rms_norm(k.reshape(bn, L, DA_HEADS, 2, DA_DH), k_g)
    return q, k, v.reshape(bn, L, DA_HEADS, 2 * DA_DH)


def axial_rope(x):
    bn, L = x.shape[0], x.shape[1]
    nf = DA_DH // 4
    t = jnp.arange(L)
    pos = jnp.stack([t // GRID_W, t % GRID_W], axis=-1).astype(jnp.float32)
    inv = ROPE_BASE ** (-jnp.arange(nf, dtype=jnp.float32) / nf)
    ang = pos[:, :, None] * inv
    cos = jnp.cos(ang)[None, :, None, None]
    sin = jnp.sin(ang)[None, :, None, None]
    xr = x.astype(jnp.float32).reshape(bn, L, DA_HEADS, 2, 2, 2, nf)
    x0, x1 = xr[..., 0, :], xr[..., 1, :]
    out = jnp.stack([x0 * cos - x1 * sin, x1 * cos + x0 * sin], axis=-2)
    return out.reshape(x.shape).astype(x.dtype)


def diff_attention(q, k, v, lam):
    bn, lq = q.shape[0], q.shape[1]
    nb = lq // Q_BLOCK
    qb = jnp.moveaxis(q.reshape(bn, nb, Q_BLOCK, DA_HEADS, 2, DA_DH), 1, 0)
    scale = DA_DH ** -0.5
    def block(qblk):
        s = jnp.einsum('bqhsd,bkhsd->bhsqk', qblk, k).astype(jnp.float32) * scale
        p = jax.nn.softmax(s, axis=-1)
        a = p[:, :, 0] - lam * p[:, :, 1]
        return jnp.einsum('bhqk,bkhe->bqhe', a.astype(v.dtype), v)
    o = lax.map(block, qb)
    return jnp.moveaxis(o, 0, 1).reshape(bn, lq, DA_HEADS, 2 * DA_DH)


def diff_attn_out(o, lam_init, sub_g, w_out):
    bn, L = o.shape[0], o.shape[1]
    o = rms_norm(o, sub_g) * (1.0 - lam_init)
    return o.reshape(bn, L, DA_HEADS * 2 * DA_DH) @ w_out


def swiglu(u, w_in, w_out):
    g, h = jnp.split(u @ w_in, 2, axis=-1)
    return (jax.nn.silu(g) * h) @ w_out


def moe_swiglu(u, router_w, router_b, w_in, w_out):
    logits = (u @ router_w).astype(jnp.float32) + router_b.astype(jnp.float32)
    top_v, top_i = lax.top_k(logits, MOE_K)
    wts = jax.nn.softmax(top_v, axis=-1)
    comb = jnp.sum(jax.nn.one_hot(top_i, MOE_E, dtype=jnp.float32) * wts[..., None], axis=-2)
    y = jnp.zeros_like(u)
    for e in range(MOE_E):
        y = y + comb[..., e:e + 1].astype(u.dtype) * swiglu(u, w_in[e], w_out[e])
    return y


def setup_inputs(seed: int = 0) -> dict:
    key = jax.random.key(seed)
    keys = iter(jax.random.split(key, 64))
    f32 = jnp.float32
    D = D_MODEL
    hk, hv = HG_HEADS * HG_DK, HG_HEADS * HG_DV
    dw = DA_HEADS * 2 * DA_DH

    def nrm(shape, scale):
        return jax.random.normal(next(keys), shape, f32) * scale

    def gain(shape):
        return 1.0 + nrm(shape, 0.02)

    ua = jax.random.uniform(next(keys), (N_LAYERS_A, 2, RG_WIDTH), f32, 0.9, 0.999)
    sa = ua ** (1.0 / RG_C)
    rg_lambda = jnp.log(sa) - jnp.log1p(-sa)
    hy_decay = (jnp.tile(jnp.linspace(HY_DECAY_MIN, HY_DECAY_MAX, D, dtype=f32), HY_ORDER * 2)[None]
                * jnp.exp(nrm((N_LAYERS_C, HY_ORDER * 2 * D), 0.05)))
    return {
        'x_prompt': nrm((BATCH, SEQ, D), 1.0),
        'x_sample': nrm((DEC_BATCH, DEC_SEQ, D), 1.0),
        'cache_k': nrm((DEC_BATCH, N_LAYERS_D, PAST_LEN, DA_HEADS, 2 * DA_DH), 1.0),
        'cache_v': nrm((DEC_BATCH, N_LAYERS_D, PAST_LEN, DA_HEADS, 2 * DA_DH), 1.0),
        'state_rglru': nrm((DEC_BATCH, N_LAYERS_A, 2, RG_WIDTH), 0.5),
        'state_hgrn': nrm((DEC_BATCH, N_LAYERS_B, 2, HG_HEADS, HG_DK, HG_DV), 0.5),
        'c': nrm((DEC_BATCH, D), 1.0),
        'c_ctx': nrm((D,), 1.0),
        'mod_w': nrm((DEPTH, D, 6 * D), 0.5 * D ** -0.5),
        'mod_b': nrm((DEPTH, 6 * D), 0.02),
        'norm_g': gain((DEPTH, 2, D)),
        'hgrn_lb': nrm((DEPTH, 2, hk), 0.1),
        'rg_w_in': nrm((N_LAYERS_A, D, 2 * RG_WIDTH), D ** -0.5),
        'rg_conv_w': nrm((N_LAYERS_A, RG_CONV, RG_WIDTH), RG_CONV ** -0.5),
        'rg_conv_b': nrm((N_LAYERS_A, RG_WIDTH), 0.02),
        'rg_w_a': nrm((N_LAYERS_A, 2, RG_HEADS, RG_BW, RG_BW), RG_BW ** -0.5),
        'rg_b_a': nrm((N_LAYERS_A, 2, RG_WIDTH), 0.02),
        'rg_w_x': nrm((N_LAYERS_A, 2, RG_HEADS, RG_BW, RG_BW), RG_BW ** -0.5),
        'rg_b_x': nrm((N_LAYERS_A, 2, RG_WIDTH), 0.02),
        'rg_lambda': rg_lambda,
        'rg_w_out': nrm((N_LAYERS_A, RG_WIDTH, D), RG_WIDTH ** -0.5),
        'hg_w_in': nrm((N_LAYERS_B, D, 3 * hk + 2 * hv), D ** -0.5),
        'hg_norm_g': gain((N_LAYERS_B, HG_DV)),
        'hg_w_out': nrm((N_LAYERS_B, hv, D), hv ** -0.5),
        'hy_w_in': nrm((N_LAYERS_C, D, 3 * D), D ** -0.5),
        'hy_b_in': nrm((N_LAYERS_C, 3 * D), 0.02),
        'hy_conv_w': nrm((N_LAYERS_C, HY_CONV, 3 * D), HY_CONV ** -0.5),
        'hy_conv_b': nrm((N_LAYERS_C, 3 * D), 0.02),
        'hy_f_w1': nrm((N_LAYERS_C, HY_EMB, HY_FW), HY_EMB ** -0.5),
        'hy_f_b1': nrm((N_LAYERS_C, HY_FW), 0.02),
        'hy_f_w2': nrm((N_LAYERS_C, HY_FW, HY_FW), HY_FW ** -0.5),
        'hy_f_b2': nrm((N_LAYERS_C, HY_FW), 0.02),
        'hy_f_w3': nrm((N_LAYERS_C, HY_FW, HY_ORDER * 2 * D), HY_FW ** -0.5),
        'hy_freq': gain((N_LAYERS_C, HY_FW)),
        'hy_decay': hy_decay,
        'hy_bias': nrm((N_LAYERS_C, HY_ORDER, D), 0.5),
        'hy_w_out': nrm((N_LAYERS_C, D, D), D ** -0.5),
        'hy_b_out': nrm((N_LAYERS_C, D), 0.02),
        'da_w_in': nrm((N_LAYERS_D, D, 3 * dw), D ** -0.5),
        'da_q_norm': gain((N_LAYERS_D, DA_DH)),
        'da_k_norm': gain((N_LAYERS_D, DA_DH)),
        'da_lambda': nrm((N_LAYERS_D, 4, DA_DH), 0.1),
        'da_sub_norm': gain((N_LAYERS_D, 2 * DA_DH)),
        'da_w_out': nrm((N_LAYERS_D, dw, D), dw ** -0.5),
        'ff_w_in': nrm((N_DENSE, D, 2 * FF_DIM), D ** -0.5),
        'ff_w_out': nrm((N_DENSE, FF_DIM, D), FF_DIM ** -0.5),
        'moe_router': nrm((N_MOE, D, MOE_E), D ** -0.5),
        'moe_router_b': nrm((N_MOE, MOE_E), 0.01),
        'moe_w_in': nrm((N_MOE, MOE_E, D, 2 * MOE_FF), D ** -0.5),
        'moe_w_out': nrm((N_MOE, MOE_E, MOE_FF, D), MOE_FF ** -0.5),
    }


def reference(x_prompt, x_sample, cache_k, cache_v, state_rglru, state_hgrn, c, c_ctx,
              mod_w, mod_b, norm_g, hgrn_lb,
              rg_w_in, rg_conv_w, rg_conv_b, rg_w_a, rg_b_a, rg_w_x, rg_b_x, rg_lambda, rg_w_out,
              hg_w_in, hg_norm_g, hg_w_out,
              hy_w_in, hy_b_in, hy_conv_w, hy_conv_b, hy_f_w1, hy_f_b1, hy_f_w2, hy_f_b2, hy_f_w3,
              hy_freq, hy_decay, hy_bias, hy_w_out, hy_b_out,
              da_w_in, da_q_norm, da_k_norm, da_lambda, da_sub_norm, da_w_out,
              ff_w_in, ff_w_out, moe_router, moe_router_b, moe_w_in, moe_w_out):
    f32 = jnp.float32
    xp, xs = x_prompt, x_sample
    bp = x_prompt.shape[0]
    lb_all = jnp.cumsum(jax.nn.softmax(hgrn_lb.astype(f32), axis=0), axis=0)
    lb_all = lb_all - lb_all[0]
    new_k, new_v, new_rg, new_hg = [], [], [], []
    for i in range(DEPTH):
        kind, j = i % N_MIXERS, i // N_MIXERS
        mp = modulation(c_ctx[None, :], mod_w[i], mod_b[i])
        ms = modulation(c, mod_w[i], mod_b[i])
        up = modulate(xp, norm_g[i, 0], mp, 0)
        us = modulate(xs, norm_g[i, 0], ms, 0)
        if kind == 0:
            yp, st = rglru_mixer(up, jnp.zeros((bp, 2, RG_WIDTH), f32), rg_w_in[j], rg_conv_w[j], rg_conv_b[j],
                                 rg_w_a[j], rg_b_a[j], rg_w_x[j], rg_b_x[j], rg_lambda[j], rg_w_out[j])
            ys, _ = rglru_mixer(us, state_rglru[:, j], rg_w_in[j], rg_conv_w[j], rg_conv_b[j],
                                rg_w_a[j], rg_b_a[j], rg_w_x[j], rg_b_x[j], rg_lambda[j], rg_w_out[j])
            new_rg.append(st)
        elif kind == 1:
            yp, st = hgrn2_mixer(up, jnp.zeros((bp, 2, HG_HEADS, HG_DK, HG_DV), f32), lb_all[i],
                                 hg_w_in[j], hg_norm_g[j], hg_w_out[j])
            ys, _ = hgrn2_mixer(us, state_hgrn[:, j], lb_all[i], hg_w_in[j], hg_norm_g[j], hg_w_out[j])
            new_hg.append(st)
        elif kind == 2:
            yp = hyena_mixer(up, hy_w_in[j], hy_b_in[j], hy_conv_w[j], hy_conv_b[j], hy_f_w1[j], hy_f_b1[j],
                             hy_f_w2[j], hy_f_b2[j], hy_f_w3[j], hy_freq[j], hy_decay[j], hy_bias[j],
                             hy_w_out[j], hy_b_out[j])
            ys = hyena_mixer(us, hy_w_in[j], hy_b_in[j], hy_conv_w[j], hy_conv_b[j], hy_f_w1[j], hy_f_b1[j],
                             hy_f_w2[j], hy_f_b2[j], hy_f_w3[j], hy_freq[j], hy_decay[j], hy_bias[j],
                             hy_w_out[j], hy_b_out[j])
        else:
            lam_init = 0.8 - 0.6 * math.exp(-0.3 * i)
            lp = da_lambda[j].astype(f32)
            lam = jnp.exp(jnp.sum(lp[0] * lp[1])) - jnp.exp(jnp.sum(lp[2] * lp[3])) + lam_init
            qp, kp, vp = diff_attn_project(up, da_w_in[j], da_q_norm[j], da_k_norm[j])
            yp = diff_attn_out(diff_attention(qp, kp, vp, lam), lam_init, da_sub_norm[j], da_w_out[j])
            ql, kl, vl = diff_attn_project(us, da_w_in[j], da_q_norm[j], da_k_norm[j])
            ql, kl = axial_rope(ql), axial_rope(kl)
            ck, cv = cache_k[:, j], cache_v[:, j]
            ck = ck.reshape(ck.shape[0], ck.shape[1], DA_HEADS, 2, DA_DH).astype(kl.dtype)
            k_all = jnp.concatenate([kl, ck], axis=1)
            v_all = jnp.concatenate([vl, cv.astype(vl.dtype)], axis=1)
            ys = diff_attn_out(diff_attention(ql, k_all, v_all, lam), lam_init, da_sub_norm[j], da_w_out[j])
            new_k.append(kp.reshape(bp, kp.shape[1], DA_HEADS, 2 * DA_DH))
            new_v.append(vp)
        xp = xp + mp[:, :, 2] * yp.astype(xp.dtype)
        xs = xs + ms[:, :, 2] * ys.astype(xs.dtype)
        hp = modulate(xp, norm_g[i, 1], mp, 3)
        hs = modulate(xs, norm_g[i, 1], ms, 3)
        n = i // 2
        if i % 2 == 0:
            fp = swiglu(hp, ff_w_in[n], ff_w_out[n])
            fs = swiglu(hs, ff_w_in[n], ff_w_out[n])
        else:
            fp = moe_swiglu(hp, moe_router[n], moe_router_b[n], moe_w_in[n], moe_w_out[n])
            fs = moe_swiglu(hs, moe_router[n], moe_router_b[n], moe_w_in[n], moe_w_out[n])
        xp = xp + mp[:, :, 5] * fp.astype(xp.dtype)
        xs = xs + ms[:, :, 5] * fs.astype(xs.dtype)
    new_cache_k = jnp.stack(new_k, axis=1).astype(x_prompt.dtype)
    new_cache_v = jnp.stack(new_v, axis=1).astype(x_prompt.dtype)
    new_state_rglru = jnp.stack(new_rg, axis=1).astype(x_prompt.dtype)
    new_state_hgrn = jnp.stack(new_hg, axis=1).astype(x_prompt.dtype)
    return (xp, xs, new_cache_k, new_cache_v, new_state_rglru, new_state_hgrn)
```

```python
import functools
import math
from typing import NamedTuple

import numpy as np
import jax
import jax.numpy as jnp
from jax import lax
from jax.experimental import pallas as pl
from jax.experimental.pallas import tpu as pltpu

F32 = jnp.float32
BF16 = jnp.bfloat16

D_MODEL = 1024
DEPTH = 4
EPS = 1e-6
GRID_W = 64
RG_HEADS = 4
RG_BW = D_MODEL // RG_HEADS
RG_C = 8.0
HG_HEADS = 8
HG_DK = D_MODEL // HG_HEADS
HG_CHUNK = 32
HY_EMB = 33
HY_FW = 64
DA_HEADS = 8
DA_DH = 64
ROPE_BASE = 10000.0
FF_DIM = 2816
MOE_E = 8
MOE_FF = 3584

LANES = 128
ROW_TILE = 512
SEQ_TILE = 256
SCAN_CHUNK = 64
VMEM_LIMIT = 56 * 1024 * 1024


class Cfg(NamedTuple):
    bp: int
    lp: int
    bs: int
    ls: int

    @property
    def tp(self):
        return self.bp * self.lp

    @property
    def ts(self):
        return self.bs * self.ls

    @property
    def t(self):
        return self.tp + self.ts


def _cparams(sem):
    return pltpu.CompilerParams(dimension_semantics=sem, vmem_limit_bytes=VMEM_LIMIT)


def _mod_idx(i, cfg, tm):
    npt = cfg.tp // tm
    return jnp.where(i < npt, 0, 1 + (i - npt) // (cfg.ls // tm))


def _modulate(x, g, m, k):
    ms = jnp.mean(x * x, axis=-1, keepdims=True)
    y = x * lax.rsqrt(ms + EPS) * g
    return y * (1.0 + m[k + 1:k + 2, :]) + m[k:k + 1, :]


def _silu(x):
    return x * jax.nn.sigmoid(x)


def _dot(a, b):
    return jnp.dot(a, b, preferred_element_type=F32)


def _dot_nt(a, b):
    return lax.dot_general(a, b, (((1,), (1,)), ((), ())), preferred_element_type=F32)


def _dot_tn(a, b):
    return lax.dot_general(a, b, (((0,), (0,)), ((), ())), preferred_element_type=F32)


def _dot_hi(a, b):
    return jnp.dot(a, b, preferred_element_type=F32, precision=lax.Precision.HIGHEST)


def _modtab_kernel(c_ref, w_ref, b_ref, o_ref):
    s = _silu(c_ref[...]).astype(BF16)
    o_ref[...] = _dot(s, w_ref[...].astype(BF16)) + b_ref[...]


def modulation_table(cvec, mod_w, mod_b):
    n = cvec.shape[0]
    npad = 16
    cpad = jnp.zeros((npad, D_MODEL), F32).at[:n].set(cvec)
    tn = 1536
    out = pl.pallas_call(
        _modtab_kernel,
        grid=(DEPTH, 6 * D_MODEL // tn),
        in_specs=[
            pl.BlockSpec((npad, D_MODEL), lambda l, j: (0, 0)),
            pl.BlockSpec((None, D_MODEL, tn), lambda l, j: (l, 0, j)),
            pl.BlockSpec((None, 1, tn), lambda l, j: (l, 0, j)),
        ],
        out_specs=pl.BlockSpec((None, npad, tn), lambda l, j: (l, 0, j)),
        out_shape=jax.ShapeDtypeStruct((DEPTH, npad, 6 * D_MODEL), F32),
        compiler_params=_cparams(("parallel", "parallel")),
        name="modtab",
    )(cpad, mod_w, mod_b.reshape(DEPTH, 1, 6 * D_MODEL))
    return out[:, :n].reshape(DEPTH, n, 6, D_MODEL)


def _modmm_kernel(*refs, k, has_bias, heads_out):
    if has_bias:
        x_ref, m_ref, g_ref, w_ref, b_ref, o_ref, u_scr = refs
    else:
        x_ref, m_ref, g_ref, w_ref, o_ref, u_scr = refs
        b_ref = None

    @pl.when(pl.program_id(1) == 0)
    def _():
        u_scr[...] = _modulate(x_ref[...], g_ref[...], m_ref[...], k).astype(BF16)

    acc = _dot(u_scr[...], w_ref[...])
    if has_bias:
        acc = acc + b_ref[...]
    if heads_out:
        for hh in range(o_ref.shape[0]):
            o_ref[hh] = acc[:, hh * LANES:(hh + 1) * LANES]
    else:
        o_ref[...] = acc


def mod_matmul(x, mtab, layer, k, g, w, b, cfg, heads_out=False, tn=1024):
    t, n = x.shape[0], w.shape[1]
    tm = ROW_TILE
    in_specs = [
        pl.BlockSpec((tm, D_MODEL), lambda i, j: (i, 0)),
        pl.BlockSpec((None, None, 6, D_MODEL), lambda i, j: (layer, _mod_idx(i, cfg, tm), 0, 0)),
        pl.BlockSpec((1, D_MODEL), lambda i, j: (0, 0)),
        pl.BlockSpec((D_MODEL, tn), lambda i, j: (0, j)),
    ]
    args = [x, mtab, g.reshape(1, D_MODEL), w]
    if b is not None:
        in_specs.append(pl.BlockSpec((1, tn), lambda i, j: (0, j)))
        args.append(b.reshape(1, n))
    if heads_out:
        out_specs = pl.BlockSpec((tn // LANES, tm, LANES), lambda i, j: (j, i, 0))
        out_shape = jax.ShapeDtypeStruct((n // LANES, t, LANES), F32)
    else:
        out_specs = pl.BlockSpec((tm, tn), lambda i, j: (i, j))
        out_shape = jax.ShapeDtypeStruct((t, n), F32)
    return pl.pallas_call(
        functools.partial(_modmm_kernel, k=k, has_bias=b is not None, heads_out=heads_out),
        grid=(t // tm, n // tn),
        in_specs=in_specs,
        out_specs=out_specs,
        out_shape=out_shape,
        scratch_shapes=[pltpu.VMEM((tm, D_MODEL), BF16)],
        compiler_params=_cparams(("parallel", "arbitrary")),
        name="modmm",
    )(*args)


def _seq_edges(i, cfg, lb):
    npb = cfg.tp // lb
    pp, ps = cfg.lp // lb, cfg.ls // lb
    first = jnp.where(i < npb, i % pp == 0, (i - npb) % ps == 0)
    last = jnp.where(i < npb, i % pp == pp - 1, (i - npb) % ps == ps - 1)
    return first, last


def _dwconv_kernel(x_ref, p_ref, n_ref, w_ref, b_ref, o_ref, *, cfg, taps):
    lb = x_ref.shape[0]
    first, last = _seq_edges(pl.program_id(0), cfg, lb)
    prev = jnp.where(first, 0.0, p_ref[...])
    nxt = jnp.where(last, 0.0, n_ref[...])
    ext = jnp.concatenate([prev, x_ref[...], nxt], axis=0)
    left = (taps - 1) // 2
    n_ext = lb + 16
    acc = jnp.zeros(x_ref.shape, F32) + b_ref[...]
    for kk in range(taps):
        sh = (left - kk) % n_ext
        shifted = ext if sh == 0 else pltpu.roll(ext, sh, 0)
        acc = acc + w_ref[kk:kk + 1, :] * shifted[8:8 + lb]
    o_ref[...] = acc


def dwconv(x, col_off, w, b, cfg, tc=1024):
    t = x.shape[0]
    taps, c = w.shape
    lb = SEQ_TILE
    cb = col_off // tc
    r8 = lb // 8
    nblk8 = t // 8
    return pl.pallas_call(
        functools.partial(_dwconv_kernel, cfg=cfg, taps=taps),
        grid=(t // lb, c // tc),
        in_specs=[
            pl.BlockSpec((lb, tc), lambda i, j: (i, cb + j)),
            pl.BlockSpec((8, tc), lambda i, j: (jnp.maximum(i * r8 - 1, 0), cb + j)),
            pl.BlockSpec((8, tc), lambda i, j: (jnp.minimum((i + 1) * r8, nblk8 - 1), cb + j)),
            pl.BlockSpec((taps, tc), lambda i, j: (0, j)),
            pl.BlockSpec((1, tc), lambda i, j: (0, j)),
        ],
        out_specs=pl.BlockSpec((lb, tc), lambda i, j: (i, j)),
        out_shape=jax.ShapeDtypeStruct((t, c), F32),
        compiler_params=_cparams(("parallel", "parallel")),
        name="dwconv",
    )(x, x, x, w, b.reshape(1, c))


def _rg_gates_kernel(xc_ref, wa_ref, wx_ref, ba_ref, bx_ref, lam_ref, a_ref, b_ref):
    xc = xc_ref[...]
    xb = xc.astype(BF16)
    nlam = -lam_ref[...]
    sp = jnp.maximum(nlam, 0.0) + jnp.log1p(jnp.exp(-jnp.abs(nlam)))
    for d in range(2):
        ra = jnp.concatenate(
            [_dot(xb[:, h * RG_BW:(h + 1) * RG_BW], wa_ref[d, h]) for h in range(RG_HEADS)], axis=1)
        rx = jnp.concatenate(
            [_dot(xb[:, h * RG_BW:(h + 1) * RG_BW], wx_ref[d, h]) for h in range(RG_HEADS)], axis=1)
        r = jax.nn.sigmoid(ra + ba_ref[d:d + 1, :])
        ig = jax.nn.sigmoid(rx + bx_ref[d:d + 1, :])
        log_a = (-RG_C) * sp[d:d + 1, :] * r
        a = jnp.exp(log_a)
        gain = jnp.sqrt(-jnp.tanh(log_a) * (1.0 + a * a))
        a_ref[d] = a
        b_ref[d] = gain * ig * xc


def rg_gates(xc, w_a, w_x, b_a, b_x, lam):
    t = xc.shape[0]
    tm = ROW_TILE
    wspec = pl.BlockSpec((2, RG_HEADS, RG_BW, RG_BW), lambda i: (0, 0, 0, 0))
    vspec = pl.BlockSpec((2, D_MODEL), lambda i: (0, 0))
    ospec = pl.BlockSpec((2, tm, D_MODEL), lambda i: (0, i, 0))
    oshape = jax.ShapeDtypeStruct((2, t, D_MODEL), F32)
    return pl.pallas_call(
        _rg_gates_kernel,
        grid=(t // tm,),
        in_specs=[pl.BlockSpec((tm, D_MODEL), lambda i: (i, 0)), wspec, wspec, vspec, vspec, vspec],
        out_specs=[ospec, ospec],
        out_shape=[oshape, oshape],
        compiler_params=_cparams(("parallel",)),
        name="rg_gates",
    )(xc, w_a, w_x, b_a, b_x, lam)


def _rg_scan_kernel(af_ref, bf_ref, ab_ref, bb_ref, h0_ref, hf_ref, hb_ref, hc_scr):
    lc = af_ref.shape[1]

    @pl.when(pl.program_id(1) == 0)
    def _():
        hc_scr[0] = h0_ref[:, 0]
        hc_scr[1] = h0_ref[:, 1]

    def body(tt, carry):
        hf, hb = carry
        hf = af_ref[:, tt] * hf + bf_ref[:, tt]
        hf_ref[:, tt] = hf
        tb = lc - 1 - tt
        hb = ab_ref[:, tb] * hb + bb_ref[:, tb]
        hb_ref[:, tb] = hb
        return hf, hb

    hf, hb = lax.fori_loop(0, lc, body, (hc_scr[0], hc_scr[1]), unroll=4)
    hc_scr[0] = hf
    hc_scr[1] = hb


def rg_scan_group(a, bx, h0, seq0, nseq, length, bsz):
    t = a.shape[1]
    lc = min(SCAN_CHUNK, length)
    nc = length // lc
    sub = D_MODEL // LANES
    av = a.reshape(2, t // length, length, sub, LANES)
    bv = bx.reshape(2, t // length, length, sub, LANES)
    sb0 = seq0 // bsz
    blk = (None, bsz, lc, sub, LANES)
    fwd = lambda g, c: (0, sb0 + g, c, 0, 0)
    bwd = lambda g, c: (1, sb0 + g, nc - 1 - c, 0, 0)
    oblk = (bsz, lc, sub, LANES)
    oshape = jax.ShapeDtypeStruct((nseq, length, sub, LANES), F32)
    hf, hb = pl.pallas_call(
        _rg_scan_kernel,
        grid=(nseq // bsz, nc),
        in_specs=[
            pl.BlockSpec(blk, fwd), pl.BlockSpec(blk, fwd),
            pl.BlockSpec(blk, bwd), pl.BlockSpec(blk, bwd),
            pl.BlockSpec((bsz, 2, sub, LANES), lambda g, c: (g, 0, 0, 0)),
        ],
        out_specs=[
            pl.BlockSpec(oblk, lambda g, c: (g, c, 0, 0)),
            pl.BlockSpec(oblk, lambda g, c: (g, nc - 1 - c, 0, 0)),
        ],
        out_shape=[oshape, oshape],
        scratch_shapes=[pltpu.VMEM((2, bsz, sub, LANES), F32)],
        compiler_params=_cparams(("parallel", "arbitrary")),
        name="rg_scan",
    )(av, bv, av, bv, h0.reshape(nseq, 2, sub, LANES))
    return hf.reshape(nseq * length, D_MODEL), hb.reshape(nseq * length, D_MODEL)


def _gelu_tanh(x):
    return 0.5 * x * (1.0 + jnp.tanh(math.sqrt(2.0 / math.pi) * (x + 0.044715 * (x * x * x))))


def _rg_out_kernel(gate_ref, hf_ref, hb_ref, w_ref, x_ref, m_ref, o_ref):
    y = (_gelu_tanh(gate_ref[...]) * (hf_ref[...] + hb_ref[...])).astype(BF16)
    o_ref[...] = x_ref[...] + m_ref[2:3, :] * _dot(y, w_ref[...])


def _tok_spec(tm, cb=0):
    return pl.BlockSpec((tm, D_MODEL), lambda i: (i, cb))


def _mtab_spec(layer, cfg, tm):
    return pl.BlockSpec((None, None, 6, D_MODEL), lambda i: (layer, _mod_idx(i, cfg, tm), 0, 0))


def _w_spec(k=D_MODEL):
    return pl.BlockSpec((k, D_MODEL), lambda i: (0, 0))


def rg_out(gx, hf, hb, w_out, x, mtab, layer, cfg):
    t = x.shape[0]
    tm = ROW_TILE
    return pl.pallas_call(
        _rg_out_kernel,
        grid=(t // tm,),
        in_specs=[_tok_spec(tm, 0), _tok_spec(tm), _tok_spec(tm), _w_spec(), _tok_spec(tm),
                  _mtab_spec(layer, cfg, tm)],
        out_specs=_tok_spec(tm),
        out_shape=jax.ShapeDtypeStruct((t, D_MODEL), F32),
        compiler_params=_cparams(("parallel",)),
        name="rg_out",
    )(gx, hf, hb, w_out, x, mtab)


def rglru_layer(x, mtab, layer, j, norm_g, p, state_rglru, cfg):
    gx = mod_matmul(x, mtab, layer, 0, norm_g, p['rg_w_in'][j].astype(BF16), None, cfg)
    xc = dwconv(gx, D_MODEL, p['rg_conv_w'][j], p['rg_conv_b'][j], cfg)
    a, bx = rg_gates(xc, p['rg_w_a'][j].astype(BF16), p['rg_w_x'][j].astype(BF16),
                     p['rg_b_a'][j], p['rg_b_x'][j], p['rg_lambda'][j])
    h0p = jnp.zeros((cfg.bp, 2, D_MODEL), F32)
    hfp, hbp = rg_scan_group(a, bx, h0p, 0, cfg.bp, cfg.lp, math.gcd(cfg.bp, 8))
    bsz = math.gcd(math.gcd(cfg.bs, cfg.tp // cfg.ls), 8)
    hfs, hbs = rg_scan_group(a, bx, state_rglru[:, j].astype(F32), cfg.tp // cfg.ls, cfg.bs, cfg.ls, bsz)
    hf = jnp.concatenate([hfp, hfs], axis=0)
    hb = jnp.concatenate([hbp, hbs], axis=0)
    new_state = jnp.stack([hfp.reshape(cfg.bp, cfg.lp, D_MODEL)[:, -1],
                           hbp.reshape(cfg.bp, cfg.lp, D_MODEL)[:, 0]], axis=1)
    x = rg_out(gx, hf, hb, p['rg_w_out'][j].astype(BF16), x, mtab, layer, cfg)
    return x, new_state


def _chunk_cumsum(x, rev):
    rows = x.shape[0]
    r = lax.broadcasted_iota(jnp.int32, x.shape, 0) % HG_CHUNK
    s = 1
    while s < HG_CHUNK:
        if rev:
            x = x + jnp.where(r < HG_CHUNK - s, pltpu.roll(x, rows - s, 0), 0.0)
        else:
            x = x + jnp.where(r >= s, pltpu.roll(x, s, 0), 0.0)
        s *= 2
    return x


def _gla_kernel(fblk, bblk, first, last, seqo, s0i, has0,
                qf_ref, ff_ref, vf_ref, qb_ref, fb_ref, vb_ref, lb_ref, s0_ref,
                of_ref, ob_ref, sfin_ref, s_scr, *, layer):
    i = pl.program_id(1)
    rows = qf_ref.shape[0]
    nch = rows // HG_CHUNK

    @pl.when(first[i] == 1)
    def _():
        for d in range(2):
            s_scr[d] = jnp.where(has0[i] == 1, s0_ref[d].T, 0.0)

    lbx = lb_ref[...]
    e = jnp.exp(lbx - jnp.max(lbx, axis=0, keepdims=True))
    sm = e / jnp.sum(e, axis=0, keepdims=True)
    lb = jnp.zeros(lbx.shape[1:], F32)
    for l in range(1, layer + 1):
        lb = lb + sm[l]

    ri = lax.broadcasted_iota(jnp.int32, (rows, rows), 0)
    ci = lax.broadcasted_iota(jnp.int32, (rows, rows), 1)
    same = (ri // HG_CHUNK) == (ci // HG_CHUNK)

    for d, (q_ref, f_ref, v_ref, o_ref) in enumerate(((qf_ref, ff_ref, vf_ref, of_ref),
                                                       (qb_ref, fb_ref, vb_ref, ob_ref))):
        rev = d == 1
        q = q_ref[...]
        v16 = v_ref[...].astype(BF16)
        lbd = lb[d:d + 1, :]
        f = lbd + (1.0 - lbd) * jax.nn.sigmoid(f_ref[...])
        k = 1.0 - f
        b = _chunk_cumsum(jnp.log(f), rev)
        b3 = b.reshape(nch, HG_CHUNK, HG_DK)
        bl = b3[:, 0:1, :] if rev else b3[:, HG_CHUNK - 1:HG_CHUNK, :]
        qi16 = (q * jnp.exp(b)).astype(BF16)
        ki16 = (k * jnp.exp(-b)).astype(BF16)
        ks16 = (k.reshape(nch, HG_CHUNK, HG_DK) * jnp.exp(bl - b3)).reshape(rows, HG_DK).astype(BF16)
        g = jnp.exp(bl)
        att = _dot_nt(qi16, ki16)
        mask = same & ((ci >= ri) if rev else (ci <= ri))
        att = jnp.where(mask, att, 0.0).astype(BF16)
        o = _dot(att, v16)
        st = s_scr[d]
        inter = [None] * nch
        for n in (range(nch - 1, -1, -1) if rev else range(nch)):
            sl = slice(n * HG_CHUNK, (n + 1) * HG_CHUNK)
            inter[n] = _dot_nt(qi16[sl], st.astype(BF16))
            st = st * g[n] + _dot_tn(v16[sl], ks16[sl])
        s_scr[d] = st
        o_ref[...] = o + jnp.concatenate(inter, axis=0)

    @pl.when(last[i] == 1)
    def _():
        for d in range(2):
            sfin_ref[d] = s_scr[d].T


def _gla_tables(cfg):
    rb = SEQ_TILE
    pp, ps = cfg.lp // rb, cfg.ls // rb
    fblk, bblk, first, last, seqo, s0i, has0 = [], [], [], [], [], [], []
    for s in range(cfg.bp):
        for c in range(pp):
            fblk.append(s * pp + c); bblk.append(s * pp + pp - 1 - c)
            first.append(int(c == 0)); last.append(int(c == pp - 1))
            seqo.append(s); s0i.append(0); has0.append(0)
    base = cfg.tp // rb
    for s in range(cfg.bs):
        for c in range(ps):
            fblk.append(base + s * ps + c); bblk.append(base + s * ps + ps - 1 - c)
            first.append(int(c == 0)); last.append(int(c == ps - 1))
            seqo.append(cfg.bp); s0i.append(s); has0.append(1)
    return [jnp.asarray(np.asarray(v, np.int32)) for v in (fblk, bblk, first, last, seqo, s0i, has0)]


def gla(proj, hgrn_lb, s0, layer, cfg):
    t = proj.shape[1]
    rb = SEQ_TILE
    tabs = _gla_tables(cfg)
    nslots = tabs[0].shape[0]
    h8 = HG_HEADS

    def pspec(sec, which):
        return pl.BlockSpec((None, rb, HG_DK),
                            lambda h, i, fb, bb, *_: (sec * h8 + h, (fb if which == 0 else bb)[i], 0))

    ospec_f = pl.BlockSpec((None, rb, HG_DK), lambda h, i, fb, bb, *_: (h, fb[i], 0))
    ospec_b = pl.BlockSpec((None, rb, HG_DK), lambda h, i, fb, bb, *_: (h, bb[i], 0))
    grid_spec = pltpu.PrefetchScalarGridSpec(
        num_scalar_prefetch=7,
        grid=(h8, nslots),
        in_specs=[
            pspec(0, 0), pspec(1, 0), pspec(3, 0),
            pspec(0, 1), pspec(2, 1), pspec(3, 1),
            pl.BlockSpec((None, DEPTH, 2, HG_DK), lambda h, i, *_: (h, 0, 0, 0)),
            pl.BlockSpec((None, 2, None, HG_DK, HG_DK),
                         lambda h, i, fb, bb, fi, la, so, s0i, *_: (s0i[i], 0, h, 0, 0)),
        ],
        out_specs=[
            ospec_f, ospec_b,
            pl.BlockSpec((None, 2, None, HG_DK, HG_DK),
                         lambda h, i, fb, bb, fi, la, so, *_: (so[i], 0, h, 0, 0)),
        ],
        scratch_shapes=[pltpu.VMEM((2, HG_DK, HG_DK), F32)],
    )
    oshape = jax.ShapeDtypeStruct((h8, t, HG_DK), F32)
    of, ob, sfin = pl.pallas_call(
        functools.partial(_gla_kernel, layer=layer),
        grid_spec=grid_spec,
        out_shape=[oshape, oshape,
                   jax.ShapeDtypeStruct((cfg.bp + 1, 2, h8, HG_DK, HG_DK), F32)],
        compiler_params=_cparams(("parallel", "arbitrary")),
        name="gla",
    )(*tabs, proj, proj, proj, proj, proj, proj,
      hgrn_lb.reshape(DEPTH, 2, h8, HG_DK).transpose(2, 0, 1, 3), s0)
    return of, ob, sfin[:cfg.bp]


def _head_rms(o, g):
    ms = jnp.mean(o * o, axis=-1, keepdims=True)
    return o * lax.rsqrt(ms + EPS) * g


def _hg_out_kernel(of_ref, ob_ref, gh_ref, ng_ref, w_ref, x_ref, m_ref, o_ref):
    parts = []
    for h in range(HG_HEADS):
        o = _head_rms(of_ref[h] + ob_ref[h], ng_ref[...]) * _silu(gh_ref[h])
        parts.append(o.astype(BF16))
    y = jnp.concatenate(parts, axis=1)
    o_ref[...] = x_ref[...] + m_ref[2:3, :] * _dot(y, w_ref[...])


def hg_out(of, ob, proj, norm_g, w_out, x, mtab, layer, cfg):
    t = x.shape[0]
    tm = ROW_TILE
    hspec = pl.BlockSpec((HG_HEADS, tm, HG_DK), lambda i: (0, i, 0))
    return pl.pallas_call(
        _hg_out_kernel,
        grid=(t // tm,),
        in_specs=[hspec, hspec,
                  pl.BlockSpec((HG_HEADS, tm, HG_DK), lambda i: (4, i, 0)),
                  pl.BlockSpec((1, HG_DK), lambda i: (0, 0)),
                  _w_spec(), _tok_spec(tm), _mtab_spec(layer, cfg, tm)],
        out_specs=_tok_spec(tm),
        out_shape=jax.ShapeDtypeStruct((t, D_MODEL), F32),
        compiler_params=_cparams(("parallel",)),
        name="hg_out",
    )(of, ob, proj, norm_g.reshape(1, HG_DK), w_out, x, mtab)


def hgrn2_layer(x, mtab, layer, j, norm_g, p, state_hgrn, cfg):
    proj = mod_matmul(x, mtab, layer, 0, norm_g, p['hg_w_in'][j].astype(BF16), None, cfg, heads_out=True)
    of, ob, sfin = gla(proj, p['hgrn_lb'], state_hgrn[:, j].astype(F32), layer, cfg)
    x = hg_out(of, ob, proj, p['hg_norm_g'][j], p['hg_w_out'][j].astype(BF16), x, mtab, layer, cfg)
    return x, sfin


def _dft_tables(length):
    n = 2 * length
    kk = lax.broadcasted_iota(jnp.int32, (length, length), 0)
    jj = lax.broadcasted_iota(jnp.int32, (length, length), 1)
    ang = ((kk * jj) % n).astype(F32) * (2.0 * math.pi / n)
    a = jnp.cos(ang)
    nyq = jnp.where(jj % 2 == 0, 1.0, -1.0)
    b = jnp.where(kk == 0, nyq, -jnp.sin(ang))
    return a.astype(BF16), b.astype(BF16), b.T.astype(BF16)


def _hy_features(length):
    t_idx = jnp.arange(length, dtype=F32)
    tt = t_idx / (length - 1)
    bands = (HY_EMB - 1) // 2
    fr = jnp.linspace(1e-4, bands - 1, bands, dtype=F32)
    ang = (2.0 * math.pi * t_idx / length)[:, None] * fr[None, :]
    z = jnp.concatenate([tt[:, None], jnp.cos(ang), -jnp.sin(ang)], axis=-1)
    return jnp.zeros((length, LANES), F32).at[:, :HY_EMB].set(z)


def _hy_filter_kernel(z_ref, w1_ref, b1_ref, w2_ref, b2_ref, w3f_ref, w3b_ref, fq_ref, dcf_ref, dcb_ref,
                      a_ref, b_ref, ka_ref, ki_ref, kn_ref):
    length = z_ref.shape[0]
    fq = fq_ref[...]
    h = jnp.sin(fq * (_dot_hi(z_ref[...], w1_ref[...]) + b1_ref[...]))
    h = jnp.sin(fq * (_dot_hi(h, w2_ref[...]) + b2_ref[...]))
    row = lax.broadcasted_iota(jnp.int32, (length, 1), 0)
    tt = row.astype(F32) / float(length - 1)
    hf = _dot_hi(h, w3f_ref[...]) * jnp.exp(-tt * jnp.abs(dcf_ref[...]))
    hb = _dot_hi(h, w3b_ref[...]) * jnp.exp(-tt * jnp.abs(dcb_ref[...]))
    hb = jnp.where(row == 0, 0.0, hb)
    nrm = lax.rsqrt(jnp.sum(hf * hf + hb * hb, axis=0, keepdims=True) + EPS)
    hf = hf * nrm
    hb = hb * nrm
    a = a_ref[...]
    bm = b_ref[...]
    ka_ref[...] = _dot(a, (hf + hb).astype(BF16))
    kbf = _dot(bm, hf.astype(BF16))
    kbb = _dot(bm, hb.astype(BF16))
    ki_ref[...] = jnp.where(row == 0, 0.0, kbf - kbb)
    kn_ref[...] = jnp.broadcast_to(kbf[0:1, :] + kbb[0:1, :], kn_ref.shape)


def _const_spec(shape):
    nd = len(shape)
    return pl.BlockSpec(shape, lambda *_: (0,) * nd, pipeline_mode=pl.Buffered(1))


def hy_filter_spectrum(length, a_tab, b_tab, p, j, tc=256):
    d = D_MODEL
    nct = d // tc
    w1 = jnp.zeros((LANES, HY_FW), F32).at[:HY_EMB].set(p['hy_f_w1'][j])
    w3 = p['hy_f_w3'][j]
    dec = p['hy_decay'][j].reshape(1, 4 * d)
    z = _hy_features(length)
    fwd_col = lambda o, c: (0, (o * 2) * nct + c)
    bwd_col = lambda o, c: (0, (o * 2 + 1) * nct + c)
    return pl.pallas_call(
        _hy_filter_kernel,
        grid=(2, nct),
        in_specs=[
            _const_spec((length, LANES)), _const_spec((LANES, HY_FW)), _const_spec((1, HY_FW)),
            _const_spec((HY_FW, HY_FW)), _const_spec((1, HY_FW)),
            pl.BlockSpec((HY_FW, tc), fwd_col), pl.BlockSpec((HY_FW, tc), bwd_col),
            _const_spec((1, HY_FW)),
            pl.BlockSpec((1, tc), fwd_col), pl.BlockSpec((1, tc), bwd_col),
            _const_spec((length, length)), _const_spec((length, length)),
        ],
        out_specs=[
            pl.BlockSpec((None, length, tc), lambda o, c: (o, 0, c)),
            pl.BlockSpec((None, length, tc), lambda o, c: (o, 0, c)),
            pl.BlockSpec((None, 8, tc), lambda o, c: (o, 0, c)),
        ],
        out_shape=[jax.ShapeDtypeStruct((2, length, d), F32),
                   jax.ShapeDtypeStruct((2, length, d), F32),
                   jax.ShapeDtypeStruct((2, 8, d), F32)],
        compiler_params=_cparams(("parallel", "parallel")),
        name="hy_filter",
    )(z, w1, p['hy_f_b1'][j].reshape(1, HY_FW), p['hy_f_w2'][j], p['hy_f_b2'][j].reshape(1, HY_FW),
      w3, w3, p['hy_freq'][j].reshape(1, HY_FW), dec, dec, a_tab, b_tab)


def _sconv_kernel(z_ref, x_ref, a_ref, b_ref, bt_ref, ka_ref, ki_ref, kn_ref, bias_ref, o_ref,
                  yr_scr, yi_scr, *, fb):
    length = z_ref.shape[0]
    zb = z_ref[...].astype(BF16)
    inv = 1.0 / (2 * length)
    for kf in range(length // fb):
        rows = slice(kf * fb, (kf + 1) * fb)
        pr = _dot(a_ref[rows, :], zb)
        qi = _dot(b_ref[rows, :], zb)
        ka = ka_ref[rows, :]
        ki = ki_ref[rows, :]
        if kf == 0:
            r0 = lax.broadcasted_iota(jnp.int32, (fb, 1), 0) == 0
            kd = jnp.where(r0, kn_ref[0:1, :], ka)
            wgt = jnp.where(r0, inv, 2.0 * inv)
        else:
            kd = ka
            wgt = 2.0 * inv
        yr_scr[rows, :] = ((pr * ka - qi * ki) * wgt).astype(BF16)
        yi_scr[rows, :] = ((pr * ki + qi * kd) * wgt).astype(BF16)
    for tb in range(length // fb):
        rows = slice(tb * fb, (tb + 1) * fb)
        y = _dot(a_ref[rows, :], yr_scr[...]) + _dot(bt_ref[rows, :], yi_scr[...])
        o_ref[rows, :] = x_ref[rows, :] * (y + bias_ref[...] * z_ref[rows, :])


def sconv_group(zarr, zcol, zseq0, xarr, xcol, xseq0, nseq, length, tabs, ka, ki, kn, order, bias, tc=256):
    a_tab, b_tab, bt_tab = tabs
    d = D_MODEL
    fb = min(256, length)
    zc, xc = zcol // tc, xcol // tc
    return pl.pallas_call(
        functools.partial(_sconv_kernel, fb=fb),
        grid=(nseq, d // tc),
        in_specs=[
            pl.BlockSpec((length, tc), lambda s, c: (zseq0 + s, zc + c)),
            pl.BlockSpec((length, tc), lambda s, c: (xseq0 + s, xc + c)),
            _const_spec((length, length)), _const_spec((length, length)), _const_spec((length, length)),
            pl.BlockSpec((None, length, tc), lambda s, c: (order, 0, c)),
            pl.BlockSpec((None, length, tc), lambda s, c: (order, 0, c)),
            pl.BlockSpec((None, 8, tc), lambda s, c: (order, 0, c)),
            pl.BlockSpec((None, 1, tc), lambda s, c: (order, 0, c)),
        ],
        out_specs=pl.BlockSpec((length, tc), lambda s, c: (s, c)),
        out_shape=jax.ShapeDtypeStruct((nseq * length, d), F32),
        scratch_shapes=[pltpu.VMEM((length, tc), BF16), pltpu.VMEM((length, tc), BF16)],
        compiler_params=_cparams(("parallel", "parallel")),
        name="sconv",
    )(zarr, xarr, a_tab, b_tab, bt_tab, ka, ki, kn, bias.reshape(2, 1, d))


def _lin_out_kernel(a_ref, w_ref, b_ref, x_ref, m_ref, o_ref):
    y = _dot(a_ref[...].astype(BF16), w_ref[...]) + b_ref[...]
    o_ref[...] = x_ref[...] + m_ref[2:3, :] * y


def lin_out(a, w, b, x, mtab, layer, cfg):
    t = x.shape[0]
    tm = ROW_TILE
    return pl.pallas_call(
        _lin_out_kernel,
        grid=(t // tm,),
        in_specs=[_tok_spec(tm), _w_spec(), pl.BlockSpec((1, D_MODEL), lambda i: (0, 0)),
                  _tok_spec(tm), _mtab_spec(layer, cfg, tm)],
        out_specs=_tok_spec(tm),
        out_shape=jax.ShapeDtypeStruct((t, D_MODEL), F32),
        compiler_params=_cparams(("parallel",)),
        name="lin_out",
    )(a, w, b.reshape(1, D_MODEL), x, mtab)


def hyena_layer(x, mtab, layer, j, norm_g, p, cfg):
    d = D_MODEL
    pre = mod_matmul(x, mtab, layer, 0, norm_g, p['hy_w_in'][j].astype(BF16), p['hy_b_in'][j], cfg)
    pc = dwconv(pre, 0, p['hy_conv_w'][j], p['hy_conv_b'][j], cfg)
    bias = p['hy_bias'][j]
    outs = []
    for (seq0_rows, nseq, length) in ((0, cfg.bp, cfg.lp), (cfg.tp, cfg.bs, cfg.ls)):
        tabs = _dft_tables(length)
        ka, ki, kn = hy_filter_spectrum(length, tabs[0], tabs[1], p, j)
        seq0 = seq0_rows // length
        z1 = sconv_group(pc, 0, seq0, pc, d, seq0, nseq, length, tabs, ka, ki, kn, 0, bias)
        z2 = sconv_group(z1, 0, 0, pc, 2 * d, seq0, nseq, length, tabs, ka, ki, kn, 1, bias)
        outs.append(z2)
    z = jnp.concatenate(outs, axis=0)
    return lin_out(z, p['hy_w_out'][j].astype(BF16), p['hy_b_out'][j], x, mtab, layer, cfg)


def _group_rms_scale(x, g1, g1t):
    x2 = x * x
    hi = x2.astype(BF16)
    lo = (x2 - hi.astype(F32)).astype(BF16)
    s = _dot(hi, g1) + _dot(lo, g1)
    r = lax.rsqrt(s * (1.0 / DA_DH) + EPS)
    rhi = r.astype(BF16)
    rlo = (r - rhi.astype(F32)).astype(BF16)
    return _dot(rhi, g1t) + _dot(rlo, g1t)


def _rope(x, cos, sin_signed):
    lane = lax.broadcasted_iota(jnp.int32, x.shape, 1)
    w = x.shape[1]
    nf = DA_DH // 4
    rot = jnp.where(lane % (2 * nf) < nf, pltpu.roll(x, w - nf, 1), pltpu.roll(x, nf, 1))
    return x * cos + rot * sin_signed


def _qk_prep_kernel(q_ref, k_ref, v_ref, qg_ref, kg_ref, g1_ref, g1t_ref, cos_ref, sin_ref,
                    qh_ref, kh_ref, vh_ref, kc_ref, *, cfg):
    tm = q_ref.shape[0]
    i = pl.program_id(0)
    is_prompt = i < cfg.tp // tm
    g1 = g1_ref[...]
    g1t = g1t_ref[...]
    q = q_ref[...]
    k = k_ref[...]
    qn = q * _group_rms_scale(q, g1, g1t) * qg_ref[...]
    kn = k * _group_rms_scale(k, g1, g1t) * kg_ref[...]

    @pl.when(is_prompt)
    def _():
        kc_ref[...] = kn

    cos = cos_ref[...]
    sin = sin_ref[...]
    qo = jnp.where(is_prompt, qn, _rope(qn, cos, sin)).astype(BF16)
    ko = jnp.where(is_prompt, kn, _rope(kn, cos, sin)).astype(BF16)
    vo = v_ref[...].astype(BF16)
    for h in range(DA_HEADS):
        sl = slice(h * LANES, (h + 1) * LANES)
        qh_ref[h] = qo[:, sl]
        kh_ref[h] = ko[:, sl]
        vh_ref[h] = vo[:, sl]


def _rope_tables(length):
    nf = DA_DH // 4
    t = jnp.arange(length)
    pos = jnp.stack([t // GRID_W, t % GRID_W], axis=-1).astype(F32)
    inv = ROPE_BASE ** (-jnp.arange(nf, dtype=F32) / nf)
    ang = pos[:, :, None] * inv
    cos = jnp.cos(ang)[:, :, None, :]
    sin = jnp.sin(ang)[:, :, None, :]
    cos = jnp.broadcast_to(cos, (length, 2, 2, nf)).reshape(length, DA_DH)
    sin = jnp.concatenate([-sin, sin], axis=2).reshape(length, DA_DH)
    reps = D_MODEL // DA_DH
    return jnp.tile(cos, (1, reps)), jnp.tile(sin, (1, reps))


def qk_prep(qkv, q_g, k_g, cfg):
    t = qkv.shape[0]
    tm = ROW_TILE
    d = D_MODEL
    ngrp = d // DA_DH
    grp = jnp.arange(d) // DA_DH
    g1 = (grp[:, None] == jnp.arange(LANES)[None, :]).astype(BF16)
    g1t = g1.T
    cos, sin = _rope_tables(cfg.ls)
    npt = cfg.tp // tm
    pps = cfg.ls // tm
    tab_spec = pl.BlockSpec((tm, d), lambda i: (jnp.maximum(i - npt, 0) % pps, 0))
    hspec = pl.BlockSpec((DA_HEADS, tm, LANES), lambda i: (0, i, 0))
    hshape = jax.ShapeDtypeStruct((DA_HEADS, t, LANES), BF16)
    return pl.pallas_call(
        functools.partial(_qk_prep_kernel, cfg=cfg),
        grid=(t // tm,),
        in_specs=[_tok_spec(tm, 0), _tok_spec(tm, 1), _tok_spec(tm, 2),
                  pl.BlockSpec((1, d), lambda i: (0, 0)), pl.BlockSpec((1, d), lambda i: (0, 0)),
                  pl.BlockSpec((d, LANES), lambda i: (0, 0)), pl.BlockSpec((LANES, d), lambda i: (0, 0)),
                  tab_spec, tab_spec],
        out_specs=[hspec, hspec, hspec,
                   pl.BlockSpec((tm, d), lambda i: (jnp.minimum(i, npt - 1), 0))],
        out_shape=[hshape, hshape, hshape, jax.ShapeDtypeStruct((cfg.tp, d), F32)],
        compiler_params=_cparams(("arbitrary",)),
        name="qk_prep",
    )(qkv, qkv, qkv, jnp.tile(q_g, ngrp).reshape(1, d), jnp.tile(k_g, ngrp).reshape(1, d),
      g1, g1t, cos, sin)


def _dattn_kernel(*refs, has_cache, lam_init):
    if has_cache:
        lp_ref, q_ref, k_ref, v_ref, ck_ref, cv_ref, o_ref = refs
    else:
        lp_ref, q_ref, k_ref, v_ref, o_ref = refs
    lp = lp_ref[...]
    lam = (jnp.exp(jnp.sum(lp[0:1] * lp[1:2], axis=1, keepdims=True))
           - jnp.exp(jnp.sum(lp[2:3] * lp[3:4], axis=1, keepdims=True)) + lam_init)
    scale = DA_DH ** -0.5
    q = q_ref[...]
    lane = lax.broadcasted_iota(jnp.int32, q.shape, 1)
    zero = jnp.zeros_like(q)
    qs = (jnp.where(lane < DA_DH, q, zero), jnp.where(lane >= DA_DH, q, zero))
    k = k_ref[...]
    v = v_ref[...]
    if has_cache:
        ck = ck_ref[...].astype(BF16)
        cv = cv_ref[...].astype(BF16)
    acc_l = None
    acc_c = None
    for half in range(2):
        s = _dot_nt(qs[half], k) * scale
        m = jnp.max(s, axis=-1, keepdims=True)
        if has_cache:
            c = _dot_nt(qs[half], ck) * scale
            m = jnp.maximum(m, jnp.max(c, axis=-1, keepdims=True))
        e = jnp.exp(s - m)
        z = jnp.sum(e, axis=-1, keepdims=True)
        if has_cache:
            ec = jnp.exp(c - m)
            z = z + jnp.sum(ec, axis=-1, keepdims=True)
        coef = (1.0 / z) if half == 0 else (-lam / z)
        acc_l = e * coef if acc_l is None else acc_l + e * coef
        if has_cache:
            acc_c = ec * coef if acc_c is None else acc_c + ec * coef
    o = _dot(acc_l.astype(BF16), v)
    if has_cache:
        o = o + _dot(acc_c.astype(BF16), cv)
    o_ref[...] = o


def dattn_group(qh, kh, vh, da_lambda, lam_init, seq0, nseq, length, cache_k=None, cache_v=None, tq=256):
    tq = min(tq, length)
    nq = length // tq
    has_cache = cache_k is not None
    in_specs = [
        pl.BlockSpec((4, DA_DH), lambda b, h, qi: (0, 0)),
        pl.BlockSpec((None, tq, LANES), lambda b, h, qi: (h, (seq0 + b) * nq + qi, 0)),
        pl.BlockSpec((None, length, LANES), lambda b, h, qi: (h, seq0 + b, 0)),
        pl.BlockSpec((None, length, LANES), lambda b, h, qi: (h, seq0 + b, 0)),
    ]
    args = [da_lambda, qh, kh, vh]
    if has_cache:
        past = cache_k.shape[1]
        cspec = pl.BlockSpec((None, past, LANES), lambda b, h, qi: (b, 0, h))
        in_specs += [cspec, cspec]
        args += [cache_k.reshape(nseq, past, DA_HEADS * LANES), cache_v.reshape(nseq, past, DA_HEADS * LANES)]
    return pl.pallas_call(
        functools.partial(_dattn_kernel, has_cache=has_cache, lam_init=lam_init),
        grid=(nseq, DA_HEADS, nq),
        in_specs=in_specs,
        out_specs=pl.BlockSpec((None, tq, LANES), lambda b, h, qi: (h, b * nq + qi, 0)),
        out_shape=jax.ShapeDtypeStruct((DA_HEADS, nseq * length, LANES), F32),
        compiler_params=_cparams(("parallel", "parallel", "parallel")),
        name="dattn",
    )(*args)


def _da_out_kernel(o_ref, sg_ref, w_ref, x_ref, m_ref, out_ref, *, lam_init):
    parts = []
    for h in range(DA_HEADS):
        parts.append((_head_rms(o_ref[h], sg_ref[...]) * (1.0 - lam_init)).astype(BF16))
    y = jnp.concatenate(parts, axis=1)
    out_ref[...] = x_ref[...] + m_ref[2:3, :] * _dot(y, w_ref[...])


def da_out(o, sub_g, w_out, x, mtab, layer, lam_init, cfg):
    t = x.shape[0]
    tm = ROW_TILE
    return pl.pallas_call(
        functools.partial(_da_out_kernel, lam_init=lam_init),
        grid=(t // tm,),
        in_specs=[pl.BlockSpec((DA_HEADS, tm, LANES), lambda i: (0, i, 0)),
                  pl.BlockSpec((1, LANES), lambda i: (0, 0)),
                  _w_spec(), _tok_spec(tm), _mtab_spec(layer, cfg, tm)],
        out_specs=_tok_spec(tm),
        out_shape=jax.ShapeDtypeStruct((t, D_MODEL), F32),
        compiler_params=_cparams(("parallel",)),
        name="da_out",
    )(o, sub_g.reshape(1, LANES), w_out, x, mtab)


def diffattn_layer(x, mtab, layer, j, norm_g, p, cache_k, cache_v, cfg):
    d = D_MODEL
    lam_init = 0.8 - 0.6 * math.exp(-0.3 * layer)
    qkv = mod_matmul(x, mtab, layer, 0, norm_g, p['da_w_in'][j].astype(BF16), None, cfg)
    qh, kh, vh, kc = qk_prep(qkv, p['da_q_norm'][j], p['da_k_norm'][j], cfg)
    lamp = p['da_lambda'][j].astype(F32)
    op = dattn_group(qh, kh, vh, lamp, lam_init, 0, cfg.bp, cfg.lp)
    os_ = dattn_group(qh, kh, vh, lamp, lam_init, cfg.tp // cfg.ls, cfg.bs, cfg.ls,
                      cache_k[:, j], cache_v[:, j])
    o = jnp.concatenate([op, os_], axis=1)
    x = da_out(o, p['da_sub_norm'][j], p['da_w_out'][j].astype(BF16), x, mtab, layer, lam_init, cfg)
    new_k = kc.reshape(cfg.bp, cfg.lp, DA_HEADS, 2 * DA_DH)
    new_v = qkv[:cfg.tp, 2 * d:].reshape(cfg.bp, cfg.lp, DA_HEADS, 2 * DA_DH)
    return x, new_k, new_v


def _ffn_kernel(x_ref, m_ref, g_ref, wg_ref, wh_ref, wo_ref, o_ref, u_scr, acc_scr):
    f = pl.program_id(1)

    @pl.when(f == 0)
    def _():
        u_scr[...] = _modulate(x_ref[...], g_ref[...], m_ref[...], 3).astype(BF16)
        acc_scr[...] = jnp.zeros_like(acc_scr)

    u = u_scr[...]
    a = (_silu(_dot(u, wg_ref[...])) * _dot(u, wh_ref[...])).astype(BF16)
    acc_scr[...] += _dot(a, wo_ref[...])

    @pl.when(f == pl.num_programs(1) - 1)
    def _():
        o_ref[...] = x_ref[...] + m_ref[5:6, :] * acc_scr[...]


def dense_ffn(x, mtab, layer, g, w_in, w_out, cfg, tf=1408):
    t = x.shape[0]
    tm = ROW_TILE
    nf = FF_DIM // tf
    return pl.pallas_call(
        _ffn_kernel,
        grid=(t // tm, nf),
        in_specs=[
            pl.BlockSpec((tm, D_MODEL), lambda i, f: (i, 0)),
            pl.BlockSpec((None, None, 6, D_MODEL), lambda i, f: (layer, _mod_idx(i, cfg, tm), 0, 0)),
            pl.BlockSpec((1, D_MODEL), lambda i, f: (0, 0)),
            pl.BlockSpec((D_MODEL, tf), lambda i, f: (0, f)),
            pl.BlockSpec((D_MODEL, tf), lambda i, f: (0, nf + f)),
            pl.BlockSpec((tf, D_MODEL), lambda i, f: (f, 0)),
        ],
        out_specs=pl.BlockSpec((tm, D_MODEL), lambda i, f: (i, 0)),
        out_shape=jax.ShapeDtypeStruct((t, D_MODEL), F32),
        scratch_shapes=[pltpu.VMEM((tm, D_MODEL), BF16), pltpu.VMEM((tm, D_MODEL), F32)],
        compiler_params=_cparams(("parallel", "arbitrary")),
        name="ffn",
    )(x, mtab, g.reshape(1, D_MODEL), w_in, w_in, w_out)


def _moe_pre_kernel(x_ref, m_ref, g_ref, rw_ref, rb_ref, u_ref, r_ref):
    u = _modulate(x_ref[...], g_ref[...], m_ref[...], 3)
    u_ref[...] = u.astype(BF16)
    logits = _dot_hi(u, rw_ref[...]) + rb_ref[...]
    lane = lax.broadcasted_iota(jnp.int32, logits.shape, 1)
    neg = -jnp.inf
    lg = jnp.where(lane < MOE_E, logits, neg)
    m1 = jnp.max(lg, axis=-1, keepdims=True)
    i1 = jnp.min(jnp.where(lg == m1, lane, LANES), axis=-1, keepdims=True)
    lg2 = jnp.where(lane == i1, neg, lg)
    m2 = jnp.max(lg2, axis=-1, keepdims=True)
    i2 = jnp.min(jnp.where(lg2 == m2, lane, LANES), axis=-1, keepdims=True)
    e2 = jnp.exp(m2 - m1)
    w1 = 1.0 / (1.0 + e2)
    w2 = e2 / (1.0 + e2)
    out = jnp.where(lane == 0, i1.astype(F32),
                    jnp.where(lane == 1, i2.astype(F32),
                              jnp.where(lane == 2, w1, jnp.where(lane == 3, w2, 0.0))))
    r_ref[...] = out


def moe_pre(x, mtab, layer, g, router_w, router_b, cfg):
    t = x.shape[0]
    tm = ROW_TILE
    rw = jnp.zeros((D_MODEL, LANES), F32).at[:, :MOE_E].set(router_w)
    rb = jnp.zeros((1, LANES), F32).at[0, :MOE_E].set(router_b)
    return pl.pallas_call(
        _moe_pre_kernel,
        grid=(t // tm,),
        in_specs=[_tok_spec(tm), _mtab_spec(layer, cfg, tm), pl.BlockSpec((1, D_MODEL), lambda i: (0, 0)),
                  pl.BlockSpec((D_MODEL, LANES), lambda i: (0, 0)), pl.BlockSpec((1, LANES), lambda i: (0, 0))],
        out_specs=[_tok_spec(tm), pl.BlockSpec((tm, LANES), lambda i: (i, 0))],
        out_shape=[jax.ShapeDtypeStruct((t, D_MODEL), BF16), jax.ShapeDtypeStruct((t, LANES), F32)],
        compiler_params=_cparams(("parallel",)),
        name="moe_pre",
    )(x, mtab, g.reshape(1, D_MODEL), rw, rb)


def _moe_ffn_kernel(te_ref, act_ref, x_ref, rw_ref, wg_ref, wh_ref, wo_ref, o_ref, acc_scr):
    i = pl.program_id(0)
    f = pl.program_id(1)
    last = pl.num_programs(1) - 1

    @pl.when(f == 0)
    def _():
        acc_scr[...] = jnp.zeros_like(acc_scr)

    @pl.when(act_ref[i] == 1)
    def _():
        u = x_ref[...]
        a = (_silu(_dot(u, wg_ref[...])) * _dot(u, wh_ref[...])).astype(BF16)
        acc_scr[...] += _dot(a, wo_ref[...])

    @pl.when(f == last)
    def _():
        o_ref[...] = rw_ref[...] * acc_scr[...]


def moe_ffn(xs, row_w, tile_expert, tile_active, w_in, w_out, tf=896):
    npad = xs.shape[0]
    tm = ROW_TILE
    nf = MOE_FF // tf
    grid_spec = pltpu.PrefetchScalarGridSpec(
        num_scalar_prefetch=2,
        grid=(npad // tm, nf),
        in_specs=[
            pl.BlockSpec((tm, D_MODEL), lambda i, f, te, ac: (i, 0)),
            pl.BlockSpec((tm, 1), lambda i, f, te, ac: (i, 0)),
            pl.BlockSpec((None, D_MODEL, tf), lambda i, f, te, ac: (te[i], 0, f)),
            pl.BlockSpec((None, D_MODEL, tf), lambda i, f, te, ac: (te[i], 0, nf + f)),
            pl.BlockSpec((None, tf, D_MODEL), lambda i, f, te, ac: (te[i], f, 0)),
        ],
        out_specs=pl.BlockSpec((tm, D_MODEL), lambda i, f, te, ac: (i, 0)),
        scratch_shapes=[pltpu.VMEM((tm, D_MODEL), F32)],
    )
    return pl.pallas_call(
        _moe_ffn_kernel,
        grid_spec=grid_spec,
        out_shape=jax.ShapeDtypeStruct((npad, D_MODEL), F32),
        compiler_params=_cparams(("parallel", "arbitrary")),
        name="moe_ffn",
    )(tile_expert, tile_active, xs, row_w, w_in, w_in, w_out)


def _res_gate_kernel(y_ref, x_ref, m_ref, o_ref):
    o_ref[...] = x_ref[...] + m_ref[5:6, :] * y_ref[...]


def res_gate(y, x, mtab, layer, cfg):
    t = x.shape[0]
    tm = ROW_TILE
    return pl.pallas_call(
        _res_gate_kernel,
        grid=(t // tm,),
        in_specs=[_tok_spec(tm), _tok_spec(tm), _mtab_spec(layer, cfg, tm)],
        out_specs=_tok_spec(tm),
        out_shape=jax.ShapeDtypeStruct((t, D_MODEL), F32),
        compiler_params=_cparams(("parallel",)),
        name="res_gate",
    )(y, x, mtab)


def moe_layer(x, mtab, layer, g, router_w, router_b, w_in, w_out, cfg):
    t = x.shape[0]
    tm = ROW_TILE
    u, route = moe_pre(x, mtab, layer, g, router_w, router_b, cfg)
    e = jnp.concatenate([route[:, 0], route[:, 1]]).astype(jnp.int32)
    w = jnp.concatenate([route[:, 2], route[:, 3]])
    tok = jnp.concatenate([jnp.arange(t, dtype=jnp.int32)] * 2)
    onehot = (e[:, None] == jnp.arange(MOE_E, dtype=jnp.int32)[None, :]).astype(jnp.int32)
    rank = jnp.sum((jnp.cumsum(onehot, axis=0) - 1) * onehot, axis=1)
    counts = jnp.sum(onehot, axis=0)
    padded = ((counts + tm - 1) // tm) * tm
    ends = jnp.cumsum(padded)
    starts = ends - padded
    dest = starts[e] + rank
    npad = 2 * t + MOE_E * tm
    row_tok = jnp.zeros((npad,), jnp.int32).at[dest].set(tok)
    row_w = jnp.zeros((npad,), F32).at[dest].set(w)
    tile_start = jnp.arange(npad // tm, dtype=jnp.int32) * tm
    tile_active = (tile_start < ends[-1]).astype(jnp.int32)
    tile_expert = jnp.minimum(jnp.searchsorted(ends, tile_start, side='right'), MOE_E - 1).astype(jnp.int32)
    last_e = tile_expert[jnp.maximum(jnp.sum(tile_active) - 1, 0)]
    tile_expert = jnp.where(tile_active == 1, tile_expert, last_e)
    xs = jnp.take(u, row_tok, axis=0)
    ys = moe_ffn(xs, row_w.reshape(npad, 1), tile_expert, tile_active, w_in, w_out)
    y = jnp.take(ys, dest[:t], axis=0) + jnp.take(ys, dest[t:], axis=0)
    return res_gate(y, x, mtab, layer, cfg)


def backbone(x_prompt, x_sample, cache_k, cache_v, state_rglru, state_hgrn, c, c_ctx, p):
    bp, lp, d = x_prompt.shape
    bs, ls, _ = x_sample.shape
    cfg = Cfg(bp, lp, bs, ls)
    x = jnp.concatenate([x_prompt.reshape(bp * lp, d), x_sample.reshape(bs * ls, d)], axis=0)
    cvec = jnp.concatenate([c_ctx[None, :], c], axis=0)
    mtab = modulation_table(cvec, p['mod_w'], p['mod_b'])
    new_k, new_v, new_rg, new_hg = [], [], [], []
    for i in range(DEPTH):
        kind, j = i % 4, i // 4
        g0 = p['norm_g'][i, 0]
        if kind == 0:
            x, st = rglru_layer(x, mtab, i, j, g0, p, state_rglru, cfg)
            new_rg.append(st)
        elif kind == 1:
            x, st = hgrn2_layer(x, mtab, i, j, g0, p, state_hgrn, cfg)
            new_hg.append(st)
        elif kind == 2:
            x = hyena_layer(x, mtab, i, j, g0, p, cfg)
        else:
            x, nk, nv = diffattn_layer(x, mtab, i, j, g0, p, cache_k, cache_v, cfg)
            new_k.append(nk)
            new_v.append(nv)
        n = i // 2
        g1 = p['norm_g'][i, 1]
        if i % 2 == 0:
            x = dense_ffn(x, mtab, i, g1, p['ff_w_in'][n].astype(BF16), p['ff_w_out'][n].astype(BF16), cfg)
        else:
            x = moe_layer(x, mtab, i, g1, p['moe_router'][n], p['moe_router_b'][n],
                          p['moe_w_in'][n].astype(BF16), p['moe_w_out'][n].astype(BF16), cfg)
    y_prompt = x[:cfg.tp].reshape(bp, lp, d)
    y_sample = x[cfg.tp:].reshape(bs, ls, d)
    return (y_prompt, y_sample,
            jnp.stack(new_k, axis=1), jnp.stack(new_v, axis=1),
            jnp.stack(new_rg, axis=1), jnp.stack(new_hg, axis=1))


def kernel(x_prompt, x_sample, cache_k, cache_v, state_rglru, state_hgrn, c, c_ctx, mod_w, mod_b, norm_g, hgrn_lb, rg_w_in, rg_conv_w, rg_conv_b, rg_w_a, rg_b_a, rg_w_x, rg_b_x, rg_lambda, rg_w_out, hg_w_in, hg_norm_g, hg_w_out, hy_w_in, hy_b_in, hy_conv_w, hy_conv_b, hy_f_w1, hy_f_b1, hy_f_w2, hy_f_b2, hy_f_w3, hy_freq, hy_decay, hy_bias, hy_w_out, hy_b_out, da_w_in, da_q_norm, da_k_norm, da_lambda, da_sub_norm, da_w_out, ff_w_in, ff_w_out, moe_router, moe_router_b, moe_w_in, moe_w_out):
    p = dict(mod_w=mod_w, mod_b=mod_b, norm_g=norm_g, hgrn_lb=hgrn_lb,
             rg_w_in=rg_w_in, rg_conv_w=rg_conv_w, rg_conv_b=rg_conv_b, rg_w_a=rg_w_a, rg_b_a=rg_b_a,
             rg_w_x=rg_w_x, rg_b_x=rg_b_x, rg_lambda=rg_lambda, rg_w_out=rg_w_out,
             hg_w_in=hg_w_in, hg_norm_g=hg_norm_g, hg_w_out=hg_w_out,
             hy_w_in=hy_w_in, hy_b_in=hy_b_in, hy_conv_w=hy_conv_w, hy_conv_b=hy_conv_b,
             hy_f_w1=hy_f_w1, hy_f_b1=hy_f_b1, hy_f_w2=hy_f_w2, hy_f_b2=hy_f_b2, hy_f_w3=hy_f_w3,
             hy_freq=hy_freq, hy_decay=hy_decay, hy_bias=hy_bias, hy_w_out=hy_w_out, hy_b_out=hy_b_out,
             da_w_in=da_w_in, da_q_norm=da_q_norm, da_k_norm=da_k_norm, da_lambda=da_lambda,
             da_sub_norm=da_sub_norm, da_w_out=da_w_out,
             ff_w_in=ff_w_in, ff_w_out=ff_w_out, moe_router=moe_router, moe_router_b=moe_router_b,
             moe_w_in=moe_w_in, moe_w_out=moe_w_out)
    return backbone(x_prompt, x_sample, cache_k, cache_v, state_rglru, state_hgrn, c, c_ctx, p)
```

```python
import functools
import math
from typing import NamedTuple

import numpy as np
import jax
import jax.numpy as jnp
from jax import lax
from jax.experimental import pallas as pl
from jax.experimental.pallas import tpu as pltpu

F32 = jnp.float32
BF16 = jnp.bfloat16

D_MODEL = 1024
DEPTH = 4
EPS = 1e-6
GRID_W = 64
RG_HEADS = 4
RG_BW = D_MODEL // RG_HEADS
RG_C = 8.0
HG_HEADS = 8
HG_DK = D_MODEL // HG_HEADS
HG_CHUNK = 32
HY_EMB = 33
HY_FW = 64
DA_HEADS = 8
DA_DH = 64
ROPE_BASE = 10000.0
FF_DIM = 2816
MOE_E = 8
MOE_FF = 3584

LANES = 128
ROW_TILE = 512
MOE_TILE = 1024
SEQ_TILE = 256
SCAN_CHUNK = 64
GLA_HEADS_PER_STEP = 2
VMEM_LIMIT = 56 * 1024 * 1024


class Cfg(NamedTuple):
    bp: int
    lp: int
    bs: int
    ls: int

    @property
    def tp(self):
        return self.bp * self.lp

    @property
    def ts(self):
        return self.bs * self.ls

    @property
    def t(self):
        return self.tp + self.ts


def _cparams(sem):
    return pltpu.CompilerParams(dimension_semantics=sem, vmem_limit_bytes=VMEM_LIMIT)


def _mod_idx(i, cfg, tm):
    npt = cfg.tp // tm
    return jnp.where(i < npt, 0, 1 + (i - npt) // (cfg.ls // tm))


def _modulate(x, g, m, k):
    ms = jnp.mean(x * x, axis=-1, keepdims=True)
    y = x * lax.rsqrt(ms + EPS) * g
    return y * (1.0 + m[k + 1:k + 2, :]) + m[k:k + 1, :]


def _silu(x):
    return x * jax.nn.sigmoid(x)


def _dot(a, b):
    return jnp.dot(a, b, preferred_element_type=F32)


def _dot_nt(a, b):
    return lax.dot_general(a, b, (((1,), (1,)), ((), ())), preferred_element_type=F32)


def _dot_tn(a, b):
    return lax.dot_general(a, b, (((0,), (0,)), ((), ())), preferred_element_type=F32)


def _dot_hi(a, b):
    return jnp.dot(a, b, preferred_element_type=F32, precision=lax.Precision.HIGHEST)


def _modtab_kernel(c_ref, w_ref, b_ref, o_ref):
    s = _silu(c_ref[...]).astype(BF16)
    o_ref[...] = _dot(s, w_ref[...].astype(BF16)) + b_ref[...]


def modulation_table(cvec, mod_w, mod_b):
    n = cvec.shape[0]
    npad = 16
    cpad = jnp.zeros((npad, D_MODEL), F32).at[:n].set(cvec)
    tn = 1536
    out = pl.pallas_call(
        _modtab_kernel,
        grid=(DEPTH, 6 * D_MODEL // tn),
        in_specs=[
            pl.BlockSpec((npad, D_MODEL), lambda l, j: (0, 0)),
            pl.BlockSpec((None, D_MODEL, tn), lambda l, j: (l, 0, j)),
            pl.BlockSpec((None, 1, tn), lambda l, j: (l, 0, j)),
        ],
        out_specs=pl.BlockSpec((None, npad, tn), lambda l, j: (l, 0, j)),
        out_shape=jax.ShapeDtypeStruct((DEPTH, npad, 6 * D_MODEL), F32),
        compiler_params=_cparams(("parallel", "parallel")),
        name="modtab",
    )(cpad, mod_w, mod_b.reshape(DEPTH, 1, 6 * D_MODEL))
    return out[:, :n].reshape(DEPTH, n, 6, D_MODEL)


def _modmm_kernel(*refs, k, has_bias, heads_out):
    if has_bias:
        x_ref, m_ref, g_ref, w_ref, b_ref, o_ref, u_scr = refs
    else:
        x_ref, m_ref, g_ref, w_ref, o_ref, u_scr = refs
        b_ref = None

    @pl.when(pl.program_id(1) == 0)
    def _():
        u_scr[...] = _modulate(x_ref[...], g_ref[...], m_ref[...], k).astype(BF16)

    acc = _dot(u_scr[...], w_ref[...])
    if has_bias:
        acc = acc + b_ref[...]
    if heads_out:
        for hh in range(o_ref.shape[0]):
            o_ref[hh] = acc[:, hh * LANES:(hh + 1) * LANES]
    else:
        o_ref[...] = acc


def mod_matmul(x, mtab, layer, k, g, w, b, cfg, heads_out=False, tn=1024):
    t, n = x.shape[0], w.shape[1]
    tm = ROW_TILE
    in_specs = [
        pl.BlockSpec((tm, D_MODEL), lambda i, j: (i, 0)),
        pl.BlockSpec((None, None, 6, D_MODEL), lambda i, j: (layer, _mod_idx(i, cfg, tm), 0, 0)),
        pl.BlockSpec((1, D_MODEL), lambda i, j: (0, 0)),
        pl.BlockSpec((D_MODEL, tn), lambda i, j: (0, j)),
    ]
    args = [x, mtab, g.reshape(1, D_MODEL), w]
    if b is not None:
        in_specs.append(pl.BlockSpec((1, tn), lambda i, j: (0, j)))
        args.append(b.reshape(1, n))
    if heads_out:
        out_specs = pl.BlockSpec((tn // LANES, tm, LANES), lambda i, j: (j, i, 0))
        out_shape = jax.ShapeDtypeStruct((n // LANES, t, LANES), F32)
    else:
        out_specs = pl.BlockSpec((tm, tn), lambda i, j: (i, j))
        out_shape = jax.ShapeDtypeStruct((t, n), F32)
    return pl.pallas_call(
        functools.partial(_modmm_kernel, k=k, has_bias=b is not None, heads_out=heads_out),
        grid=(t // tm, n // tn),
        in_specs=in_specs,
        out_specs=out_specs,
        out_shape=out_shape,
        scratch_shapes=[pltpu.VMEM((tm, D_MODEL), BF16)],
        compiler_params=_cparams(("parallel", "arbitrary")),
        name="modmm",
    )(*args)


def _seq_edges(i, cfg, lb):
    npb = cfg.tp // lb
    pp, ps = cfg.lp // lb, cfg.ls // lb
    first = jnp.where(i < npb, i % pp == 0, (i - npb) % ps == 0)
    last = jnp.where(i < npb, i % pp == pp - 1, (i - npb) % ps == ps - 1)
    return first, last


def _dwconv_kernel(x_ref, p_ref, n_ref, w_ref, b_ref, o_ref, *, cfg, taps):
    lb = x_ref.shape[0]
    first, last = _seq_edges(pl.program_id(0), cfg, lb)
    prev = jnp.where(first, 0.0, p_ref[...])
    nxt = jnp.where(last, 0.0, n_ref[...])
    ext = jnp.concatenate([prev, x_ref[...], nxt], axis=0)
    left = (taps - 1) // 2
    n_ext = lb + 16
    acc = jnp.zeros(x_ref.shape, F32) + b_ref[...]
    for kk in range(taps):
        sh = (left - kk) % n_ext
        shifted = ext if sh == 0 else pltpu.roll(ext, sh, 0)
        acc = acc + w_ref[kk:kk + 1, :] * shifted[8:8 + lb]
    o_ref[...] = acc


def dwconv(x, col_off, w, b, cfg, tc=1024):
    t = x.shape[0]
    taps, c = w.shape
    lb = SEQ_TILE
    cb = col_off // tc
    r8 = lb // 8
    nblk8 = t // 8
    return pl.pallas_call(
        functools.partial(_dwconv_kernel, cfg=cfg, taps=taps),
        grid=(t // lb, c // tc),
        in_specs=[
            pl.BlockSpec((lb, tc), lambda i, j: (i, cb + j)),
            pl.BlockSpec((8, tc), lambda i, j: (jnp.maximum(i * r8 - 1, 0), cb + j)),
            pl.BlockSpec((8, tc), lambda i, j: (jnp.minimum((i + 1) * r8, nblk8 - 1), cb + j)),
            pl.BlockSpec((taps, tc), lambda i, j: (0, j)),
            pl.BlockSpec((1, tc), lambda i, j: (0, j)),
        ],
        out_specs=pl.BlockSpec((lb, tc), lambda i, j: (i, j)),
        out_shape=jax.ShapeDtypeStruct((t, c), F32),
        compiler_params=_cparams(("parallel", "parallel")),
        name="dwconv",
    )(x, x, x, w, b.reshape(1, c))


def _rg_gates_kernel(xc_ref, wa_ref, wx_ref, ba_ref, bx_ref, lam_ref, a_ref, b_ref):
    xc = xc_ref[...]
    xb = xc.astype(BF16)
    nlam = -lam_ref[...]
    sp = jnp.maximum(nlam, 0.0) + jnp.log1p(jnp.exp(-jnp.abs(nlam)))
    for d in range(2):
        ra = jnp.concatenate(
            [_dot(xb[:, h * RG_BW:(h + 1) * RG_BW], wa_ref[d, h]) for h in range(RG_HEADS)], axis=1)
        rx = jnp.concatenate(
            [_dot(xb[:, h * RG_BW:(h + 1) * RG_BW], wx_ref[d, h]) for h in range(RG_HEADS)], axis=1)
        r = jax.nn.sigmoid(ra + ba_ref[d:d + 1, :])
        ig = jax.nn.sigmoid(rx + bx_ref[d:d + 1, :])
        log_a = (-RG_C) * sp[d:d + 1, :] * r
        a = jnp.exp(log_a)
        gain = jnp.sqrt(-jnp.tanh(log_a) * (1.0 + a * a))
        a_ref[d] = a
        b_ref[d] = gain * ig * xc


def rg_gates(xc, w_a, w_x, b_a, b_x, lam):
    t = xc.shape[0]
    tm = ROW_TILE
    wspec = pl.BlockSpec((2, RG_HEADS, RG_BW, RG_BW), lambda i: (0, 0, 0, 0))
    vspec = pl.BlockSpec((2, D_MODEL), lambda i: (0, 0))
    ospec = pl.BlockSpec((2, tm, D_MODEL), lambda i: (0, i, 0))
    oshape = jax.ShapeDtypeStruct((2, t, D_MODEL), F32)
    return pl.pallas_call(
        _rg_gates_kernel,
        grid=(t // tm,),
        in_specs=[pl.BlockSpec((tm, D_MODEL), lambda i: (i, 0)), wspec, wspec, vspec, vspec, vspec],
        out_specs=[ospec, ospec],
        out_shape=[oshape, oshape],
        compiler_params=_cparams(("parallel",)),
        name="rg_gates",
    )(xc, w_a, w_x, b_a, b_x, lam)


def _rg_scan_kernel(af_ref, bf_ref, ab_ref, bb_ref, h0_ref, hf_ref, hb_ref, hc_scr):
    lc = af_ref.shape[1]

    @pl.when(pl.program_id(1) == 0)
    def _():
        hc_scr[0] = h0_ref[:, 0]
        hc_scr[1] = h0_ref[:, 1]

    def body(tt, carry):
        hf, hb = carry
        hf = af_ref[:, tt] * hf + bf_ref[:, tt]
        hf_ref[:, tt] = hf
        tb = lc - 1 - tt
        hb = ab_ref[:, tb] * hb + bb_ref[:, tb]
        hb_ref[:, tb] = hb
        return hf, hb

    hf, hb = lax.fori_loop(0, lc, body, (hc_scr[0], hc_scr[1]), unroll=4)
    hc_scr[0] = hf
    hc_scr[1] = hb


def rg_scan_group(a, bx, h0, seq0, nseq, length, bsz):
    t = a.shape[1]
    lc = min(SCAN_CHUNK, length)
    nc = length // lc
    sub = D_MODEL // LANES
    av = a.reshape(2, t // length, length, sub, LANES)
    bv = bx.reshape(2, t // length, length, sub, LANES)
    sb0 = seq0 // bsz
    blk = (None, bsz, lc, sub, LANES)
    fwd = lambda g, c: (0, sb0 + g, c, 0, 0)
    bwd = lambda g, c: (1, sb0 + g, nc - 1 - c, 0, 0)
    oblk = (bsz, lc, sub, LANES)
    oshape = jax.ShapeDtypeStruct((nseq, length, sub, LANES), F32)
    hf, hb = pl.pallas_call(
        _rg_scan_kernel,
        grid=(nseq // bsz, nc),
        in_specs=[
            pl.BlockSpec(blk, fwd), pl.BlockSpec(blk, fwd),
            pl.BlockSpec(blk, bwd), pl.BlockSpec(blk, bwd),
            pl.BlockSpec((bsz, 2, sub, LANES), lambda g, c: (g, 0, 0, 0)),
        ],
        out_specs=[
            pl.BlockSpec(oblk, lambda g, c: (g, c, 0, 0)),
            pl.BlockSpec(oblk, lambda g, c: (g, nc - 1 - c, 0, 0)),
        ],
        out_shape=[oshape, oshape],
        scratch_shapes=[pltpu.VMEM((2, bsz, sub, LANES), F32)],
        compiler_params=_cparams(("parallel", "arbitrary")),
        name="rg_scan",
    )(av, bv, av, bv, h0.reshape(nseq, 2, sub, LANES))
    return hf.reshape(nseq * length, D_MODEL), hb.reshape(nseq * length, D_MODEL)


def _gelu_tanh(x):
    return 0.5 * x * (1.0 + jnp.tanh(math.sqrt(2.0 / math.pi) * (x + 0.044715 * (x * x * x))))


def _rg_out_kernel(gate_ref, hf_ref, hb_ref, w_ref, x_ref, m_ref, o_ref):
    y = (_gelu_tanh(gate_ref[...]) * (hf_ref[...] + hb_ref[...])).astype(BF16)
    o_ref[...] = x_ref[...] + m_ref[2:3, :] * _dot(y, w_ref[...])


def _tok_spec(tm, cb=0):
    return pl.BlockSpec((tm, D_MODEL), lambda i: (i, cb))


def _mtab_spec(layer, cfg, tm):
    return pl.BlockSpec((None, None, 6, D_MODEL), lambda i: (layer, _mod_idx(i, cfg, tm), 0, 0))


def _w_spec(k=D_MODEL):
    return pl.BlockSpec((k, D_MODEL), lambda i: (0, 0))


def rg_out(gx, hf, hb, w_out, x, mtab, layer, cfg):
    t = x.shape[0]
    tm = ROW_TILE
    return pl.pallas_call(
        _rg_out_kernel,
        grid=(t // tm,),
        in_specs=[_tok_spec(tm, 0), _tok_spec(tm), _tok_spec(tm), _w_spec(), _tok_spec(tm),
                  _mtab_spec(layer, cfg, tm)],
        out_specs=_tok_spec(tm),
        out_shape=jax.ShapeDtypeStruct((t, D_MODEL), F32),
        compiler_params=_cparams(("parallel",)),
        name="rg_out",
    )(gx, hf, hb, w_out, x, mtab)


def rglru_layer(x, mtab, layer, j, norm_g, p, state_rglru, cfg):
    gx = mod_matmul(x, mtab, layer, 0, norm_g, p['rg_w_in'][j].astype(BF16), None, cfg)
    xc = dwconv(gx, D_MODEL, p['rg_conv_w'][j], p['rg_conv_b'][j], cfg)
    a, bx = rg_gates(xc, p['rg_w_a'][j].astype(BF16), p['rg_w_x'][j].astype(BF16),
                     p['rg_b_a'][j], p['rg_b_x'][j], p['rg_lambda'][j])
    h0p = jnp.zeros((cfg.bp, 2, D_MODEL), F32)
    hfp, hbp = rg_scan_group(a, bx, h0p, 0, cfg.bp, cfg.lp, math.gcd(cfg.bp, 8))
    bsz = math.gcd(math.gcd(cfg.bs, cfg.tp // cfg.ls), 8)
    hfs, hbs = rg_scan_group(a, bx, state_rglru[:, j].astype(F32), cfg.tp // cfg.ls, cfg.bs, cfg.ls, bsz)
    hf = jnp.concatenate([hfp, hfs], axis=0)
    hb = jnp.concatenate([hbp, hbs], axis=0)
    new_state = jnp.stack([hfp.reshape(cfg.bp, cfg.lp, D_MODEL)[:, -1],
                           hbp.reshape(cfg.bp, cfg.lp, D_MODEL)[:, 0]], axis=1)
    x = rg_out(gx, hf, hb, p['rg_w_out'][j].astype(BF16), x, mtab, layer, cfg)
    return x, new_state


def _chunk_cumsum(x, rev):
    rows = x.shape[0]
    r = lax.broadcasted_iota(jnp.int32, x.shape, 0) % HG_CHUNK
    s = 1
    while s < HG_CHUNK:
        if rev:
            x = x + jnp.where(r < HG_CHUNK - s, pltpu.roll(x, rows - s, 0), 0.0)
        else:
            x = x + jnp.where(r >= s, pltpu.roll(x, s, 0), 0.0)
        s *= 2
    return x


def _gla_kernel(fblk, bblk, first, last, seqo, s0i, has0,
                qf_ref, ff_ref, vf_ref, qb_ref, fb_ref, vb_ref, lb_ref, s0_ref,
                of_ref, ob_ref, sfin_ref, s_scr, *, layer):
    i = pl.program_id(1)
    hpb, rows = qf_ref.shape[0], qf_ref.shape[1]
    nch = rows // HG_CHUNK
    units = [(hh, d) for hh in range(hpb) for d in range(2)]

    @pl.when(first[i] == 1)
    def _():
        for hh, d in units:
            s_scr[hh, d] = jnp.where(has0[i] == 1, s0_ref[d, hh].T, 0.0)

    ri = lax.broadcasted_iota(jnp.int32, (rows, rows), 0)
    ci = lax.broadcasted_iota(jnp.int32, (rows, rows), 1)
    same = (ri // HG_CHUNK) == (ci // HG_CHUNK)
    chunks = [slice(n * HG_CHUNK, (n + 1) * HG_CHUNK) for n in range(nch)]
    in_refs = ((qf_ref, ff_ref, vf_ref), (qb_ref, fb_ref, vb_ref))
    out_refs = (of_ref, ob_ref)

    pre = {}
    for hh in range(hpb):
        lbx = lb_ref[hh]
        e = jnp.exp(lbx - jnp.max(lbx, axis=0, keepdims=True))
        sm = e / jnp.sum(e, axis=0, keepdims=True)
        lb = jnp.zeros(lbx.shape[1:], F32)
        for l in range(1, layer + 1):
            lb = lb + sm[l]
        for d in range(2):
            q_ref, f_ref, v_ref = in_refs[d]
            rev = d == 1
            v16 = v_ref[hh].astype(BF16)
            lbd = lb[d:d + 1, :]
            f = lbd + (1.0 - lbd) * jax.nn.sigmoid(f_ref[hh])
            k = 1.0 - f
            b = _chunk_cumsum(jnp.log(f), rev)
            b3 = b.reshape(nch, HG_CHUNK, HG_DK)
            bl = b3[:, 0:1, :] if rev else b3[:, HG_CHUNK - 1:HG_CHUNK, :]
            qi16 = (q_ref[hh] * jnp.exp(b)).astype(BF16)
            ki16 = (k * jnp.exp(-b)).astype(BF16)
            ks16 = (k.reshape(nch, HG_CHUNK, HG_DK) * jnp.exp(bl - b3)).reshape(rows, HG_DK).astype(BF16)
            g = jnp.exp(bl)
            mask = same & ((ci >= ri) if rev else (ci <= ri))
            att = jnp.where(mask, _dot_nt(qi16, ki16), 0.0).astype(BF16)
            ds = [_dot_tn(v16[sl], ks16[sl]) for sl in chunks]
            pre[hh, d] = (qi16, v16, att, ds, g)

    prev = {}
    for hh, d in units:
        _, _, _, ds, g = pre[hh, d]
        st = s_scr[hh, d]
        sp = [None] * nch
        for n in (range(nch - 1, -1, -1) if d == 1 else range(nch)):
            sp[n] = st.astype(BF16)
            st = st * g[n] + ds[n]
        s_scr[hh, d] = st
        prev[hh, d] = sp

    for hh, d in units:
        qi16, v16, att, _, _ = pre[hh, d]
        inter = [_dot_nt(qi16[sl], prev[hh, d][n]) for n, sl in enumerate(chunks)]
        out_refs[d][hh] = _dot(att, v16) + jnp.concatenate(inter, axis=0)

    @pl.when(last[i] == 1)
    def _():
        for hh, d in units:
            sfin_ref[d, hh] = s_scr[hh, d].T


def _gla_tables(cfg):
    rb = SEQ_TILE
    pp, ps = cfg.lp // rb, cfg.ls // rb
    fblk, bblk, first, last, seqo, s0i, has0 = [], [], [], [], [], [], []
    for s in range(cfg.bp):
        for c in range(pp):
            fblk.append(s * pp + c); bblk.append(s * pp + pp - 1 - c)
            first.append(int(c == 0)); last.append(int(c == pp - 1))
            seqo.append(s); s0i.append(0); has0.append(0)
    base = cfg.tp // rb
    for s in range(cfg.bs):
        for c in range(ps):
            fblk.append(base + s * ps + c); bblk.append(base + s * ps + ps - 1 - c)
            first.append(int(c == 0)); last.append(int(c == ps - 1))
            seqo.append(cfg.bp); s0i.append(s); has0.append(1)
    return [jnp.asarray(np.asarray(v, np.int32)) for v in (fblk, bblk, first, last, seqo, s0i, has0)]


def gla(proj, hgrn_lb, s0, layer, cfg):
    t = proj.shape[1]
    rb = SEQ_TILE
    tabs = _gla_tables(cfg)
    nslots = tabs[0].shape[0]
    h8 = HG_HEADS
    hpb = GLA_HEADS_PER_STEP
    ng = h8 // hpb

    def pspec(sec, which):
        return pl.BlockSpec((hpb, rb, HG_DK),
                            lambda h, i, fb, bb, *_: (sec * ng + h, (fb if which == 0 else bb)[i], 0))

    ospec_f = pl.BlockSpec((hpb, rb, HG_DK), lambda h, i, fb, bb, *_: (h, fb[i], 0))
    ospec_b = pl.BlockSpec((hpb, rb, HG_DK), lambda h, i, fb, bb, *_: (h, bb[i], 0))
    grid_spec = pltpu.PrefetchScalarGridSpec(
        num_scalar_prefetch=7,
        grid=(ng, nslots),
        in_specs=[
            pspec(0, 0), pspec(1, 0), pspec(3, 0),
            pspec(0, 1), pspec(2, 1), pspec(3, 1),
            pl.BlockSpec((hpb, DEPTH, 2, HG_DK), lambda h, i, *_: (h, 0, 0, 0)),
            pl.BlockSpec((None, 2, hpb, HG_DK, HG_DK),
                         lambda h, i, fb, bb, fi, la, so, s0i, *_: (s0i[i], 0, h, 0, 0)),
        ],
        out_specs=[
            ospec_f, ospec_b,
            pl.BlockSpec((None, 2, hpb, HG_DK, HG_DK),
                         lambda h, i, fb, bb, fi, la, so, *_: (so[i], 0, h, 0, 0)),
        ],
        scratch_shapes=[pltpu.VMEM((hpb, 2, HG_DK, HG_DK), F32)],
    )
    oshape = jax.ShapeDtypeStruct((h8, t, HG_DK), F32)
    of, ob, sfin = pl.pallas_call(
        functools.partial(_gla_kernel, layer=layer),
        grid_spec=grid_spec,
        out_shape=[oshape, oshape,
                   jax.ShapeDtypeStruct((cfg.bp + 1, 2, h8, HG_DK, HG_DK), F32)],
        compiler_params=_cparams(("parallel", "arbitrary")),
        name="gla",
    )(*tabs, proj, proj, proj, proj, proj, proj,
      hgrn_lb.reshape(DEPTH, 2, h8, HG_DK).transpose(2, 0, 1, 3), s0)
    return of, ob, sfin[:cfg.bp]


def _head_rms(o, g):
    ms = jnp.mean(o * o, axis=-1, keepdims=True)
    return o * lax.rsqrt(ms + EPS) * g


def _hg_out_kernel(of_ref, ob_ref, gh_ref, ng_ref, w_ref, x_ref, m_ref, o_ref):
    parts = []
    for h in range(HG_HEADS):
        o = _head_rms(of_ref[h] + ob_ref[h], ng_ref[...]) * _silu(gh_ref[h])
        parts.append(o.astype(BF16))
    y = jnp.concatenate(parts, axis=1)
    o_ref[...] = x_ref[...] + m_ref[2:3, :] * _dot(y, w_ref[...])


def hg_out(of, ob, proj, norm_g, w_out, x, mtab, layer, cfg):
    t = x.shape[0]
    tm = ROW_TILE
    hspec = pl.BlockSpec((HG_HEADS, tm, HG_DK), lambda i: (0, i, 0))
    return pl.pallas_call(
        _hg_out_kernel,
        grid=(t // tm,),
        in_specs=[hspec, hspec,
                  pl.BlockSpec((HG_HEADS, tm, HG_DK), lambda i: (4, i, 0)),
                  pl.BlockSpec((1, HG_DK), lambda i: (0, 0)),
                  _w_spec(), _tok_spec(tm), _mtab_spec(layer, cfg, tm)],
        out_specs=_tok_spec(tm),
        out_shape=jax.ShapeDtypeStruct((t, D_MODEL), F32),
        compiler_params=_cparams(("parallel",)),
        name="hg_out",
    )(of, ob, proj, norm_g.reshape(1, HG_DK), w_out, x, mtab)


def hgrn2_layer(x, mtab, layer, j, norm_g, p, state_hgrn, cfg):
    proj = mod_matmul(x, mtab, layer, 0, norm_g, p['hg_w_in'][j].astype(BF16), None, cfg, heads_out=True)
    of, ob, sfin = gla(proj, p['hgrn_lb'], state_hgrn[:, j].astype(F32), layer, cfg)
    x = hg_out(of, ob, proj, p['hg_norm_g'][j], p['hg_w_out'][j].astype(BF16), x, mtab, layer, cfg)
    return x, sfin


def _dft_tables(length):
    n = 2 * length
    kk = lax.broadcasted_iota(jnp.int32, (length, length), 0)
    jj = lax.broadcasted_iota(jnp.int32, (length, length), 1)
    ang = ((kk * jj) % n).astype(F32) * (2.0 * math.pi / n)
    a = jnp.cos(ang)
    nyq = jnp.where(jj % 2 == 0, 1.0, -1.0)
    b = jnp.where(kk == 0, nyq, -jnp.sin(ang))
    return a.astype(BF16), b.astype(BF16), b.T.astype(BF16)


def _hy_features(length):
    t_idx = jnp.arange(length, dtype=F32)
    tt = t_idx / (length - 1)
    bands = (HY_EMB - 1) // 2
    fr = jnp.linspace(1e-4, bands - 1, bands, dtype=F32)
    ang = (2.0 * math.pi * t_idx / length)[:, None] * fr[None, :]
    z = jnp.concatenate([tt[:, None], jnp.cos(ang), -jnp.sin(ang)], axis=-1)
    return jnp.zeros((length, LANES), F32).at[:, :HY_EMB].set(z)


def _hy_filter_kernel(z_ref, w1_ref, b1_ref, w2_ref, b2_ref, w3f_ref, w3b_ref, fq_ref, dcf_ref, dcb_ref,
                      a_ref, b_ref, ka_ref, ki_ref, kn_ref):
    length = z_ref.shape[0]
    fq = fq_ref[...]
    h = jnp.sin(fq * (_dot_hi(z_ref[...], w1_ref[...]) + b1_ref[...]))
    h = jnp.sin(fq * (_dot_hi(h, w2_ref[...]) + b2_ref[...]))
    row = lax.broadcasted_iota(jnp.int32, (length, 1), 0)
    tt = row.astype(F32) / float(length - 1)
    hf = _dot_hi(h, w3f_ref[...]) * jnp.exp(-tt * jnp.abs(dcf_ref[...]))
    hb = _dot_hi(h, w3b_ref[...]) * jnp.exp(-tt * jnp.abs(dcb_ref[...]))
    hb = jnp.where(row == 0, 0.0, hb)
    nrm = lax.rsqrt(jnp.sum(hf * hf + hb * hb, axis=0, keepdims=True) + EPS)
    hf = hf * nrm
    hb = hb * nrm
    a = a_ref[...]
    bm = b_ref[...]
    ka_ref[...] = _dot(a, (hf + hb).astype(BF16))
    kbf = _dot(bm, hf.astype(BF16))
    kbb = _dot(bm, hb.astype(BF16))
    ki_ref[...] = jnp.where(row == 0, 0.0, kbf - kbb)
    kn_ref[...] = jnp.broadcast_to(kbf[0:1, :] + kbb[0:1, :], kn_ref.shape)


def _const_spec(shape):
    nd = len(shape)
    return pl.BlockSpec(shape, lambda *_: (0,) * nd, pipeline_mode=pl.Buffered(1))


def hy_filter_spectrum(length, a_tab, b_tab, p, j, tc=256):
    d = D_MODEL
    nct = d // tc
    w1 = jnp.zeros((LANES, HY_FW), F32).at[:HY_EMB].set(p['hy_f_w1'][j])
    w3 = p['hy_f_w3'][j]
    dec = p['hy_decay'][j].reshape(1, 4 * d)
    z = _hy_features(length)
    fwd_col = lambda o, c: (0, (o * 2) * nct + c)
    bwd_col = lambda o, c: (0, (o * 2 + 1) * nct + c)
    return pl.pallas_call(
        _hy_filter_kernel,
        grid=(2, nct),
        in_specs=[
            _const_spec((length, LANES)), _const_spec((LANES, HY_FW)), _const_spec((1, HY_FW)),
            _const_spec((HY_FW, HY_FW)), _const_spec((1, HY_FW)),
            pl.BlockSpec((HY_FW, tc), fwd_col), pl.BlockSpec((HY_FW, tc), bwd_col),
            _const_spec((1, HY_FW)),
            pl.BlockSpec((1, tc), fwd_col), pl.BlockSpec((1, tc), bwd_col),
            _const_spec((length, length)), _const_spec((length, length)),
        ],
        out_specs=[
            pl.BlockSpec((None, length, tc), lambda o, c: (o, 0, c)),
            pl.BlockSpec((None, length, tc), lambda o, c: (o, 0, c)),
            pl.BlockSpec((None, 8, tc), lambda o, c: (o, 0, c)),
        ],
        out_shape=[jax.ShapeDtypeStruct((2, length, d), F32),
                   jax.ShapeDtypeStruct((2, length, d), F32),
                   jax.ShapeDtypeStruct((2, 8, d), F32)],
        compiler_params=_cparams(("parallel", "parallel")),
        name="hy_filter",
    )(z, w1, p['hy_f_b1'][j].reshape(1, HY_FW), p['hy_f_w2'][j], p['hy_f_b2'][j].reshape(1, HY_FW),
      w3, w3, p['hy_freq'][j].reshape(1, HY_FW), dec, dec, a_tab, b_tab)


def _sconv_kernel(z_ref, x_ref, a_ref, b_ref, bt_ref, ka_ref, ki_ref, kn_ref, bias_ref, o_ref,
                  yr_scr, yi_scr, *, fb):
    length = z_ref.shape[0]
    zb = z_ref[...].astype(BF16)
    inv = 1.0 / (2 * length)
    for kf in range(length // fb):
        rows = slice(kf * fb, (kf + 1) * fb)
        pr = _dot(a_ref[rows, :], zb)
        qi = _dot(b_ref[rows, :], zb)
        ka = ka_ref[rows, :]
        ki = ki_ref[rows, :]
        if kf == 0:
            r0 = lax.broadcasted_iota(jnp.int32, (fb, 1), 0) == 0
            kd = jnp.where(r0, kn_ref[0:1, :], ka)
            wgt = jnp.where(r0, inv, 2.0 * inv)
        else:
            kd = ka
            wgt = 2.0 * inv
        yr_scr[rows, :] = ((pr * ka - qi * ki) * wgt).astype(BF16)
        yi_scr[rows, :] = ((pr * ki + qi * kd) * wgt).astype(BF16)
    for tb in range(length // fb):
        rows = slice(tb * fb, (tb + 1) * fb)
        y = _dot(a_ref[rows, :], yr_scr[...]) + _dot(bt_ref[rows, :], yi_scr[...])
        o_ref[rows, :] = x_ref[rows, :] * (y + bias_ref[...] * z_ref[rows, :])


def sconv_group(zarr, zcol, zseq0, xarr, xcol, xseq0, nseq, length, tabs, ka, ki, kn, order, bias, tc=256):
    a_tab, b_tab, bt_tab = tabs
    d = D_MODEL
    fb = min(256, length)
    zc, xc = zcol // tc, xcol // tc
    return pl.pallas_call(
        functools.partial(_sconv_kernel, fb=fb),
        grid=(nseq, d // tc),
        in_specs=[
            pl.BlockSpec((length, tc), lambda s, c: (zseq0 + s, zc + c)),
            pl.BlockSpec((length, tc), lambda s, c: (xseq0 + s, xc + c)),
            _const_spec((length, length)), _const_spec((length, length)), _const_spec((length, length)),
            pl.BlockSpec((None, length, tc), lambda s, c: (order, 0, c)),
            pl.BlockSpec((None, length, tc), lambda s, c: (order, 0, c)),
            pl.BlockSpec((None, 8, tc), lambda s, c: (order, 0, c)),
            pl.BlockSpec((None, 1, tc), lambda s, c: (order, 0, c)),
        ],
        out_specs=pl.BlockSpec((length, tc), lambda s, c: (s, c)),
        out_shape=jax.ShapeDtypeStruct((nseq * length, d), F32),
        scratch_shapes=[pltpu.VMEM((length, tc), BF16), pltpu.VMEM((length, tc), BF16)],
        compiler_params=_cparams(("parallel", "parallel")),
        name="sconv",
    )(zarr, xarr, a_tab, b_tab, bt_tab, ka, ki, kn, bias.reshape(2, 1, d))


def _lin_out_kernel(a_ref, w_ref, b_ref, x_ref, m_ref, o_ref):
    y = _dot(a_ref[...].astype(BF16), w_ref[...]) + b_ref[...]
    o_ref[...] = x_ref[...] + m_ref[2:3, :] * y


def lin_out(a, w, b, x, mtab, layer, cfg):
    t = x.shape[0]
    tm = ROW_TILE
    return pl.pallas_call(
        _lin_out_kernel,
        grid=(t // tm,),
        in_specs=[_tok_spec(tm), _w_spec(), pl.BlockSpec((1, D_MODEL), lambda i: (0, 0)),
                  _tok_spec(tm), _mtab_spec(layer, cfg, tm)],
        out_specs=_tok_spec(tm),
        out_shape=jax.ShapeDtypeStruct((t, D_MODEL), F32),
        compiler_params=_cparams(("parallel",)),
        name="lin_out",
    )(a, w, b.reshape(1, D_MODEL), x, mtab)


def hyena_layer(x, mtab, layer, j, norm_g, p, cfg):
    d = D_MODEL
    pre = mod_matmul(x, mtab, layer, 0, norm_g, p['hy_w_in'][j].astype(BF16), p['hy_b_in'][j], cfg)
    pc = dwconv(pre, 0, p['hy_conv_w'][j], p['hy_conv_b'][j], cfg)
    bias = p['hy_bias'][j]
    outs = []
    for (seq0_rows, nseq, length) in ((0, cfg.bp, cfg.lp), (cfg.tp, cfg.bs, cfg.ls)):
        tabs = _dft_tables(length)
        ka, ki, kn = hy_filter_spectrum(length, tabs[0], tabs[1], p, j)
        seq0 = seq0_rows // length
        tc = 256 if length > 512 else d
        z1 = sconv_group(pc, 0, seq0, pc, d, seq0, nseq, length, tabs, ka, ki, kn, 0, bias, tc)
        z2 = sconv_group(z1, 0, 0, pc, 2 * d, seq0, nseq, length, tabs, ka, ki, kn, 1, bias, tc)
        outs.append(z2)
    z = jnp.concatenate(outs, axis=0)
    return lin_out(z, p['hy_w_out'][j].astype(BF16), p['hy_b_out'][j], x, mtab, layer, cfg)


def _group_rms_scale(x, g1, g1t):
    x2 = x * x
    hi = x2.astype(BF16)
    lo = (x2 - hi.astype(F32)).astype(BF16)
    s = _dot(hi, g1) + _dot(lo, g1)
    r = lax.rsqrt(s * (1.0 / DA_DH) + EPS)
    rhi = r.astype(BF16)
    rlo = (r - rhi.astype(F32)).astype(BF16)
    return _dot(rhi, g1t) + _dot(rlo, g1t)


def _rope(x, cos, sin_signed):
    lane = lax.broadcasted_iota(jnp.int32, x.shape, 1)
    w = x.shape[1]
    nf = DA_DH // 4
    rot = jnp.where(lane % (2 * nf) < nf, pltpu.roll(x, w - nf, 1), pltpu.roll(x, nf, 1))
    return x * cos + rot * sin_signed


def _qk_prep_kernel(q_ref, k_ref, v_ref, qg_ref, kg_ref, g1_ref, g1t_ref, cos_ref, sin_ref,
                    qh_ref, kh_ref, vh_ref, kc_ref, *, cfg):
    tm = q_ref.shape[0]
    i = pl.program_id(0)
    is_prompt = i < cfg.tp // tm
    g1 = g1_ref[...]
    g1t = g1t_ref[...]
    q = q_ref[...]
    k = k_ref[...]
    qn = q * _group_rms_scale(q, g1, g1t) * qg_ref[...]
    kn = k * _group_rms_scale(k, g1, g1t) * kg_ref[...]

    @pl.when(is_prompt)
    def _():
        kc_ref[...] = kn

    cos = cos_ref[...]
    sin = sin_ref[...]
    qo = (jnp.where(is_prompt, qn, _rope(qn, cos, sin)) * (DA_DH ** -0.5)).astype(BF16)
    ko = jnp.where(is_prompt, kn, _rope(kn, cos, sin)).astype(BF16)
    vo = v_ref[...].astype(BF16)
    for h in range(DA_HEADS):
        sl = slice(h * LANES, (h + 1) * LANES)
        qh_ref[h] = qo[:, sl]
        kh_ref[h] = ko[:, sl]
        vh_ref[h] = vo[:, sl]


def _rope_tables(length):
    nf = DA_DH // 4
    t = jnp.arange(length)
    pos = jnp.stack([t // GRID_W, t % GRID_W], axis=-1).astype(F32)
    inv = ROPE_BASE ** (-jnp.arange(nf, dtype=F32) / nf)
    ang = pos[:, :, None] * inv
    cos = jnp.cos(ang)[:, :, None, :]
    sin = jnp.sin(ang)[:, :, None, :]
    cos = jnp.broadcast_to(cos, (length, 2, 2, nf)).reshape(length, DA_DH)
    sin = jnp.concatenate([-sin, sin], axis=2).reshape(length, DA_DH)
    reps = D_MODEL // DA_DH
    return jnp.tile(cos, (1, reps)), jnp.tile(sin, (1, reps))


def qk_prep(qkv, q_g, k_g, cfg):
    t = qkv.shape[0]
    tm = ROW_TILE
    d = D_MODEL
    ngrp = d // DA_DH
    grp = jnp.arange(d) // DA_DH
    g1 = (grp[:, None] == jnp.arange(LANES)[None, :]).astype(BF16)
    g1t = g1.T
    cos, sin = _rope_tables(cfg.ls)
    npt = cfg.tp // tm
    pps = cfg.ls // tm
    tab_spec = pl.BlockSpec((tm, d), lambda i: (jnp.maximum(i - npt, 0) % pps, 0))
    hspec = pl.BlockSpec((DA_HEADS, tm, LANES), lambda i: (0, i, 0))
    hshape = jax.ShapeDtypeStruct((DA_HEADS, t, LANES), BF16)
    return pl.pallas_call(
        functools.partial(_qk_prep_kernel, cfg=cfg),
        grid=(t // tm,),
        in_specs=[_tok_spec(tm, 0), _tok_spec(tm, 1), _tok_spec(tm, 2),
                  pl.BlockSpec((1, d), lambda i: (0, 0)), pl.BlockSpec((1, d), lambda i: (0, 0)),
                  pl.BlockSpec((d, LANES), lambda i: (0, 0)), pl.BlockSpec((LANES, d), lambda i: (0, 0)),
                  tab_spec, tab_spec],
        out_specs=[hspec, hspec, hspec,
                   pl.BlockSpec((tm, d), lambda i: (jnp.minimum(i, npt - 1), 0))],
        out_shape=[hshape, hshape, hshape, jax.ShapeDtypeStruct((cfg.tp, d), F32)],
        compiler_params=_cparams(("arbitrary",)),
        name="qk_prep",
    )(qkv, qkv, qkv, jnp.tile(q_g, ngrp).reshape(1, d), jnp.tile(k_g, ngrp).reshape(1, d),
      g1, g1t, cos, sin)


def _dattn_kernel(*refs, has_cache, lam_init):
    if has_cache:
        lp_ref, q_ref, k_ref, v_ref, ck_ref, cv_ref, o_ref = refs
    else:
        lp_ref, q_ref, k_ref, v_ref, o_ref = refs
    lp = lp_ref[...]
    lam = (jnp.exp(jnp.sum(lp[0:1] * lp[1:2], axis=1, keepdims=True))
           - jnp.exp(jnp.sum(lp[2:3] * lp[3:4], axis=1, keepdims=True)) + lam_init)
    q = q_ref[...]
    lane = lax.broadcasted_iota(jnp.int32, q.shape, 1)
    zero = jnp.zeros_like(q)
    qs = (jnp.where(lane < DA_DH, q, zero), jnp.where(lane >= DA_DH, q, zero))
    k = k_ref[...]
    v = v_ref[...]
    if has_cache:
        ck = ck_ref[...].astype(BF16)
        cv = cv_ref[...].astype(BF16)
    acc_l = None
    acc_c = None
    for half in range(2):
        s = _dot_nt(qs[half], k)
        m = jnp.max(s, axis=-1, keepdims=True)
        if has_cache:
            c = _dot_nt(qs[half], ck)
            m = jnp.maximum(m, jnp.max(c, axis=-1, keepdims=True))
        e = jnp.exp(s - m)
        z = jnp.sum(e, axis=-1, keepdims=True)
        if has_cache:
            ec = jnp.exp(c - m)
            z = z + jnp.sum(ec, axis=-1, keepdims=True)
        coef = (1.0 / z) if half == 0 else (-lam / z)
        acc_l = e * coef if acc_l is None else acc_l + e * coef
        if has_cache:
            acc_c = ec * coef if acc_c is None else acc_c + ec * coef
    o = _dot(acc_l.astype(BF16), v)
    if has_cache:
        o = o + _dot(acc_c.astype(BF16), cv)
    o_ref[...] = o


def dattn_group(qh, kh, vh, da_lambda, lam_init, seq0, nseq, length, cache_k=None, cache_v=None, tq=256):
    tq = min(tq, length)
    nq = length // tq
    has_cache = cache_k is not None
    in_specs = [
        pl.BlockSpec((4, DA_DH), lambda b, h, qi: (0, 0)),
        pl.BlockSpec((None, tq, LANES), lambda b, h, qi: (h, (seq0 + b) * nq + qi, 0)),
        pl.BlockSpec((None, length, LANES), lambda b, h, qi: (h, seq0 + b, 0)),
        pl.BlockSpec((None, length, LANES), lambda b, h, qi: (h, seq0 + b, 0)),
    ]
    args = [da_lambda, qh, kh, vh]
    if has_cache:
        past = cache_k.shape[1]
        cspec = pl.BlockSpec((None, past, LANES), lambda b, h, qi: (b, 0, h))
        in_specs += [cspec, cspec]
        args += [cache_k.reshape(nseq, past, DA_HEADS * LANES), cache_v.reshape(nseq, past, DA_HEADS * LANES)]
    return pl.pallas_call(
        functools.partial(_dattn_kernel, has_cache=has_cache, lam_init=lam_init),
        grid=(nseq, DA_HEADS, nq),
        in_specs=in_specs,
        out_specs=pl.BlockSpec((None, tq, LANES), lambda b, h, qi: (h, b * nq + qi, 0)),
        out_shape=jax.ShapeDtypeStruct((DA_HEADS, nseq * length, LANES), F32),
        compiler_params=_cparams(("parallel", "parallel", "parallel")),
        name="dattn",
    )(*args)


def _da_out_kernel(o_ref, sg_ref, w_ref, x_ref, m_ref, out_ref, *, lam_init):
    parts = []
    for h in range(DA_HEADS):
        parts.append((_head_rms(o_ref[h], sg_ref[...]) * (1.0 - lam_init)).astype(BF16))
    y = jnp.concatenate(parts, axis=1)
    out_ref[...] = x_ref[...] + m_ref[2:3, :] * _dot(y, w_ref[...])


def da_out(o, sub_g, w_out, x, mtab, layer, lam_init, cfg):
    t = x.shape[0]
    tm = ROW_TILE
    return pl.pallas_call(
        functools.partial(_da_out_kernel, lam_init=lam_init),
        grid=(t // tm,),
        in_specs=[pl.BlockSpec((DA_HEADS, tm, LANES), lambda i: (0, i, 0)),
                  pl.BlockSpec((1, LANES), lambda i: (0, 0)),
                  _w_spec(), _tok_spec(tm), _mtab_spec(layer, cfg, tm)],
        out_specs=_tok_spec(tm),
        out_shape=jax.ShapeDtypeStruct((t, D_MODEL), F32),
        compiler_params=_cparams(("parallel",)),
        name="da_out",
    )(o, sub_g.reshape(1, LANES), w_out, x, mtab)


def diffattn_layer(x, mtab, layer, j, norm_g, p, cache_k, cache_v, cfg):
    d = D_MODEL
    lam_init = 0.8 - 0.6 * math.exp(-0.3 * layer)
    qkv = mod_matmul(x, mtab, layer, 0, norm_g, p['da_w_in'][j].astype(BF16), None, cfg)
    qh, kh, vh, kc = qk_prep(qkv, p['da_q_norm'][j], p['da_k_norm'][j], cfg)
    lamp = p['da_lambda'][j].astype(F32)
    op = dattn_group(qh, kh, vh, lamp, lam_init, 0, cfg.bp, cfg.lp)
    os_ = dattn_group(qh, kh, vh, lamp, lam_init, cfg.tp // cfg.ls, cfg.bs, cfg.ls,
                      cache_k[:, j], cache_v[:, j])
    o = jnp.concatenate([op, os_], axis=1)
    x = da_out(o, p['da_sub_norm'][j], p['da_w_out'][j].astype(BF16), x, mtab, layer, lam_init, cfg)
    new_k = kc.reshape(cfg.bp, cfg.lp, DA_HEADS, 2 * DA_DH)
    new_v = qkv[:cfg.tp, 2 * d:].reshape(cfg.bp, cfg.lp, DA_HEADS, 2 * DA_DH)
    return x, new_k, new_v


def _ffn_kernel(x_ref, m_ref, g_ref, wg_ref, wh_ref, wo_ref, o_ref, u_scr, acc_scr):
    f = pl.program_id(1)

    @pl.when(f == 0)
    def _():
        u_scr[...] = _modulate(x_ref[...], g_ref[...], m_ref[...], 3).astype(BF16)
        acc_scr[...] = jnp.zeros_like(acc_scr)

    u = u_scr[...]
    a = (_silu(_dot(u, wg_ref[...])) * _dot(u, wh_ref[...])).astype(BF16)
    acc_scr[...] += _dot(a, wo_ref[...])

    @pl.when(f == pl.num_programs(1) - 1)
    def _():
        o_ref[...] = x_ref[...] + m_ref[5:6, :] * acc_scr[...]


def dense_ffn(x, mtab, layer, g, w_in, w_out, cfg, tf=1408):
    t = x.shape[0]
    tm = ROW_TILE
    nf = FF_DIM // tf
    return pl.pallas_call(
        _ffn_kernel,
        grid=(t // tm, nf),
        in_specs=[
            pl.BlockSpec((tm, D_MODEL), lambda i, f: (i, 0)),
            pl.BlockSpec((None, None, 6, D_MODEL), lambda i, f: (layer, _mod_idx(i, cfg, tm), 0, 0)),
            pl.BlockSpec((1, D_MODEL), lambda i, f: (0, 0)),
            pl.BlockSpec((D_MODEL, tf), lambda i, f: (0, f)),
            pl.BlockSpec((D_MODEL, tf), lambda i, f: (0, nf + f)),
            pl.BlockSpec((tf, D_MODEL), lambda i, f: (f, 0)),
        ],
        out_specs=pl.BlockSpec((tm, D_MODEL), lambda i, f: (i, 0)),
        out_shape=jax.ShapeDtypeStruct((t, D_MODEL), F32),
        scratch_shapes=[pltpu.VMEM((tm, D_MODEL), BF16), pltpu.VMEM((tm, D_MODEL), F32)],
        compiler_params=_cparams(("parallel", "arbitrary")),
        name="ffn",
    )(x, mtab, g.reshape(1, D_MODEL), w_in, w_in, w_out)


def _moe_pre_kernel(x_ref, m_ref, g_ref, rw_ref, rb_ref, u_ref, r_ref, c_ref, cnt_scr):
    tm = x_ref.shape[0]

    @pl.when(pl.program_id(0) == 0)
    def _():
        cnt_scr[...] = jnp.zeros_like(cnt_scr)

    u = _modulate(x_ref[...], g_ref[...], m_ref[...], 3)
    bits = pltpu.bitcast(u.astype(BF16).astype(F32), jnp.uint32)
    half = D_MODEL // 2
    u_ref[...] = (bits[:, :half] >> 16) | (bits[:, half:] & jnp.uint32(0xFFFF0000))
    logits = _dot_hi(u, rw_ref[...]) + rb_ref[...]
    lane = lax.broadcasted_iota(jnp.int32, logits.shape, 1)
    neg = -jnp.inf
    lg = jnp.where(lane < MOE_E, logits, neg)
    m1 = jnp.max(lg, axis=-1, keepdims=True)
    i1 = jnp.min(jnp.where(lg == m1, lane, LANES), axis=-1, keepdims=True)
    lg2 = jnp.where(lane == i1, neg, lg)
    m2 = jnp.max(lg2, axis=-1, keepdims=True)
    i2 = jnp.min(jnp.where(lg2 == m2, lane, LANES), axis=-1, keepdims=True)
    e2 = jnp.exp(m2 - m1)
    w1 = 1.0 / (1.0 + e2)
    w2 = e2 / (1.0 + e2)
    hit = jnp.where((lane == i1) | (lane == i2), 1.0, 0.0)
    ri = lax.broadcasted_iota(jnp.int32, (tm, tm), 0)
    ci = lax.broadcasted_iota(jnp.int32, (tm, tm), 1)
    ahead = jnp.where(ci < ri, 1.0, 0.0).astype(BF16)
    pos = _dot(ahead, hit.astype(BF16)) + cnt_scr[...]
    r1 = jnp.sum(jnp.where(lane == i1, pos, 0.0), axis=-1, keepdims=True)
    r2 = jnp.sum(jnp.where(lane == i2, pos, 0.0), axis=-1, keepdims=True)
    cnt = cnt_scr[...] + jnp.sum(hit, axis=0, keepdims=True)
    cnt_scr[...] = cnt
    c_ref[...] = jnp.broadcast_to(cnt, c_ref.shape)
    vals = (i1.astype(F32), i2.astype(F32), w1, w2, r1, r2)
    out = jnp.zeros(logits.shape, F32)
    for col, val in enumerate(vals):
        out = jnp.where(lane == col, val, out)
    r_ref[...] = out


def moe_pre(x, mtab, layer, g, router_w, router_b, cfg):
    t = x.shape[0]
    tm = ROW_TILE
    rw = jnp.zeros((D_MODEL, LANES), F32).at[:, :MOE_E].set(router_w)
    rb = jnp.zeros((1, LANES), F32).at[0, :MOE_E].set(router_b)
    return pl.pallas_call(
        _moe_pre_kernel,
        grid=(t // tm,),
        in_specs=[_tok_spec(tm), _mtab_spec(layer, cfg, tm), pl.BlockSpec((1, D_MODEL), lambda i: (0, 0)),
                  pl.BlockSpec((D_MODEL, LANES), lambda i: (0, 0)), pl.BlockSpec((1, LANES), lambda i: (0, 0))],
        out_specs=[pl.BlockSpec((tm, D_MODEL // 2), lambda i: (i, 0)),
                   pl.BlockSpec((tm, LANES), lambda i: (i, 0)),
                   pl.BlockSpec((8, LANES), lambda i: (0, 0))],
        out_shape=[jax.ShapeDtypeStruct((t, D_MODEL // 2), jnp.uint32),
                   jax.ShapeDtypeStruct((t, LANES), F32),
                   jax.ShapeDtypeStruct((8, LANES), F32)],
        scratch_shapes=[pltpu.VMEM((1, LANES), F32)],
        compiler_params=_cparams(("arbitrary",)),
        name="moe_pre",
    )(x, mtab, g.reshape(1, D_MODEL), rw, rb)


def _moe_ffn_kernel(te_ref, act_ref, x_ref, wg_ref, wh_ref, wo_ref, o_ref, xb_scr, acc_scr):
    i = pl.program_id(0)
    f = pl.program_id(1)
    active = act_ref[i] == 1

    @pl.when(f == 0)
    def _():
        acc_scr[...] = jnp.zeros_like(acc_scr)

    @pl.when(active & (f == 0))
    def _():
        w = x_ref[...]
        lo = pltpu.bitcast(w << 16, F32)
        hi = pltpu.bitcast(w & jnp.uint32(0xFFFF0000), F32)
        xb_scr[...] = jnp.concatenate([lo, hi], axis=1).astype(BF16)

    @pl.when(active)
    def _():
        u = xb_scr[...]
        tf = wg_ref.shape[1]
        sub = 256
        part = None
        for c in range(tf // sub):
            cs = slice(c * sub, (c + 1) * sub)
            a = (_silu(_dot(u, wg_ref[:, cs].astype(BF16)))
                 * _dot(u, wh_ref[:, cs].astype(BF16))).astype(BF16)
            y = _dot(a, wo_ref[cs, :].astype(BF16))
            part = y if part is None else part + y
        acc_scr[...] += part

    @pl.when(f == pl.num_programs(1) - 1)
    def _():
        o_ref[...] = acc_scr[...]


def moe_ffn(xs, tile_expert, tile_active, w_in, w_out, n, tf=512):
    npad = xs.shape[0]
    tm = MOE_TILE
    nf = MOE_FF // tf
    grid_spec = pltpu.PrefetchScalarGridSpec(
        num_scalar_prefetch=2,
        grid=(npad // tm, nf),
        in_specs=[
            pl.BlockSpec((tm, D_MODEL // 2), lambda i, f, te, ac: (i, 0)),
            pl.BlockSpec((None, None, D_MODEL, tf), lambda i, f, te, ac: (n, te[i], 0, f)),
            pl.BlockSpec((None, None, D_MODEL, tf), lambda i, f, te, ac: (n, te[i], 0, nf + f)),
            pl.BlockSpec((None, None, tf, D_MODEL), lambda i, f, te, ac: (n, te[i], f, 0)),
        ],
        out_specs=pl.BlockSpec((tm, D_MODEL), lambda i, f, te, ac: (i, 0)),
        scratch_shapes=[pltpu.VMEM((tm, D_MODEL), BF16), pltpu.VMEM((tm, D_MODEL), F32)],
    )
    return pl.pallas_call(
        _moe_ffn_kernel,
        grid_spec=grid_spec,
        out_shape=jax.ShapeDtypeStruct((npad, D_MODEL), F32),
        compiler_params=_cparams(("parallel", "arbitrary")),
        name="moe_ffn",
    )(tile_expert, tile_active, xs, w_in, w_in, w_out)


def _moe_combine_kernel(y1_ref, y2_ref, r_ref, x_ref, m_ref, o_ref):
    r = r_ref[...]
    y = r[:, 2:3] * y1_ref[...] + r[:, 3:4] * y2_ref[...]
    o_ref[...] = x_ref[...] + m_ref[5:6, :] * y


def moe_combine(y1, y2, route, x, mtab, layer, cfg):
    t = x.shape[0]
    tm = ROW_TILE
    return pl.pallas_call(
        _moe_combine_kernel,
        grid=(t // tm,),
        in_specs=[_tok_spec(tm), _tok_spec(tm), pl.BlockSpec((tm, LANES), lambda i: (i, 0)),
                  _tok_spec(tm), _mtab_spec(layer, cfg, tm)],
        out_specs=_tok_spec(tm),
        out_shape=jax.ShapeDtypeStruct((t, D_MODEL), F32),
        compiler_params=_cparams(("parallel",)),
        name="moe_combine",
    )(y1, y2, route, x, mtab)


def moe_layer(x, mtab, layer, g, router_w, router_b, w_in, w_out, n, cfg):
    t = x.shape[0]
    tm = MOE_TILE
    u, route, cnt = moe_pre(x, mtab, layer, g, router_w, router_b, cfg)
    counts = cnt[0, :MOE_E].astype(jnp.int32)
    padded = ((counts + tm - 1) // tm) * tm
    ends = jnp.cumsum(padded)
    starts = ends - padded
    e = route[:, 0:2].astype(jnp.int32)
    dest = starts[e] + route[:, 4:6].astype(jnp.int32)
    npad = 2 * t + MOE_E * tm
    tok = jnp.arange(t, dtype=jnp.int32)
    row_tok = jnp.zeros((npad,), jnp.int32).at[jnp.concatenate([dest[:, 0], dest[:, 1]])].set(
        jnp.concatenate([tok, tok]))
    tile_start = jnp.arange(npad // tm, dtype=jnp.int32) * tm
    tile_active = (tile_start < ends[-1]).astype(jnp.int32)
    tile_expert = jnp.minimum(jnp.sum((tile_start[:, None] >= ends[None, :]).astype(jnp.int32), axis=1),
                              MOE_E - 1)
    last_e = tile_expert[jnp.maximum(jnp.sum(tile_active) - 1, 0)]
    tile_expert = jnp.where(tile_active == 1, tile_expert, last_e)
    xs = jnp.take(u, row_tok, axis=0)
    ys = moe_ffn(xs, tile_expert, tile_active, w_in, w_out, n)
    y1 = jnp.take(ys, dest[:, 0], axis=0)
    y2 = jnp.take(ys, dest[:, 1], axis=0)
    return moe_combine(y1, y2, route, x, mtab, layer, cfg)


def backbone(x_prompt, x_sample, cache_k, cache_v, state_rglru, state_hgrn, c, c_ctx, p):
    bp, lp, d = x_prompt.shape
    bs, ls, _ = x_sample.shape
    cfg = Cfg(bp, lp, bs, ls)
    x = jnp.concatenate([x_prompt.reshape(bp * lp, d), x_sample.reshape(bs * ls, d)], axis=0)
    cvec = jnp.concatenate([c_ctx[None, :], c], axis=0)
    mtab = modulation_table(cvec, p['mod_w'], p['mod_b'])
    new_k, new_v, new_rg, new_hg = [], [], [], []
    for i in range(DEPTH):
        kind, j = i % 4, i // 4
        g0 = p['norm_g'][i, 0]
        if kind == 0:
            x, st = rglru_layer(x, mtab, i, j, g0, p, state_rglru, cfg)
            new_rg.append(st)
        elif kind == 1:
            x, st = hgrn2_layer(x, mtab, i, j, g0, p, state_hgrn, cfg)
            new_hg.append(st)
        elif kind == 2:
            x = hyena_layer(x, mtab, i, j, g0, p, cfg)
        else:
            x, nk, nv = diffattn_layer(x, mtab, i, j, g0, p, cache_k, cache_v, cfg)
            new_k.append(nk)
            new_v.append(nv)
        n = i // 2
        g1 = p['norm_g'][i, 1]
        if i % 2 == 0:
            x = dense_ffn(x, mtab, i, g1, p['ff_w_in'][n].astype(BF16), p['ff_w_out'][n].astype(BF16), cfg)
        else:
            x = moe_layer(x, mtab, i, g1, p['moe_router'][n], p['moe_router_b'][n],
                          p['moe_w_in'], p['moe_w_out'], n, cfg)
    y_prompt = x[:cfg.tp].reshape(bp, lp, d)
    y_sample = x[cfg.tp:].reshape(bs, ls, d)
    return (y_prompt, y_sample,
            jnp.stack(new_k, axis=1), jnp.stack(new_v, axis=1),
            jnp.stack(new_rg, axis=1), jnp.stack(new_hg, axis=1))


def kernel(x_prompt, x_sample, cache_k, cache_v, state_rglru, state_hgrn, c, c_ctx, mod_w, mod_b, norm_g, hgrn_lb, rg_w_in, rg_conv_w, rg_conv_b, rg_w_a, rg_b_a, rg_w_x, rg_b_x, rg_lambda, rg_w_out, hg_w_in, hg_norm_g, hg_w_out, hy_w_in, hy_b_in, hy_conv_w, hy_conv_b, hy_f_w1, hy_f_b1, hy_f_w2, hy_f_b2, hy_f_w3, hy_freq, hy_decay, hy_bias, hy_w_out, hy_b_out, da_w_in, da_q_norm, da_k_norm, da_lambda, da_sub_norm, da_w_out, ff_w_in, ff_w_out, moe_router, moe_router_b, moe_w_in, moe_w_out):
    p = dict(mod_w=mod_w, mod_b=mod_b, norm_g=norm_g, hgrn_lb=hgrn_lb,
             rg_w_in=rg_w_in, rg_conv_w=rg_conv_w, rg_conv_b=rg_conv_b, rg_w_a=rg_w_a, rg_b_a=rg_b_a,
             rg_w_x=rg_w_x, rg_b_x=rg_b_x, rg_lambda=rg_lambda, rg_w_out=rg_w_out,
             hg_w_in=hg_w_in, hg_norm_g=hg_norm_g, hg_w_out=hg_w_out,
             hy_w_in=hy_w_in, hy_b_in=hy_b_in, hy_conv_w=hy_conv_w, hy_conv_b=hy_conv_b,
             hy_f_w1=hy_f_w1, hy_f_b1=hy_f_b1, hy_f_w2=hy_f_w2, hy_f_b2=hy_f_b2, hy_f_w3=hy_f_w3,
             hy_freq=hy_freq, hy_decay=hy_decay, hy_bias=hy_bias, hy_w_out=hy_w_out, hy_b_out=hy_b_out,
             da_w_in=da_w_in, da_q_norm=da_q_norm, da_k_norm=da_k_norm, da_lambda=da_lambda,
             da_sub_norm=da_sub_norm, da_w_out=da_w_out,
             ff_w_in=ff_w_in, ff_w_out=ff_w_out, moe_router=moe_router, moe_router_b=moe_router_b,
             moe_w_in=moe_w_in, moe_w_out=moe_w_out)
    return backbone(x_prompt, x_sample, cache_k, cache_v, state_rglru, state_hgrn, c, c_ctx, p)
```

```python
import functools
import math
from typing import NamedTuple

import numpy as np
import jax
import jax.numpy as jnp
from jax import lax
from jax.experimental import pallas as pl
from jax.experimental.pallas import tpu as pltpu

F32 = jnp.float32
BF16 = jnp.bfloat16

D_MODEL = 1024
DEPTH = 4
EPS = 1e-6
GRID_W = 64
RG_HEADS = 4
RG_BW = D_MODEL // RG_HEADS
RG_C = 8.0
HG_HEADS = 8
HG_DK = D_MODEL // HG_HEADS
HG_CHUNK = 32
HY_EMB = 33
HY_FW = 64
DA_HEADS = 8
DA_DH = 64
ROPE_BASE = 10000.0
FF_DIM = 2816
MOE_E = 8
MOE_FF = 3584

LANES = 128
ROW_TILE = 512
MOE_TILE = 1024
SEQ_TILE = 256
SCAN_CHUNK = 64
GLA_HEADS_PER_STEP = 2
DA_HEADS_PER_STEP = 2
VMEM_LIMIT = 56 * 1024 * 1024


class Cfg(NamedTuple):
    bp: int
    lp: int
    bs: int
    ls: int

    @property
    def tp(self):
        return self.bp * self.lp

    @property
    def ts(self):
        return self.bs * self.ls

    @property
    def t(self):
        return self.tp + self.ts


def _cparams(sem):
    return pltpu.CompilerParams(dimension_semantics=sem, vmem_limit_bytes=VMEM_LIMIT)


def _mod_idx(i, cfg, tm):
    npt = cfg.tp // tm
    return jnp.where(i < npt, 0, 1 + (i - npt) // (cfg.ls // tm))


def _modulate(x, g, m, k):
    ms = jnp.mean(x * x, axis=-1, keepdims=True)
    y = x * lax.rsqrt(ms + EPS) * g
    return y * (1.0 + m[k + 1:k + 2, :]) + m[k:k + 1, :]


def _silu(x):
    return x * jax.nn.sigmoid(x)


def _dot(a, b):
    return jnp.dot(a, b, preferred_element_type=F32)


def _dot_nt(a, b):
    return lax.dot_general(a, b, (((1,), (1,)), ((), ())), preferred_element_type=F32)


def _dot_tn(a, b):
    return lax.dot_general(a, b, (((0,), (0,)), ((), ())), preferred_element_type=F32)


def _dot_hi(a, b):
    return jnp.dot(a, b, preferred_element_type=F32, precision=lax.Precision.HIGHEST)


def _modtab_kernel(c_ref, w_ref, b_ref, o_ref):
    s = _silu(c_ref[...]).astype(BF16)
    o_ref[...] = _dot(s, w_ref[...].astype(BF16)) + b_ref[...]


def modulation_table(cvec, mod_w, mod_b):
    n = cvec.shape[0]
    npad = 16
    cpad = jnp.zeros((npad, D_MODEL), F32).at[:n].set(cvec)
    tn = 1536
    out = pl.pallas_call(
        _modtab_kernel,
        grid=(DEPTH, 6 * D_MODEL // tn),
        in_specs=[
            pl.BlockSpec((npad, D_MODEL), lambda l, j: (0, 0)),
            pl.BlockSpec((None, D_MODEL, tn), lambda l, j: (l, 0, j)),
            pl.BlockSpec((None, 1, tn), lambda l, j: (l, 0, j)),
        ],
        out_specs=pl.BlockSpec((None, npad, tn), lambda l, j: (l, 0, j)),
        out_shape=jax.ShapeDtypeStruct((DEPTH, npad, 6 * D_MODEL), F32),
        compiler_params=_cparams(("parallel", "parallel")),
        name="modtab",
    )(cpad, mod_w, mod_b.reshape(DEPTH, 1, 6 * D_MODEL))
    return out[:, :n].reshape(DEPTH, n, 6, D_MODEL)


def _modmm_kernel(*refs, k, has_bias, heads_out, tn):
    if has_bias:
        x_ref, m_ref, g_ref, w_ref, b_ref, o_ref = refs
    else:
        x_ref, m_ref, g_ref, w_ref, o_ref = refs
        b_ref = None
    u = _modulate(x_ref[...], g_ref[...], m_ref[...], k).astype(BF16)
    for c in range(w_ref.shape[1] // tn):
        cs = slice(c * tn, (c + 1) * tn)
        acc = _dot(u, w_ref[:, cs])
        if has_bias:
            acc = acc + b_ref[:, cs]
        if heads_out:
            for hh in range(tn // LANES):
                o_ref[c * (tn // LANES) + hh] = acc[:, hh * LANES:(hh + 1) * LANES]
        else:
            o_ref[:, cs] = acc


def mod_matmul(x, mtab, layer, k, g, w, b, cfg, heads_out=False, tn=1024):
    t, n = x.shape[0], w.shape[1]
    tm = ROW_TILE
    in_specs = [
        _tok_spec(tm),
        _mtab_spec(layer, cfg, tm),
        _const_spec((1, D_MODEL)),
        _const_spec((D_MODEL, n)),
    ]
    args = [x, mtab, g.reshape(1, D_MODEL), w]
    if b is not None:
        in_specs.append(_const_spec((1, n)))
        args.append(b.reshape(1, n))
    if heads_out:
        out_specs = pl.BlockSpec((n // LANES, tm, LANES), lambda i: (0, i, 0))
        out_shape = jax.ShapeDtypeStruct((n // LANES, t, LANES), F32)
    else:
        out_specs = pl.BlockSpec((tm, n), lambda i: (i, 0))
        out_shape = jax.ShapeDtypeStruct((t, n), F32)
    return pl.pallas_call(
        functools.partial(_modmm_kernel, k=k, has_bias=b is not None, heads_out=heads_out, tn=tn),
        grid=(t // tm,),
        in_specs=in_specs,
        out_specs=out_specs,
        out_shape=out_shape,
        compiler_params=_cparams(("parallel",)),
        name="modmm",
    )(*args)


def _seq_edges(i, cfg, lb):
    npb = cfg.tp // lb
    pp, ps = cfg.lp // lb, cfg.ls // lb
    first = jnp.where(i < npb, i % pp == 0, (i - npb) % ps == 0)
    last = jnp.where(i < npb, i % pp == pp - 1, (i - npb) % ps == ps - 1)
    return first, last


def _dwconv_kernel(x_ref, p_ref, n_ref, w_ref, b_ref, o_ref, *, cfg, taps):
    lb = x_ref.shape[0]
    first, last = _seq_edges(pl.program_id(0), cfg, lb)
    prev = jnp.where(first, 0.0, p_ref[...])
    nxt = jnp.where(last, 0.0, n_ref[...])
    ext = jnp.concatenate([prev, x_ref[...], nxt], axis=0)
    left = (taps - 1) // 2
    n_ext = lb + 16
    acc = jnp.zeros(x_ref.shape, F32) + b_ref[...]
    for kk in range(taps):
        sh = (left - kk) % n_ext
        shifted = ext if sh == 0 else pltpu.roll(ext, sh, 0)
        acc = acc + w_ref[kk:kk + 1, :] * shifted[8:8 + lb]
    o_ref[...] = acc


def dwconv(x, col_off, w, b, cfg, tc=1024):
    t = x.shape[0]
    taps, c = w.shape
    lb = SEQ_TILE
    cb = col_off // tc
    r8 = lb // 8
    nblk8 = t // 8
    return pl.pallas_call(
        functools.partial(_dwconv_kernel, cfg=cfg, taps=taps),
        grid=(t // lb, c // tc),
        in_specs=[
            pl.BlockSpec((lb, tc), lambda i, j: (i, cb + j)),
            pl.BlockSpec((8, tc), lambda i, j: (jnp.maximum(i * r8 - 1, 0), cb + j)),
            pl.BlockSpec((8, tc), lambda i, j: (jnp.minimum((i + 1) * r8, nblk8 - 1), cb + j)),
            pl.BlockSpec((taps, tc), lambda i, j: (0, j)),
            pl.BlockSpec((1, tc), lambda i, j: (0, j)),
        ],
        out_specs=pl.BlockSpec((lb, tc), lambda i, j: (i, j)),
        out_shape=jax.ShapeDtypeStruct((t, c), F32),
        compiler_params=_cparams(("parallel", "parallel")),
        name="dwconv",
    )(x, x, x, w, b.reshape(1, c))


def _rg_gates_kernel(xc_ref, wa_ref, wx_ref, ba_ref, bx_ref, lam_ref, a_ref, b_ref):
    xc = xc_ref[...]
    xb = xc.astype(BF16)
    nlam = -lam_ref[...]
    sp = jnp.maximum(nlam, 0.0) + jnp.log1p(jnp.exp(-jnp.abs(nlam)))
    for d in range(2):
        ra = jnp.concatenate(
            [_dot(xb[:, h * RG_BW:(h + 1) * RG_BW], wa_ref[d, h]) for h in range(RG_HEADS)], axis=1)
        rx = jnp.concatenate(
            [_dot(xb[:, h * RG_BW:(h + 1) * RG_BW], wx_ref[d, h]) for h in range(RG_HEADS)], axis=1)
        r = jax.nn.sigmoid(ra + ba_ref[d:d + 1, :])
        ig = jax.nn.sigmoid(rx + bx_ref[d:d + 1, :])
        log_a = (-RG_C) * sp[d:d + 1, :] * r
        a = jnp.exp(log_a)
        gain = jnp.sqrt(-jnp.tanh(log_a) * (1.0 + a * a))
        a_ref[d] = a
        b_ref[d] = gain * ig * xc


def rg_gates(xc, w_a, w_x, b_a, b_x, lam):
    t = xc.shape[0]
    tm = ROW_TILE
    wspec = pl.BlockSpec((2, RG_HEADS, RG_BW, RG_BW), lambda i: (0, 0, 0, 0))
    vspec = pl.BlockSpec((2, D_MODEL), lambda i: (0, 0))
    ospec = pl.BlockSpec((2, tm, D_MODEL), lambda i: (0, i, 0))
    oshape = jax.ShapeDtypeStruct((2, t, D_MODEL), F32)
    return pl.pallas_call(
        _rg_gates_kernel,
        grid=(t // tm,),
        in_specs=[pl.BlockSpec((tm, D_MODEL), lambda i: (i, 0)), wspec, wspec, vspec, vspec, vspec],
        out_specs=[ospec, ospec],
        out_shape=[oshape, oshape],
        compiler_params=_cparams(("parallel",)),
        name="rg_gates",
    )(xc, w_a, w_x, b_a, b_x, lam)


def _rg_scan_kernel(*refs, aliased):
    af_ref, bf_ref, ab_ref, bb_ref, h0_ref = refs[:5]
    hf_ref, hb_ref, hc_scr = refs[7:] if aliased else refs[5:]
    lc = af_ref.shape[1]

    @pl.when(pl.program_id(1) == 0)
    def _():
        hc_scr[0] = h0_ref[:, 0]
        hc_scr[1] = h0_ref[:, 1]

    def body(tt, carry):
        hf, hb = carry
        hf = af_ref[:, tt] * hf + bf_ref[:, tt]
        hf_ref[:, tt] = hf
        tb = lc - 1 - tt
        hb = ab_ref[:, tb] * hb + bb_ref[:, tb]
        hb_ref[:, tb] = hb
        return hf, hb

    hf, hb = lax.fori_loop(0, lc, body, (hc_scr[0], hc_scr[1]), unroll=4)
    hc_scr[0] = hf
    hc_scr[1] = hb


def rg_scan_group(a, bx, h0, seq0, nseq, length, bsz, prev=None):
    t = a.shape[1]
    lc = min(SCAN_CHUNK, length)
    nc = length // lc
    sub = D_MODEL // LANES
    av = a.reshape(2, t // length, length, sub, LANES)
    bv = bx.reshape(2, t // length, length, sub, LANES)
    sb0 = seq0 // bsz
    blk = (None, bsz, lc, sub, LANES)
    fwd = lambda g, c: (0, sb0 + g, c, 0, 0)
    bwd = lambda g, c: (1, sb0 + g, nc - 1 - c, 0, 0)
    oblk = (bsz, lc, sub, LANES)
    oshape = jax.ShapeDtypeStruct((t // length, length, sub, LANES), F32)
    in_specs = [
        pl.BlockSpec(blk, fwd), pl.BlockSpec(blk, fwd),
        pl.BlockSpec(blk, bwd), pl.BlockSpec(blk, bwd),
        pl.BlockSpec((bsz, 2, sub, LANES), lambda g, c: (g, 0, 0, 0)),
    ]
    args = [av, bv, av, bv, h0.reshape(nseq, 2, sub, LANES)]
    aliases = {}
    if prev is not None:
        in_specs += [pl.BlockSpec(memory_space=pl.ANY)] * 2
        args += [prev[0].reshape(oshape.shape), prev[1].reshape(oshape.shape)]
        aliases = {5: 0, 6: 1}
    hf, hb = pl.pallas_call(
        functools.partial(_rg_scan_kernel, aliased=prev is not None),
        grid=(nseq // bsz, nc),
        in_specs=in_specs,
        out_specs=[
            pl.BlockSpec(oblk, lambda g, c: (sb0 + g, c, 0, 0)),
            pl.BlockSpec(oblk, lambda g, c: (sb0 + g, nc - 1 - c, 0, 0)),
        ],
        out_shape=[oshape, oshape],
        scratch_shapes=[pltpu.VMEM((2, bsz, sub, LANES), F32)],
        input_output_aliases=aliases,
        compiler_params=_cparams(("parallel", "arbitrary")),
        name="rg_scan",
    )(*args)
    return hf.reshape(t, D_MODEL), hb.reshape(t, D_MODEL)


def _gelu_tanh(x):
    return 0.5 * x * (1.0 + jnp.tanh(math.sqrt(2.0 / math.pi) * (x + 0.044715 * (x * x * x))))


def _rg_out_kernel(gate_ref, hf_ref, hb_ref, w_ref, x_ref, m_ref, o_ref):
    y = (_gelu_tanh(gate_ref[...]) * (hf_ref[...] + hb_ref[...])).astype(BF16)
    o_ref[...] = x_ref[...] + m_ref[2:3, :] * _dot(y, w_ref[...])


def _tok_spec(tm, cb=0):
    return pl.BlockSpec((tm, D_MODEL), lambda i: (i, cb))


def _mtab_spec(layer, cfg, tm):
    return pl.BlockSpec((None, None, 6, D_MODEL), lambda i: (layer, _mod_idx(i, cfg, tm), 0, 0))


def _w_spec(k=D_MODEL):
    return pl.BlockSpec((k, D_MODEL), lambda i: (0, 0))


def rg_out(gx, hf, hb, w_out, x, mtab, layer, cfg):
    t = x.shape[0]
    tm = ROW_TILE
    return pl.pallas_call(
        _rg_out_kernel,
        grid=(t // tm,),
        in_specs=[_tok_spec(tm, 0), _tok_spec(tm), _tok_spec(tm), _w_spec(), _tok_spec(tm),
                  _mtab_spec(layer, cfg, tm)],
        out_specs=_tok_spec(tm),
        out_shape=jax.ShapeDtypeStruct((t, D_MODEL), F32),
        compiler_params=_cparams(("parallel",)),
        name="rg_out",
    )(gx, hf, hb, w_out, x, mtab)


def rglru_layer(x, mtab, layer, j, norm_g, p, state_rglru, cfg):
    gx = mod_matmul(x, mtab, layer, 0, norm_g, p['rg_w_in'][j].astype(BF16), None, cfg)
    xc = dwconv(gx, D_MODEL, p['rg_conv_w'][j], p['rg_conv_b'][j], cfg)
    a, bx = rg_gates(xc, p['rg_w_a'][j].astype(BF16), p['rg_w_x'][j].astype(BF16),
                     p['rg_b_a'][j], p['rg_b_x'][j], p['rg_lambda'][j])
    h0p = jnp.zeros((cfg.bp, 2, D_MODEL), F32)
    hp = rg_scan_group(a, bx, h0p, 0, cfg.bp, cfg.lp, math.gcd(cfg.bp, 8))
    bsz = math.gcd(math.gcd(cfg.bs, cfg.tp // cfg.ls), 8)
    hf, hb = rg_scan_group(a, bx, state_rglru[:, j].astype(F32), cfg.tp // cfg.ls, cfg.bs, cfg.ls, bsz, hp)
    new_state = jnp.stack([hf[:cfg.tp].reshape(cfg.bp, cfg.lp, D_MODEL)[:, -1],
                           hb[:cfg.tp].reshape(cfg.bp, cfg.lp, D_MODEL)[:, 0]], axis=1)
    x = rg_out(gx, hf, hb, p['rg_w_out'][j].astype(BF16), x, mtab, layer, cfg)
    return x, new_state


def _chunk_cumsum(x, rev):
    rows = x.shape[0]
    r = lax.broadcasted_iota(jnp.int32, x.shape, 0) % HG_CHUNK
    s = 1
    while s < HG_CHUNK:
        if rev:
            x = x + jnp.where(r < HG_CHUNK - s, pltpu.roll(x, rows - s, 0), 0.0)
        else:
            x = x + jnp.where(r >= s, pltpu.roll(x, s, 0), 0.0)
        s *= 2
    return x


def _gla_kernel(fblk, bblk, first, last, seqo, s0i, has0,
                qf_ref, ff_ref, vf_ref, qb_ref, fb_ref, vb_ref, lb_ref, s0_ref,
                of_ref, ob_ref, sfin_ref, s_scr, *, layer):
    i = pl.program_id(1)
    hpb, rows = qf_ref.shape[0], qf_ref.shape[1]
    nch = rows // HG_CHUNK
    units = [(hh, d) for hh in range(hpb) for d in range(2)]

    @pl.when(first[i] == 1)
    def _():
        for hh, d in units:
            s_scr[hh, d] = jnp.where(has0[i] == 1, s0_ref[d, hh].T, 0.0)

    ri = lax.broadcasted_iota(jnp.int32, (rows, rows), 0)
    ci = lax.broadcasted_iota(jnp.int32, (rows, rows), 1)
    same = (ri // HG_CHUNK) == (ci // HG_CHUNK)
    chunks = [slice(n * HG_CHUNK, (n + 1) * HG_CHUNK) for n in range(nch)]
    in_refs = ((qf_ref, ff_ref, vf_ref), (qb_ref, fb_ref, vb_ref))
    out_refs = (of_ref, ob_ref)

    pre = {}
    for hh in range(hpb):
        lbx = lb_ref[hh]
        e = jnp.exp(lbx - jnp.max(lbx, axis=0, keepdims=True))
        sm = e / jnp.sum(e, axis=0, keepdims=True)
        lb = jnp.zeros(lbx.shape[1:], F32)
        for l in range(1, layer + 1):
            lb = lb + sm[l]
        for d in range(2):
            q_ref, f_ref, v_ref = in_refs[d]
            rev = d == 1
            v16 = v_ref[hh].astype(BF16)
            lbd = lb[d:d + 1, :]
            f = lbd + (1.0 - lbd) * jax.nn.sigmoid(f_ref[hh])
            k = 1.0 - f
            b = _chunk_cumsum(jnp.log(f), rev)
            b3 = b.reshape(nch, HG_CHUNK, HG_DK)
            bl = b3[:, 0:1, :] if rev else b3[:, HG_CHUNK - 1:HG_CHUNK, :]
            qi16 = (q_ref[hh] * jnp.exp(b)).astype(BF16)
            ki16 = (k * jnp.exp(-b)).astype(BF16)
            ks16 = (k.reshape(nch, HG_CHUNK, HG_DK) * jnp.exp(bl - b3)).reshape(rows, HG_DK).astype(BF16)
            g = jnp.exp(bl)
            mask = same & ((ci >= ri) if rev else (ci <= ri))
            att = jnp.where(mask, _dot_nt(qi16, ki16), 0.0).astype(BF16)
            ds = [_dot_tn(v16[sl], ks16[sl]) for sl in chunks]
            pre[hh, d] = (qi16, v16, att, ds, g)

    prev = {}
    for hh, d in units:
        _, _, _, ds, g = pre[hh, d]
        st = s_scr[hh, d]
        sp = [None] * nch
        for n in (range(nch - 1, -1, -1) if d == 1 else range(nch)):
            sp[n] = st.astype(BF16)
            st = st * g[n] + ds[n]
        s_scr[hh, d] = st
        prev[hh, d] = sp

    for hh, d in units:
        qi16, v16, att, _, _ = pre[hh, d]
        inter = [_dot_nt(qi16[sl], prev[hh, d][n]) for n, sl in enumerate(chunks)]
        out_refs[d][hh] = _dot(att, v16) + jnp.concatenate(inter, axis=0)

    @pl.when(last[i] == 1)
    def _():
        for hh, d in units:
            sfin_ref[d, hh] = s_scr[hh, d].T


def _gla_tables(cfg):
    rb = SEQ_TILE
    pp, ps = cfg.lp // rb, cfg.ls // rb
    fblk, bblk, first, last, seqo, s0i, has0 = [], [], [], [], [], [], []
    for s in range(cfg.bp):
        for c in range(pp):
            fblk.append(s * pp + c); bblk.append(s * pp + pp - 1 - c)
            first.append(int(c == 0)); last.append(int(c == pp - 1))
            seqo.append(s); s0i.append(0); has0.append(0)
    base = cfg.tp // rb
    for s in range(cfg.bs):
        for c in range(ps):
            fblk.append(base + s * ps + c); bblk.append(base + s * ps + ps - 1 - c)
            first.append(int(c == 0)); last.append(int(c == ps - 1))
            seqo.append(cfg.bp); s0i.append(s); has0.append(1)
    return [jnp.asarray(np.asarray(v, np.int32)) for v in (fblk, bblk, first, last, seqo, s0i, has0)]


def gla(proj, hgrn_lb, s0, layer, cfg):
    t = proj.shape[1]
    rb = SEQ_TILE
    tabs = _gla_tables(cfg)
    nslots = tabs[0].shape[0]
    h8 = HG_HEADS
    hpb = GLA_HEADS_PER_STEP
    ng = h8 // hpb

    def pspec(sec, which):
        return pl.BlockSpec((hpb, rb, HG_DK),
                            lambda h, i, fb, bb, *_: (sec * ng + h, (fb if which == 0 else bb)[i], 0))

    ospec_f = pl.BlockSpec((hpb, rb, HG_DK), lambda h, i, fb, bb, *_: (h, fb[i], 0))
    ospec_b = pl.BlockSpec((hpb, rb, HG_DK), lambda h, i, fb, bb, *_: (h, bb[i], 0))
    grid_spec = pltpu.PrefetchScalarGridSpec(
        num_scalar_prefetch=7,
        grid=(ng, nslots),
        in_specs=[
            pspec(0, 0), pspec(1, 0), pspec(3, 0),
            pspec(0, 1), pspec(2, 1), pspec(3, 1),
            pl.BlockSpec((hpb, DEPTH, 2, HG_DK), lambda h, i, *_: (h, 0, 0, 0)),
            pl.BlockSpec((None, 2, hpb, HG_DK, HG_DK),
                         lambda h, i, fb, bb, fi, la, so, s0i, *_: (s0i[i], 0, h, 0, 0)),
        ],
        out_specs=[
            ospec_f, ospec_b,
            pl.BlockSpec((None, 2, hpb, HG_DK, HG_DK),
                         lambda h, i, fb, bb, fi, la, so, *_: (so[i], 0, h, 0, 0)),
        ],
        scratch_shapes=[pltpu.VMEM((hpb, 2, HG_DK, HG_DK), F32)],
    )
    oshape = jax.ShapeDtypeStruct((h8, t, HG_DK), F32)
    of, ob, sfin = pl.pallas_call(
        functools.partial(_gla_kernel, layer=layer),
        grid_spec=grid_spec,
        out_shape=[oshape, oshape,
                   jax.ShapeDtypeStruct((cfg.bp + 1, 2, h8, HG_DK, HG_DK), F32)],
        compiler_params=_cparams(("parallel", "arbitrary")),
        name="gla",
    )(*tabs, proj, proj, proj, proj, proj, proj,
      hgrn_lb.reshape(DEPTH, 2, h8, HG_DK).transpose(2, 0, 1, 3), s0)
    return of, ob, sfin[:cfg.bp]


def _head_rms(o, g):
    ms = jnp.mean(o * o, axis=-1, keepdims=True)
    return o * lax.rsqrt(ms + EPS) * g


def _hg_out_kernel(of_ref, ob_ref, gh_ref, ng_ref, w_ref, x_ref, m_ref, o_ref):
    parts = []
    for h in range(HG_HEADS):
        o = _head_rms(of_ref[h] + ob_ref[h], ng_ref[...]) * _silu(gh_ref[h])
        parts.append(o.astype(BF16))
    y = jnp.concatenate(parts, axis=1)
    o_ref[...] = x_ref[...] + m_ref[2:3, :] * _dot(y, w_ref[...])


def hg_out(of, ob, proj, norm_g, w_out, x, mtab, layer, cfg):
    t = x.shape[0]
    tm = ROW_TILE
    hspec = pl.BlockSpec((HG_HEADS, tm, HG_DK), lambda i: (0, i, 0))
    return pl.pallas_call(
        _hg_out_kernel,
        grid=(t // tm,),
        in_specs=[hspec, hspec,
                  pl.BlockSpec((HG_HEADS, tm, HG_DK), lambda i: (4, i, 0)),
                  pl.BlockSpec((1, HG_DK), lambda i: (0, 0)),
                  _w_spec(), _tok_spec(tm), _mtab_spec(layer, cfg, tm)],
        out_specs=_tok_spec(tm),
        out_shape=jax.ShapeDtypeStruct((t, D_MODEL), F32),
        compiler_params=_cparams(("parallel",)),
        name="hg_out",
    )(of, ob, proj, norm_g.reshape(1, HG_DK), w_out, x, mtab)


def hgrn2_layer(x, mtab, layer, j, norm_g, p, state_hgrn, cfg):
    proj = mod_matmul(x, mtab, layer, 0, norm_g, p['hg_w_in'][j].astype(BF16), None, cfg, heads_out=True)
    of, ob, sfin = gla(proj, p['hgrn_lb'], state_hgrn[:, j].astype(F32), layer, cfg)
    x = hg_out(of, ob, proj, p['hg_norm_g'][j], p['hg_w_out'][j].astype(BF16), x, mtab, layer, cfg)
    return x, sfin


def _dft_tables(length):
    a, b = _dft_tables_np(length)
    return jnp.asarray(a), jnp.asarray(b), jnp.asarray(np.ascontiguousarray(b.T))


@functools.lru_cache(maxsize=None)
def _dft_tables_np(length):
    n = 2 * length
    idx = np.arange(length, dtype=np.int64)
    ang = ((idx[:, None] * idx[None, :]) % n).astype(np.float64) * (2.0 * math.pi / n)
    a = np.cos(ang)
    b = -np.sin(ang)
    b[0, :] = np.where(idx % 2 == 0, 1.0, -1.0)
    return a.astype(BF16), b.astype(BF16)


@functools.lru_cache(maxsize=None)
def _hy_features_np(length):
    t_idx = np.arange(length, dtype=np.float64)
    bands = (HY_EMB - 1) // 2
    fr = np.linspace(1e-4, bands - 1, bands)
    ang = (2.0 * math.pi * t_idx / length)[:, None] * fr[None, :]
    z = np.zeros((length, LANES), np.float32)
    z[:, 0] = t_idx / (length - 1)
    z[:, 1:1 + bands] = np.cos(ang)
    z[:, 1 + bands:HY_EMB] = -np.sin(ang)
    return z


def _hy_features(length):
    return jnp.asarray(_hy_features_np(length))


def _hy_filter_kernel(z_ref, w1_ref, b1_ref, w2_ref, b2_ref, w3f_ref, w3b_ref, fq_ref, dcf_ref, dcb_ref,
                      a_ref, b_ref, ka_ref, ki_ref, kn_ref):
    length = z_ref.shape[0]
    fq = fq_ref[...]
    h = jnp.sin(fq * (_dot_hi(z_ref[...], w1_ref[...]) + b1_ref[...]))
    h = jnp.sin(fq * (_dot_hi(h, w2_ref[...]) + b2_ref[...]))
    row = lax.broadcasted_iota(jnp.int32, (length, 1), 0)
    tt = row.astype(F32) / float(length - 1)
    hf = _dot_hi(h, w3f_ref[...]) * jnp.exp(-tt * jnp.abs(dcf_ref[...]))
    hb = _dot_hi(h, w3b_ref[...]) * jnp.exp(-tt * jnp.abs(dcb_ref[...]))
    hb = jnp.where(row == 0, 0.0, hb)
    nrm = lax.rsqrt(jnp.sum(hf * hf + hb * hb, axis=0, keepdims=True) + EPS)
    hf = hf * nrm
    hb = hb * nrm
    a = a_ref[...]
    bm = b_ref[...]
    ka_ref[...] = _dot(a, (hf + hb).astype(BF16))
    kbf = _dot(bm, hf.astype(BF16))
    kbb = _dot(bm, hb.astype(BF16))
    ki_ref[...] = jnp.where(row == 0, 0.0, kbf - kbb)
    kn_ref[...] = jnp.broadcast_to(kbf[0:1, :] + kbb[0:1, :], kn_ref.shape)


def _const_spec(shape):
    nd = len(shape)
    return pl.BlockSpec(shape, lambda *_: (0,) * nd, pipeline_mode=pl.Buffered(1))


def hy_filter_spectrum(length, a_tab, b_tab, p, j, tc=256):
    d = D_MODEL
    nct = d // tc
    w1 = jnp.zeros((LANES, HY_FW), F32).at[:HY_EMB].set(p['hy_f_w1'][j])
    w3 = p['hy_f_w3'][j]
    dec = p['hy_decay'][j].reshape(1, 4 * d)
    z = _hy_features(length)
    fwd_col = lambda o, c: (0, (o * 2) * nct + c)
    bwd_col = lambda o, c: (0, (o * 2 + 1) * nct + c)
    return pl.pallas_call(
        _hy_filter_kernel,
        grid=(2, nct),
        in_specs=[
            _const_spec((length, LANES)), _const_spec((LANES, HY_FW)), _const_spec((1, HY_FW)),
            _const_spec((HY_FW, HY_FW)), _const_spec((1, HY_FW)),
            pl.BlockSpec((HY_FW, tc), fwd_col), pl.BlockSpec((HY_FW, tc), bwd_col),
            _const_spec((1, HY_FW)),
            pl.BlockSpec((1, tc), fwd_col), pl.BlockSpec((1, tc), bwd_col),
            _const_spec((length, length)), _const_spec((length, length)),
        ],
        out_specs=[
            pl.BlockSpec((None, length, tc), lambda o, c: (o, 0, c)),
            pl.BlockSpec((None, length, tc), lambda o, c: (o, 0, c)),
            pl.BlockSpec((None, 8, tc), lambda o, c: (o, 0, c)),
        ],
        out_shape=[jax.ShapeDtypeStruct((2, length, d), F32),
                   jax.ShapeDtypeStruct((2, length, d), F32),
                   jax.ShapeDtypeStruct((2, 8, d), F32)],
        compiler_params=_cparams(("parallel", "parallel")),
        name="hy_filter",
    )(z, w1, p['hy_f_b1'][j].reshape(1, HY_FW), p['hy_f_w2'][j], p['hy_f_b2'][j].reshape(1, HY_FW),
      w3, w3, p['hy_freq'][j].reshape(1, HY_FW), dec, dec, a_tab, b_tab)


def _seq_conv(x, w, b):
    length = x.shape[0]
    taps = w.shape[0]
    left = (taps - 1) // 2
    row = lax.broadcasted_iota(jnp.int32, (length, 1), 0)
    acc = b + w[left:left + 1, :] * x
    for kk in range(taps):
        off = kk - left
        if off != 0:
            shifted = pltpu.roll(x, (-off) % length, 0)
            valid = (row + off >= 0) & (row + off < length)
            acc = acc + w[kk:kk + 1, :] * jnp.where(valid, shifted, 0.0)
    return acc


def _sconv_kernel(*refs, fb, conv_z, aliased):
    (zin_ref, xin_ref, a_ref, b_ref, bt_ref, ka_ref, ki_ref, kn_ref, bias_ref,
     cwz_ref, cbz_ref, cwx_ref, cbx_ref) = refs[:13]
    o_ref, yr_scr, yi_scr, z_ref, x_ref = refs[14:] if aliased else refs[13:]
    length = zin_ref.shape[0]
    z_ref[...] = _seq_conv(zin_ref[...], cwz_ref[...], cbz_ref[...]) if conv_z else zin_ref[...]
    x_ref[...] = _seq_conv(xin_ref[...], cwx_ref[...], cbx_ref[...])
    zb = z_ref[...].astype(BF16)
    inv = 1.0 / (2 * length)
    for kf in range(length // fb):
        rows = slice(kf * fb, (kf + 1) * fb)
        pr = _dot(a_ref[rows, :], zb)
        qi = _dot(b_ref[rows, :], zb)
        ka = ka_ref[rows, :]
        ki = ki_ref[rows, :]
        if kf == 0:
            r0 = lax.broadcasted_iota(jnp.int32, (fb, 1), 0) == 0
            kd = jnp.where(r0, kn_ref[0:1, :], ka)
            wgt = jnp.where(r0, inv, 2.0 * inv)
        else:
            kd = ka
            wgt = 2.0 * inv
        yr_scr[rows, :] = ((pr * ka - qi * ki) * wgt).astype(BF16)
        yi_scr[rows, :] = ((pr * ki + qi * kd) * wgt).astype(BF16)
    for tb in range(length // fb):
        rows = slice(tb * fb, (tb + 1) * fb)
        y = _dot(a_ref[rows, :], yr_scr[...]) + _dot(bt_ref[rows, :], yi_scr[...])
        o_ref[rows, :] = x_ref[rows, :] * (y + bias_ref[...] * z_ref[rows, :])


def sconv_group(zarr, zcol, conv_z, xarr, xcol, conv_w, conv_b, seq0, nseq, length, tabs, ka, ki, kn,
                order, bias, out_prev, tc):
    a_tab, b_tab, bt_tab = tabs
    d = D_MODEL
    t = zarr.shape[0]
    fb = min(256, length)
    zc, xc = zcol // tc, xcol // tc
    taps = conv_w.shape[0]
    cb = conv_b.reshape(1, -1)
    in_specs = [
        pl.BlockSpec((length, tc), lambda s, c: (seq0 + s, zc + c)),
        pl.BlockSpec((length, tc), lambda s, c: (seq0 + s, xc + c)),
        _const_spec((length, length)), _const_spec((length, length)), _const_spec((length, length)),
        pl.BlockSpec((None, length, tc), lambda s, c: (order, 0, c)),
        pl.BlockSpec((None, length, tc), lambda s, c: (order, 0, c)),
        pl.BlockSpec((None, 8, tc), lambda s, c: (order, 0, c)),
        pl.BlockSpec((None, 1, tc), lambda s, c: (order, 0, c)),
        pl.BlockSpec((taps, tc), lambda s, c: (0, zc + c if conv_z else c)),
        pl.BlockSpec((1, tc), lambda s, c: (0, zc + c if conv_z else c)),
        pl.BlockSpec((taps, tc), lambda s, c: (0, xc + c)),
        pl.BlockSpec((1, tc), lambda s, c: (0, xc + c)),
    ]
    args = [zarr, xarr, a_tab, b_tab, bt_tab, ka, ki, kn, bias.reshape(2, 1, d), conv_w, cb, conv_w, cb]
    aliases = {}
    if out_prev is not None:
        in_specs.append(pl.BlockSpec(memory_space=pl.ANY))
        args.append(out_prev)
        aliases = {len(args) - 1: 0}
    return pl.pallas_call(
        functools.partial(_sconv_kernel, fb=fb, conv_z=conv_z, aliased=out_prev is not None),
        grid=(nseq, d // tc),
        in_specs=in_specs,
        out_specs=pl.BlockSpec((length, tc), lambda s, c: (seq0 + s, c)),
        out_shape=jax.ShapeDtypeStruct((t, d), F32),
        scratch_shapes=[pltpu.VMEM((length, tc), BF16), pltpu.VMEM((length, tc), BF16),
                        pltpu.VMEM((length, tc), F32), pltpu.VMEM((length, tc), F32)],
        input_output_aliases=aliases,
        compiler_params=_cparams(("parallel", "parallel")),
        name="sconv",
    )(*args)


def _lin_out_kernel(a_ref, w_ref, b_ref, x_ref, m_ref, o_ref):
    y = _dot(a_ref[...].astype(BF16), w_ref[...]) + b_ref[...]
    o_ref[...] = x_ref[...] + m_ref[2:3, :] * y


def lin_out(a, w, b, x, mtab, layer, cfg):
    t = x.shape[0]
    tm = ROW_TILE
    return pl.pallas_call(
        _lin_out_kernel,
        grid=(t // tm,),
        in_specs=[_tok_spec(tm), _w_spec(), pl.BlockSpec((1, D_MODEL), lambda i: (0, 0)),
                  _tok_spec(tm), _mtab_spec(layer, cfg, tm)],
        out_specs=_tok_spec(tm),
        out_shape=jax.ShapeDtypeStruct((t, D_MODEL), F32),
        compiler_params=_cparams(("parallel",)),
        name="lin_out",
    )(a, w, b.reshape(1, D_MODEL), x, mtab)


def hyena_layer(x, mtab, layer, j, norm_g, p, cfg):
    d = D_MODEL
    pre = mod_matmul(x, mtab, layer, 0, norm_g, p['hy_w_in'][j].astype(BF16), p['hy_b_in'][j], cfg)
    cw, cb, bias = p['hy_conv_w'][j], p['hy_conv_b'][j], p['hy_bias'][j]
    groups = []
    for (row0, nseq, length) in ((0, cfg.bp, cfg.lp), (cfg.tp, cfg.bs, cfg.ls)):
        tabs = _dft_tables(length)
        spec = hy_filter_spectrum(length, tabs[0], tabs[1], p, j)
        tc = 256 if length > 512 else d
        groups.append((row0 // length, nseq, length, tabs, spec, tc))
    z1 = None
    for seq0, nseq, length, tabs, (ka, ki, kn), tc in groups:
        z1 = sconv_group(pre, 0, True, pre, d, cw, cb, seq0, nseq, length, tabs, ka, ki, kn, 0, bias, z1, tc)
    z2 = None
    for seq0, nseq, length, tabs, (ka, ki, kn), tc in groups:
        z2 = sconv_group(z1, 0, False, pre, 2 * d, cw, cb, seq0, nseq, length, tabs, ka, ki, kn, 1, bias,
                         z2, tc)
    return lin_out(z2, p['hy_w_out'][j].astype(BF16), p['hy_b_out'][j], x, mtab, layer, cfg)


def _group_rms_scale(x, g1, g1t):
    x2 = x * x
    hi = x2.astype(BF16)
    lo = (x2 - hi.astype(F32)).astype(BF16)
    s = _dot(hi, g1) + _dot(lo, g1)
    r = lax.rsqrt(s * (1.0 / DA_DH) + EPS)
    rhi = r.astype(BF16)
    rlo = (r - rhi.astype(F32)).astype(BF16)
    return _dot(rhi, g1t) + _dot(rlo, g1t)


def _rope(x, cos, sin_signed):
    lane = lax.broadcasted_iota(jnp.int32, x.shape, 1)
    w = x.shape[1]
    nf = DA_DH // 4
    rot = jnp.where(lane % (2 * nf) < nf, pltpu.roll(x, w - nf, 1), pltpu.roll(x, nf, 1))
    return x * cos + rot * sin_signed


def _qk_prep_kernel(q_ref, k_ref, v_ref, qg_ref, kg_ref, g1_ref, g1t_ref, cos_ref, sin_ref,
                    qh_ref, kh_ref, vh_ref, kc_ref, *, cfg):
    tm = q_ref.shape[0]
    i = pl.program_id(0)
    is_prompt = i < cfg.tp // tm
    g1 = g1_ref[...]
    g1t = g1t_ref[...]
    q = q_ref[...]
    k = k_ref[...]
    qn = q * _group_rms_scale(q, g1, g1t) * qg_ref[...]
    kn = k * _group_rms_scale(k, g1, g1t) * kg_ref[...]

    @pl.when(is_prompt)
    def _():
        kc_ref[...] = kn

    reps = q.shape[1] // LANES
    cos = jnp.tile(cos_ref[...], (1, reps))
    sin = jnp.tile(sin_ref[...], (1, reps))
    qo = (jnp.where(is_prompt, qn, _rope(qn, cos, sin)) * (DA_DH ** -0.5)).astype(BF16)
    ko = jnp.where(is_prompt, kn, _rope(kn, cos, sin)).astype(BF16)
    vo = v_ref[...].astype(BF16)
    for h in range(DA_HEADS):
        sl = slice(h * LANES, (h + 1) * LANES)
        qh_ref[h] = qo[:, sl]
        kh_ref[h] = ko[:, sl]
        vh_ref[h] = vo[:, sl]


@functools.lru_cache(maxsize=None)
def _rope_tables_np(length):
    nf = DA_DH // 4
    t = np.arange(length)
    pos = np.stack([t // GRID_W, t % GRID_W], axis=-1).astype(np.float64)
    inv = ROPE_BASE ** (-np.arange(nf, dtype=np.float64) / nf)
    ang = pos[:, :, None] * inv
    cos = np.broadcast_to(np.cos(ang)[:, :, None, :], (length, 2, 2, nf)).reshape(length, DA_DH)
    sin = np.sin(ang)[:, :, None, :]
    sin = np.concatenate([-sin, sin], axis=2).reshape(length, DA_DH)
    reps = LANES // DA_DH
    return (np.tile(cos, (1, reps)).astype(np.float32), np.tile(sin, (1, reps)).astype(np.float32))


def _rope_tables(length):
    cos, sin = _rope_tables_np(length)
    return jnp.asarray(cos), jnp.asarray(sin)


def qk_prep(qkv, q_g, k_g, cfg):
    t = qkv.shape[0]
    tm = ROW_TILE
    d = D_MODEL
    ngrp = d // DA_DH
    grp = jnp.arange(d) // DA_DH
    g1 = (grp[:, None] == jnp.arange(LANES)[None, :]).astype(BF16)
    g1t = g1.T
    cos, sin = _rope_tables(cfg.ls)
    npt = cfg.tp // tm
    pps = cfg.ls // tm
    tab_spec = pl.BlockSpec((tm, LANES), lambda i: (jnp.maximum(i - npt, 0) % pps, 0))
    hspec = pl.BlockSpec((DA_HEADS, tm, LANES), lambda i: (0, i, 0))
    hshape = jax.ShapeDtypeStruct((DA_HEADS, t, LANES), BF16)
    return pl.pallas_call(
        functools.partial(_qk_prep_kernel, cfg=cfg),
        grid=(t // tm,),
        in_specs=[_tok_spec(tm, 0), _tok_spec(tm, 1), _tok_spec(tm, 2),
                  pl.BlockSpec((1, d), lambda i: (0, 0)), pl.BlockSpec((1, d), lambda i: (0, 0)),
                  pl.BlockSpec((d, LANES), lambda i: (0, 0)), pl.BlockSpec((LANES, d), lambda i: (0, 0)),
                  tab_spec, tab_spec],
        out_specs=[hspec, hspec, hspec,
                   pl.BlockSpec((tm, d), lambda i: (jnp.minimum(i, npt - 1), 0))],
        out_shape=[hshape, hshape, hshape, jax.ShapeDtypeStruct((cfg.tp, d), F32)],
        compiler_params=_cparams(("arbitrary",)),
        name="qk_prep",
    )(qkv, qkv, qkv, jnp.tile(q_g, ngrp).reshape(1, d), jnp.tile(k_g, ngrp).reshape(1, d),
      g1, g1t, cos, sin)


def _dattn_kernel(*refs, has_cache, aliased, lam_init):
    n_in = 4 + (2 if has_cache else 0)
    lp_ref, q_ref, k_ref, v_ref = refs[:4]
    o_ref = refs[n_in + (1 if aliased else 0)]
    lp = lp_ref[...]
    lam = (jnp.exp(jnp.sum(lp[0:1] * lp[1:2], axis=1, keepdims=True))
           - jnp.exp(jnp.sum(lp[2:3] * lp[3:4], axis=1, keepdims=True)) + lam_init)
    lane = lax.broadcasted_iota(jnp.int32, q_ref.shape[1:], 1)
    for hh in range(q_ref.shape[0]):
        q = q_ref[hh]
        zero = jnp.zeros_like(q)
        qs = (jnp.where(lane < DA_DH, q, zero), jnp.where(lane >= DA_DH, q, zero))
        k = k_ref[hh]
        v = v_ref[hh]
        if has_cache:
            ck = refs[4][:, hh * LANES:(hh + 1) * LANES].astype(BF16)
            cv = refs[5][:, hh * LANES:(hh + 1) * LANES].astype(BF16)
        halves = []
        for half in range(2):
            s = _dot_nt(qs[half], k)
            m = jnp.max(s, axis=-1, keepdims=True)
            if has_cache:
                c = _dot_nt(qs[half], ck)
                m = jnp.maximum(m, jnp.max(c, axis=-1, keepdims=True))
            e = jnp.exp(s - m)
            z = jnp.sum(e, axis=-1, keepdims=True)
            pv = _dot(e.astype(BF16), v)
            if has_cache:
                ec = jnp.exp(c - m)
                z = z + jnp.sum(ec, axis=-1, keepdims=True)
                pv = pv + _dot(ec.astype(BF16), cv)
            halves.append(pv / z)
        o_ref[hh] = halves[0] - lam * halves[1]


def dattn_group(qh, kh, vh, da_lambda, lam_init, seq0, nseq, length, cache_k=None, cache_v=None,
                prev=None, tq=256):
    t = qh.shape[1]
    tq = min(tq, length)
    nq = length // tq
    has_cache = cache_k is not None
    hpb = DA_HEADS_PER_STEP
    in_specs = [
        pl.BlockSpec((4, DA_DH), lambda b, h, qi: (0, 0)),
        pl.BlockSpec((hpb, tq, LANES), lambda b, h, qi: (h, (seq0 + b) * nq + qi, 0)),
        pl.BlockSpec((hpb, length, LANES), lambda b, h, qi: (h, seq0 + b, 0)),
        pl.BlockSpec((hpb, length, LANES), lambda b, h, qi: (h, seq0 + b, 0)),
    ]
    args = [da_lambda, qh, kh, vh]
    if has_cache:
        past = cache_k.shape[1]
        cspec = pl.BlockSpec((None, past, hpb * LANES), lambda b, h, qi: (b, 0, h))
        in_specs += [cspec, cspec]
        args += [cache_k.reshape(nseq, past, DA_HEADS * LANES), cache_v.reshape(nseq, past, DA_HEADS * LANES)]
    aliases = {}
    if prev is not None:
        in_specs.append(pl.BlockSpec(memory_space=pl.ANY))
        args.append(prev)
        aliases = {len(args) - 1: 0}
    return pl.pallas_call(
        functools.partial(_dattn_kernel, has_cache=has_cache, aliased=prev is not None, lam_init=lam_init),
        grid=(nseq, DA_HEADS // hpb, nq),
        in_specs=in_specs,
        out_specs=pl.BlockSpec((hpb, tq, LANES), lambda b, h, qi: (h, (seq0 + b) * nq + qi, 0)),
        out_shape=jax.ShapeDtypeStruct((DA_HEADS, t, LANES), F32),
        input_output_aliases=aliases,
        compiler_params=_cparams(("parallel", "parallel", "parallel")),
        name="dattn",
    )(*args)


def _da_out_kernel(o_ref, sg_ref, w_ref, x_ref, m_ref, out_ref, *, lam_init):
    parts = []
    for h in range(DA_HEADS):
        parts.append((_head_rms(o_ref[h], sg_ref[...]) * (1.0 - lam_init)).astype(BF16))
    y = jnp.concatenate(parts, axis=1)
    out_ref[...] = x_ref[...] + m_ref[2:3, :] * _dot(y, w_ref[...])


def da_out(o, sub_g, w_out, x, mtab, layer, lam_init, cfg):
    t = x.shape[0]
    tm = ROW_TILE
    return pl.pallas_call(
        functools.partial(_da_out_kernel, lam_init=lam_init),
        grid=(t // tm,),
        in_specs=[pl.BlockSpec((DA_HEADS, tm, LANES), lambda i: (0, i, 0)),
                  pl.BlockSpec((1, LANES), lambda i: (0, 0)),
                  _w_spec(), _tok_spec(tm), _mtab_spec(layer, cfg, tm)],
        out_specs=_tok_spec(tm),
        out_shape=jax.ShapeDtypeStruct((t, D_MODEL), F32),
        compiler_params=_cparams(("parallel",)),
        name="da_out",
    )(o, sub_g.reshape(1, LANES), w_out, x, mtab)


def diffattn_layer(x, mtab, layer, j, norm_g, p, cache_k, cache_v, cfg):
    d = D_MODEL
    lam_init = 0.8 - 0.6 * math.exp(-0.3 * layer)
    qkv = mod_matmul(x, mtab, layer, 0, norm_g, p['da_w_in'][j].astype(BF16), None, cfg)
    qh, kh, vh, kc = qk_prep(qkv, p['da_q_norm'][j], p['da_k_norm'][j], cfg)
    lamp = p['da_lambda'][j].astype(F32)
    o = dattn_group(qh, kh, vh, lamp, lam_init, 0, cfg.bp, cfg.lp)
    o = dattn_group(qh, kh, vh, lamp, lam_init, cfg.tp // cfg.ls, cfg.bs, cfg.ls,
                    cache_k[:, j], cache_v[:, j], prev=o)
    x = da_out(o, p['da_sub_norm'][j], p['da_w_out'][j].astype(BF16), x, mtab, layer, lam_init, cfg)
    new_k = kc.reshape(cfg.bp, cfg.lp, DA_HEADS, 2 * DA_DH)
    new_v = qkv[:cfg.tp, 2 * d:].reshape(cfg.bp, cfg.lp, DA_HEADS, 2 * DA_DH)
    return x, new_k, new_v


def _ffn_kernel(x_ref, m_ref, g_ref, wg_ref, wh_ref, wo_ref, o_ref, u_scr, acc_scr):
    f = pl.program_id(1)

    @pl.when(f == 0)
    def _():
        u_scr[...] = _modulate(x_ref[...], g_ref[...], m_ref[...], 3).astype(BF16)
        acc_scr[...] = jnp.zeros_like(acc_scr)

    u = u_scr[...]
    a = (_silu(_dot(u, wg_ref[...])) * _dot(u, wh_ref[...])).astype(BF16)
    acc_scr[...] += _dot(a, wo_ref[...])

    @pl.when(f == pl.num_programs(1) - 1)
    def _():
        o_ref[...] = x_ref[...] + m_ref[5:6, :] * acc_scr[...]


def dense_ffn(x, mtab, layer, g, w_in, w_out, cfg, tf=1408):
    t = x.shape[0]
    tm = ROW_TILE
    nf = FF_DIM // tf
    return pl.pallas_call(
        _ffn_kernel,
        grid=(t // tm, nf),
        in_specs=[
            pl.BlockSpec((tm, D_MODEL), lambda i, f: (i, 0)),
            pl.BlockSpec((None, None, 6, D_MODEL), lambda i, f: (layer, _mod_idx(i, cfg, tm), 0, 0)),
            pl.BlockSpec((1, D_MODEL), lambda i, f: (0, 0)),
            pl.BlockSpec((D_MODEL, tf), lambda i, f: (0, f)),
            pl.BlockSpec((D_MODEL, tf), lambda i, f: (0, nf + f)),
            pl.BlockSpec((tf, D_MODEL), lambda i, f: (f, 0)),
        ],
        out_specs=pl.BlockSpec((tm, D_MODEL), lambda i, f: (i, 0)),
        out_shape=jax.ShapeDtypeStruct((t, D_MODEL), F32),
        scratch_shapes=[pltpu.VMEM((tm, D_MODEL), BF16), pltpu.VMEM((tm, D_MODEL), F32)],
        compiler_params=_cparams(("parallel", "arbitrary")),
        name="ffn",
    )(x, mtab, g.reshape(1, D_MODEL), w_in, w_in, w_out)


def _moe_pre_kernel(x_ref, m_ref, g_ref, rw_ref, rb_ref, u_ref, r_ref, c_ref, cnt_scr):
    tm = x_ref.shape[0]

    @pl.when(pl.program_id(0) == 0)
    def _():
        cnt_scr[...] = jnp.zeros_like(cnt_scr)

    u = _modulate(x_ref[...], g_ref[...], m_ref[...], 3)
    bits = pltpu.bitcast(u.astype(BF16).astype(F32), jnp.uint32)
    half = D_MODEL // 2
    u_ref[...] = (bits[:, :half] >> 16) | (bits[:, half:] & jnp.uint32(0xFFFF0000))
    logits = _dot_hi(u, rw_ref[...]) + rb_ref[...]
    lane = lax.broadcasted_iota(jnp.int32, logits.shape, 1)
    neg = -jnp.inf
    lg = jnp.where(lane < MOE_E, logits, neg)
    m1 = jnp.max(lg, axis=-1, keepdims=True)
    i1 = jnp.min(jnp.where(lg == m1, lane, LANES), axis=-1, keepdims=True)
    lg2 = jnp.where(lane == i1, neg, lg)
    m2 = jnp.max(lg2, axis=-1, keepdims=True)
    i2 = jnp.min(jnp.where(lg2 == m2, lane, LANES), axis=-1, keepdims=True)
    e2 = jnp.exp(m2 - m1)
    w1 = 1.0 / (1.0 + e2)
    w2 = e2 / (1.0 + e2)
    hit = jnp.where((lane == i1) | (lane == i2), 1.0, 0.0)
    ri = lax.broadcasted_iota(jnp.int32, (tm, tm), 0)
    ci = lax.broadcasted_iota(jnp.int32, (tm, tm), 1)
    ahead = jnp.where(ci < ri, 1.0, 0.0).astype(BF16)
    pos = _dot(ahead, hit.astype(BF16)) + cnt_scr[...]
    r1 = jnp.sum(jnp.where(lane == i1, pos, 0.0), axis=-1, keepdims=True)
    r2 = jnp.sum(jnp.where(lane == i2, pos, 0.0), axis=-1, keepdims=True)
    cnt = cnt_scr[...] + jnp.sum(hit, axis=0, keepdims=True)
    cnt_scr[...] = cnt
    c_ref[...] = jnp.broadcast_to(cnt, c_ref.shape)
    vals = (i1.astype(F32), i2.astype(F32), w1, w2, r1, r2)
    out = jnp.zeros(logits.shape, F32)
    for col, val in enumerate(vals):
        out = jnp.where(lane == col, val, out)
    r_ref[...] = out


def moe_pre(x, mtab, layer, g, router_w, router_b, cfg):
    t = x.shape[0]
    tm = ROW_TILE
    rw = jnp.zeros((D_MODEL, LANES), F32).at[:, :MOE_E].set(router_w)
    rb = jnp.zeros((1, LANES), F32).at[0, :MOE_E].set(router_b)
    return pl.pallas_call(
        _moe_pre_kernel,
        grid=(t // tm,),
        in_specs=[_tok_spec(tm), _mtab_spec(layer, cfg, tm), pl.BlockSpec((1, D_MODEL), lambda i: (0, 0)),
                  pl.BlockSpec((D_MODEL, LANES), lambda i: (0, 0)), pl.BlockSpec((1, LANES), lambda i: (0, 0))],
        out_specs=[pl.BlockSpec((tm, D_MODEL // 2), lambda i: (i, 0)),
                   pl.BlockSpec((tm, LANES), lambda i: (i, 0)),
                   pl.BlockSpec((8, LANES), lambda i: (0, 0))],
        out_shape=[jax.ShapeDtypeStruct((t, D_MODEL // 2), jnp.uint32),
                   jax.ShapeDtypeStruct((t, LANES), F32),
                   jax.ShapeDtypeStruct((8, LANES), F32)],
        scratch_shapes=[pltpu.VMEM((1, LANES), F32)],
        compiler_params=_cparams(("arbitrary",)),
        name="moe_pre",
    )(x, mtab, g.reshape(1, D_MODEL), rw, rb)


def _moe_ffn_kernel(te_ref, act_ref, x_ref, wg_ref, wh_ref, wo_ref, o_ref, xb_scr, acc_scr):
    i = pl.program_id(0)
    f = pl.program_id(1)
    active = act_ref[i] == 1

    @pl.when(f == 0)
    def _():
        acc_scr[...] = jnp.zeros_like(acc_scr)

    @pl.when(active & (f == 0))
    def _():
        w = x_ref[...]
        lo = pltpu.bitcast(w << 16, F32)
        hi = pltpu.bitcast(w & jnp.uint32(0xFFFF0000), F32)
        xb_scr[...] = jnp.concatenate([lo, hi], axis=1).astype(BF16)

    @pl.when(active)
    def _():
        u = xb_scr[...]
        tf = wg_ref.shape[1]
        sub = 256
        part = None
        for c in range(tf // sub):
            cs = slice(c * sub, (c + 1) * sub)
            a = (_silu(_dot(u, wg_ref[:, cs].astype(BF16)))
                 * _dot(u, wh_ref[:, cs].astype(BF16))).astype(BF16)
            y = _dot(a, wo_ref[cs, :].astype(BF16))
            part = y if part is None else part + y
        acc_scr[...] += part

    @pl.when(f == pl.num_programs(1) - 1)
    def _():
        o_ref[...] = acc_scr[...]


def moe_ffn(xs, tile_expert, tile_active, w_in, w_out, n, tf=512):
    npad = xs.shape[0]
    tm = MOE_TILE
    nf = MOE_FF // tf
    grid_spec = pltpu.PrefetchScalarGridSpec(
        num_scalar_prefetch=2,
        grid=(npad // tm, nf),
        in_specs=[
            pl.BlockSpec((tm, D_MODEL // 2), lambda i, f, te, ac: (i, 0)),
            pl.BlockSpec((None, None, D_MODEL, tf), lambda i, f, te, ac: (n, te[i], 0, f)),
            pl.BlockSpec((None, None, D_MODEL, tf), lambda i, f, te, ac: (n, te[i], 0, nf + f)),
            pl.BlockSpec((None, None, tf, D_MODEL), lambda i, f, te, ac: (n, te[i], f, 0)),
        ],
        out_specs=pl.BlockSpec((tm, D_MODEL), lambda i, f, te, ac: (i, 0)),
        scratch_shapes=[pltpu.VMEM((tm, D_MODEL), BF16), pltpu.VMEM((tm, D_MODEL), F32)],
    )
    return pl.pallas_call(
        _moe_ffn_kernel,
        grid_spec=grid_spec,
        out_shape=jax.ShapeDtypeStruct((npad, D_MODEL), F32),
        compiler_params=_cparams(("parallel", "arbitrary")),
        name="moe_ffn",
    )(tile_expert, tile_active, xs, w_in, w_in, w_out)


def _moe_combine_kernel(y1_ref, y2_ref, r_ref, x_ref, m_ref, o_ref):
    r = r_ref[...]
    y = r[:, 2:3] * y1_ref[...] + r[:, 3:4] * y2_ref[...]
    o_ref[...] = x_ref[...] + m_ref[5:6, :] * y


def moe_combine(y1, y2, route, x, mtab, layer, cfg):
    t = x.shape[0]
    tm = ROW_TILE
    return pl.pallas_call(
        _moe_combine_kernel,
        grid=(t // tm,),
        in_specs=[_tok_spec(tm), _tok_spec(tm), pl.BlockSpec((tm, LANES), lambda i: (i, 0)),
                  _tok_spec(tm), _mtab_spec(layer, cfg, tm)],
        out_specs=_tok_spec(tm),
        out_shape=jax.ShapeDtypeStruct((t, D_MODEL), F32),
        compiler_params=_cparams(("parallel",)),
        name="moe_combine",
    )(y1, y2, route, x, mtab)


def moe_layer(x, mtab, layer, g, router_w, router_b, w_in, w_out, n, cfg):
    t = x.shape[0]
    tm = MOE_TILE
    u, route, cnt = moe_pre(x, mtab, layer, g, router_w, router_b, cfg)
    counts = cnt[0, :MOE_E].astype(jnp.int32)
    padded = ((counts + tm - 1) // tm) * tm
    ends = jnp.cumsum(padded)
    starts = ends - padded
    e = route[:, 0:2].astype(jnp.int32)
    dest = starts[e] + route[:, 4:6].astype(jnp.int32)
    npad = 2 * t + MOE_E * tm
    tok = jnp.arange(t, dtype=jnp.int32)
    row_tok = jnp.zeros((npad,), jnp.int32).at[jnp.concatenate([dest[:, 0], dest[:, 1]])].set(
        jnp.concatenate([tok, tok]))
    tile_start = jnp.arange(npad // tm, dtype=jnp.int32) * tm
    tile_active = (tile_start < ends[-1]).astype(jnp.int32)
    tile_expert = jnp.minimum(jnp.sum((tile_start[:, None] >= ends[None, :]).astype(jnp.int32), axis=1),
                              MOE_E - 1)
    last_e = tile_expert[jnp.maximum(jnp.sum(tile_active) - 1, 0)]
    tile_expert = jnp.where(tile_active == 1, tile_expert, last_e)
    xs = jnp.take(u, row_tok, axis=0)
    ys = moe_ffn(xs, tile_expert, tile_active, w_in, w_out, n)
    y1 = jnp.take(ys, dest[:, 0], axis=0)
    y2 = jnp.take(ys, dest[:, 1], axis=0)
    return moe_combine(y1, y2, route, x, mtab, layer, cfg)


def backbone(x_prompt, x_sample, cache_k, cache_v, state_rglru, state_hgrn, c, c_ctx, p):
    bp, lp, d = x_prompt.shape
    bs, ls, _ = x_sample.shape
    cfg = Cfg(bp, lp, bs, ls)
    x = jnp.concatenate([x_prompt.reshape(bp * lp, d), x_sample.reshape(bs * ls, d)], axis=0)
    cvec = jnp.concatenate([c_ctx[None, :], c], axis=0)
    mtab = modulation_table(cvec, p['mod_w'], p['mod_b'])
    new_k, new_v, new_rg, new_hg = [], [], [], []
    for i in range(DEPTH):
        kind, j = i % 4, i // 4
        g0 = p['norm_g'][i, 0]
        if kind == 0:
            x, st = rglru_layer(x, mtab, i, j, g0, p, state_rglru, cfg)
            new_rg.append(st)
        elif kind == 1:
            x, st = hgrn2_layer(x, mtab, i, j, g0, p, state_hgrn, cfg)
            new_hg.append(st)
        elif kind == 2:
            x = hyena_layer(x, mtab, i, j, g0, p, cfg)
        else:
            x, nk, nv = diffattn_layer(x, mtab, i, j, g0, p, cache_k, cache_v, cfg)
            new_k.append(nk)
            new_v.append(nv)
        n = i // 2
        g1 = p['norm_g'][i, 1]
        if i % 2 == 0:
            x = dense_ffn(x, mtab, i, g1, p['ff_w_in'][n].astype(BF16), p['ff_w_out'][n].astype(BF16), cfg)
        else:
            x = moe_layer(x, mtab, i, g1, p['moe_router'][n], p['moe_router_b'][n],
                          p['moe_w_in'], p['moe_w_out'], n, cfg)
    y_prompt = x[:cfg.tp].reshape(bp, lp, d)
    y_sample = x[cfg.tp:].reshape(bs, ls, d)
    return (y_prompt, y_sample,
            jnp.stack(new_k, axis=1), jnp.stack(new_v, axis=1),
            jnp.stack(new_rg, axis=1), jnp.stack(new_hg, axis=1))


def kernel(x_prompt, x_sample, cache_k, cache_v, state_rglru, state_hgrn, c, c_ctx, mod_w, mod_b, norm_g, hgrn_lb, rg_w_in, rg_conv_w, rg_conv_b, rg_w_a, rg_b_a, rg_w_x, rg_b_x, rg_lambda, rg_w_out, hg_w_in, hg_norm_g, hg_w_out, hy_w_in, hy_b_in, hy_conv_w, hy_conv_b, hy_f_w1, hy_f_b1, hy_f_w2, hy_f_b2, hy_f_w3, hy_freq, hy_decay, hy_bias, hy_w_out, hy_b_out, da_w_in, da_q_norm, da_k_norm, da_lambda, da_sub_norm, da_w_out, ff_w_in, ff_w_out, moe_router, moe_router_b, moe_w_in, moe_w_out):
    p = dict(mod_w=mod_w, mod_b=mod_b, norm_g=norm_g, hgrn_lb=hgrn_lb,
             rg_w_in=rg_w_in, rg_conv_w=rg_conv_w, rg_conv_b=rg_conv_b, rg_w_a=rg_w_a, rg_b_a=rg_b_a,
             rg_w_x=rg_w_x, rg_b_x=rg_b_x, rg_lambda=rg_lambda, rg_w_out=rg_w_out,
             hg_w_in=hg_w_in, hg_norm_g=hg_norm_g, hg_w_out=hg_w_out,
             hy_w_in=hy_w_in, hy_b_in=hy_b_in, hy_conv_w=hy_conv_w, hy_conv_b=hy_conv_b,
             hy_f_w1=hy_f_w1, hy_f_b1=hy_f_b1, hy_f_w2=hy_f_w2, hy_f_b2=hy_f_b2, hy_f_w3=hy_f_w3,
             hy_freq=hy_freq, hy_decay=hy_decay, hy_bias=hy_bias, hy_w_out=hy_w_out, hy_b_out=hy_b_out,
             da_w_in=da_w_in, da_q_norm=da_q_norm, da_k_norm=da_k_norm, da_lambda=da_lambda,
             da_sub_norm=da_sub_norm, da_w_out=da_w_out,
             ff_w_in=ff_w_in, ff_w_out=ff_w_out, moe_router=moe_router, moe_router_b=moe_router_b,
             moe_w_in=moe_w_in, moe_w_out=moe_w_out)
    return backbone(x_prompt, x_sample, cache_k, cache_v, state_rglru, state_hgrn, c, c_ctx, p)
```

```python
import functools
import math
from typing import NamedTuple

import numpy as np
import jax
import jax.numpy as jnp
from jax import lax
from jax.experimental import pallas as pl
from jax.experimental.pallas import tpu as pltpu

F32 = jnp.float32
BF16 = jnp.bfloat16

D_MODEL = 1024
DEPTH = 4
EPS = 1e-6
GRID_W = 64
RG_HEADS = 4
RG_BW = D_MODEL // RG_HEADS
RG_C = 8.0
HG_HEADS = 8
HG_DK = D_MODEL // HG_HEADS
HG_CHUNK = 32
HY_EMB = 33
HY_FW = 64
DA_HEADS = 8
DA_DH = 64
ROPE_BASE = 10000.0
FF_DIM = 2816
MOE_E = 8
MOE_FF = 3584

LANES = 128
ROW_TILE = 512
MOE_TILE = 1024
SEQ_TILE = 256
GLA_HEADS_PER_STEP = 4
DA_HEADS_PER_STEP = 2
VMEM_LIMIT = 56 * 1024 * 1024


class Cfg(NamedTuple):
    bp: int
    lp: int
    bs: int
    ls: int

    @property
    def tp(self):
        return self.bp * self.lp

    @property
    def ts(self):
        return self.bs * self.ls

    @property
    def t(self):
        return self.tp + self.ts


def _cparams(sem):
    return pltpu.CompilerParams(dimension_semantics=sem, vmem_limit_bytes=VMEM_LIMIT)


def _mod_idx(i, cfg, tm):
    npt = cfg.tp // tm
    return jnp.where(i < npt, 0, 1 + (i - npt) // (cfg.ls // tm))


def _modulate(x, g, m, k):
    ms = jnp.mean(x * x, axis=-1, keepdims=True)
    y = x * lax.rsqrt(ms + EPS) * g
    return y * (1.0 + m[k + 1:k + 2, :]) + m[k:k + 1, :]


def _silu(x):
    return x * jax.nn.sigmoid(x)


def _dot(a, b):
    return jnp.dot(a, b, preferred_element_type=F32)


def _dot_nt(a, b):
    return lax.dot_general(a, b, (((1,), (1,)), ((), ())), preferred_element_type=F32)


def _dot_tn(a, b):
    return lax.dot_general(a, b, (((0,), (0,)), ((), ())), preferred_element_type=F32)


def _dot_hi(a, b):
    return jnp.dot(a, b, preferred_element_type=F32, precision=lax.Precision.HIGHEST)


def _modtab_kernel(c_ref, w_ref, b_ref, o_ref):
    s = _silu(c_ref[...]).astype(BF16)
    o_ref[...] = _dot(s, w_ref[...].astype(BF16)) + b_ref[...]


def modulation_table(cvec, mod_w, mod_b):
    n = cvec.shape[0]
    npad = 16
    cpad = jnp.zeros((npad, D_MODEL), F32).at[:n].set(cvec)
    tn = 1536
    out = pl.pallas_call(
        _modtab_kernel,
        grid=(DEPTH, 6 * D_MODEL // tn),
        in_specs=[
            pl.BlockSpec((npad, D_MODEL), lambda l, j: (0, 0)),
            pl.BlockSpec((None, D_MODEL, tn), lambda l, j: (l, 0, j)),
            pl.BlockSpec((None, 1, tn), lambda l, j: (l, 0, j)),
        ],
        out_specs=pl.BlockSpec((None, npad, tn), lambda l, j: (l, 0, j)),
        out_shape=jax.ShapeDtypeStruct((DEPTH, npad, 6 * D_MODEL), F32),
        compiler_params=_cparams(("parallel", "parallel")),
        name="modtab",
    )(cpad, mod_w, mod_b.reshape(DEPTH, 1, 6 * D_MODEL))
    return out[:, :n].reshape(DEPTH, n, 6, D_MODEL)


def _modmm_kernel(*refs, k, has_bias, heads_out, tn):
    if has_bias:
        x_ref, m_ref, g_ref, w_ref, b_ref, o_ref = refs
    else:
        x_ref, m_ref, g_ref, w_ref, o_ref = refs
        b_ref = None
    u = _modulate(x_ref[...], g_ref[...], m_ref[...], k).astype(BF16)
    for c in range(w_ref.shape[1] // tn):
        cs = slice(c * tn, (c + 1) * tn)
        acc = _dot(u, w_ref[:, cs])
        if has_bias:
            acc = acc + b_ref[:, cs]
        if heads_out:
            for hh in range(tn // LANES):
                o_ref[c * (tn // LANES) + hh] = acc[:, hh * LANES:(hh + 1) * LANES]
        else:
            o_ref[:, cs] = acc


def mod_matmul(x, mtab, layer, k, g, w, b, cfg, heads_out=False, tn=1024):
    t, n = x.shape[0], w.shape[1]
    tm = ROW_TILE
    in_specs = [
        _tok_spec(tm),
        _mtab_spec(layer, cfg, tm),
        _const_spec((1, D_MODEL)),
        _const_spec((D_MODEL, n)),
    ]
    args = [x, mtab, g.reshape(1, D_MODEL), w]
    if b is not None:
        in_specs.append(_const_spec((1, n)))
        args.append(b.reshape(1, n))
    if heads_out:
        out_specs = pl.BlockSpec((n // LANES, tm, LANES), lambda i: (0, i, 0))
        out_shape = jax.ShapeDtypeStruct((n // LANES, t, LANES), F32)
    else:
        out_specs = pl.BlockSpec((tm, n), lambda i: (i, 0))
        out_shape = jax.ShapeDtypeStruct((t, n), F32)
    return pl.pallas_call(
        functools.partial(_modmm_kernel, k=k, has_bias=b is not None, heads_out=heads_out, tn=tn),
        grid=(t // tm,),
        in_specs=in_specs,
        out_specs=out_specs,
        out_shape=out_shape,
        compiler_params=_cparams(("parallel",)),
        name="modmm",
    )(*args)


def _seq_edges(i, cfg, lb):
    npb = cfg.tp // lb
    pp, ps = cfg.lp // lb, cfg.ls // lb
    first = jnp.where(i < npb, i % pp == 0, (i - npb) % ps == 0)
    last = jnp.where(i < npb, i % pp == pp - 1, (i - npb) % ps == ps - 1)
    return first, last


def _dwconv_kernel(x_ref, p_ref, n_ref, w_ref, b_ref, o_ref, *, cfg, taps):
    lb = x_ref.shape[0]
    first, last = _seq_edges(pl.program_id(0), cfg, lb)
    prev = jnp.where(first, 0.0, p_ref[...])
    nxt = jnp.where(last, 0.0, n_ref[...])
    ext = jnp.concatenate([prev, x_ref[...], nxt], axis=0)
    left = (taps - 1) // 2
    n_ext = lb + 16
    acc = jnp.zeros(x_ref.shape, F32) + b_ref[...]
    for kk in range(taps):
        sh = (left - kk) % n_ext
        shifted = ext if sh == 0 else pltpu.roll(ext, sh, 0)
        acc = acc + w_ref[kk:kk + 1, :] * shifted[8:8 + lb]
    o_ref[...] = acc


def dwconv(x, col_off, w, b, cfg, tc=1024):
    t = x.shape[0]
    taps, c = w.shape
    lb = SEQ_TILE
    cb = col_off // tc
    r8 = lb // 8
    nblk8 = t // 8
    return pl.pallas_call(
        functools.partial(_dwconv_kernel, cfg=cfg, taps=taps),
        grid=(t // lb, c // tc),
        in_specs=[
            pl.BlockSpec((lb, tc), lambda i, j: (i, cb + j)),
            pl.BlockSpec((8, tc), lambda i, j: (jnp.maximum(i * r8 - 1, 0), cb + j)),
            pl.BlockSpec((8, tc), lambda i, j: (jnp.minimum((i + 1) * r8, nblk8 - 1), cb + j)),
            pl.BlockSpec((taps, tc), lambda i, j: (0, j)),
            pl.BlockSpec((1, tc), lambda i, j: (0, j)),
        ],
        out_specs=pl.BlockSpec((lb, tc), lambda i, j: (i, j)),
        out_shape=jax.ShapeDtypeStruct((t, c), F32),
        compiler_params=_cparams(("parallel", "parallel")),
        name="dwconv",
    )(x, x, x, w, b.reshape(1, c))


def _rg_gates_kernel(xc_ref, wa_ref, wx_ref, ba_ref, bx_ref, lam_ref, a_ref, b_ref):
    xc = xc_ref[...]
    xb = xc.astype(BF16)
    nlam = -lam_ref[...]
    sp = jnp.maximum(nlam, 0.0) + jnp.log1p(jnp.exp(-jnp.abs(nlam)))
    for d in range(2):
        ra = jnp.concatenate(
            [_dot(xb[:, h * RG_BW:(h + 1) * RG_BW], wa_ref[d, h]) for h in range(RG_HEADS)], axis=1)
        rx = jnp.concatenate(
            [_dot(xb[:, h * RG_BW:(h + 1) * RG_BW], wx_ref[d, h]) for h in range(RG_HEADS)], axis=1)
        r = jax.nn.sigmoid(ra + ba_ref[d:d + 1, :])
        ig = jax.nn.sigmoid(rx + bx_ref[d:d + 1, :])
        log_a = (-RG_C) * sp[d:d + 1, :] * r
        a = jnp.exp(log_a)
        gain = jnp.sqrt(-jnp.tanh(log_a) * (1.0 + a * a))
        a_ref[d] = a
        b_ref[d] = gain * ig * xc


def rg_gates(xc, w_a, w_x, b_a, b_x, lam):
    t = xc.shape[0]
    tm = ROW_TILE
    wspec = pl.BlockSpec((2, RG_HEADS, RG_BW, RG_BW), lambda i: (0, 0, 0, 0))
    vspec = pl.BlockSpec((2, D_MODEL), lambda i: (0, 0))
    ospec = pl.BlockSpec((2, tm, D_MODEL), lambda i: (0, i, 0))
    oshape = jax.ShapeDtypeStruct((2, t, D_MODEL), F32)
    return pl.pallas_call(
        _rg_gates_kernel,
        grid=(t // tm,),
        in_specs=[pl.BlockSpec((tm, D_MODEL), lambda i: (i, 0)), wspec, wspec, vspec, vspec, vspec],
        out_specs=[ospec, ospec],
        out_shape=[oshape, oshape],
        compiler_params=_cparams(("parallel",)),
        name="rg_gates",
    )(xc, w_a, w_x, b_a, b_x, lam)


SUBLANES = 8


def _scan_groups(a, b, rev):
    rows = a.shape[0]
    r = lax.broadcasted_iota(jnp.int32, (rows, 1), 0) % SUBLANES
    s = 1
    while s < SUBLANES:
        shift = rows - s if rev else s
        keep = (r < SUBLANES - s) if rev else (r >= s)
        a_s = jnp.where(keep, pltpu.roll(a, shift, 0), 1.0)
        b_s = jnp.where(keep, pltpu.roll(b, shift, 0), 0.0)
        b = b + a * b_s
        a = a * a_s
        s *= 2
    return a, b


def _rg_scan_kernel(fblk, bblk, first, last, seqo, s0i, has0,
                    af_ref, bf_ref, ab_ref, bb_ref, h0_ref, hf_ref, hb_ref, hc_scr):
    i = pl.program_id(0)
    rows = af_ref.shape[0]
    ngroups = rows // SUBLANES

    @pl.when(first[i] == 1)
    def _():
        hc_scr[...] = jnp.where(has0[i] == 1, h0_ref[...], 0.0)

    for d, (a_ref, b_ref, o_ref) in enumerate(((af_ref, bf_ref, hf_ref), (ab_ref, bb_ref, hb_ref))):
        rev = d == 1
        decay, local = _scan_groups(a_ref[...], b_ref[...], rev)
        h = hc_scr[d:d + 1, :]
        for g in (range(ngroups - 1, -1, -1) if rev else range(ngroups)):
            sl = slice(g * SUBLANES, (g + 1) * SUBLANES)
            hg = decay[sl] * h + local[sl]
            o_ref[sl, :] = hg
            h = hg[0:1, :] if rev else hg[SUBLANES - 1:SUBLANES, :]
        hc_scr[d:d + 1, :] = h


def rg_scan(a, bx, h0, cfg):
    t = a.shape[1]
    rb = SEQ_TILE
    tabs = _seq_block_tables(cfg)
    nslots = tabs[0].shape[0]
    fwd = lambda arr: pl.BlockSpec((None, rb, D_MODEL), lambda i, fb, bb, *_: (0, fb[i], 0))
    bwd = lambda arr: pl.BlockSpec((None, rb, D_MODEL), lambda i, fb, bb, *_: (1, bb[i], 0))
    grid_spec = pltpu.PrefetchScalarGridSpec(
        num_scalar_prefetch=7,
        grid=(nslots,),
        in_specs=[
            fwd(a), fwd(bx), bwd(a), bwd(bx),
            pl.BlockSpec((None, 2, D_MODEL), lambda i, fb, bb, fi, la, so, s0i, *_: (s0i[i], 0, 0)),
        ],
        out_specs=[
            pl.BlockSpec((rb, D_MODEL), lambda i, fb, bb, *_: (fb[i], 0)),
            pl.BlockSpec((rb, D_MODEL), lambda i, fb, bb, *_: (bb[i], 0)),
        ],
        scratch_shapes=[pltpu.VMEM((2, D_MODEL), F32)],
    )
    oshape = jax.ShapeDtypeStruct((t, D_MODEL), F32)
    return pl.pallas_call(
        _rg_scan_kernel,
        grid_spec=grid_spec,
        out_shape=[oshape, oshape],
        compiler_params=_cparams(("arbitrary",)),
        name="rg_scan",
    )(*tabs, a, bx, a, bx, h0)


def _gelu_tanh(x):
    return 0.5 * x * (1.0 + jnp.tanh(math.sqrt(2.0 / math.pi) * (x + 0.044715 * (x * x * x))))


def _rg_out_kernel(gate_ref, hf_ref, hb_ref, w_ref, x_ref, m_ref, o_ref):
    y = (_gelu_tanh(gate_ref[...]) * (hf_ref[...] + hb_ref[...])).astype(BF16)
    o_ref[...] = x_ref[...] + m_ref[2:3, :] * _dot(y, w_ref[...])


def _tok_spec(tm, cb=0):
    return pl.BlockSpec((tm, D_MODEL), lambda i: (i, cb))


def _mtab_spec(layer, cfg, tm):
    return pl.BlockSpec((None, None, 6, D_MODEL), lambda i: (layer, _mod_idx(i, cfg, tm), 0, 0))


def _w_spec(k=D_MODEL):
    return pl.BlockSpec((k, D_MODEL), lambda i: (0, 0))


def rg_out(gx, hf, hb, w_out, x, mtab, layer, cfg):
    t = x.shape[0]
    tm = ROW_TILE
    return pl.pallas_call(
        _rg_out_kernel,
        grid=(t // tm,),
        in_specs=[_tok_spec(tm, 0), _tok_spec(tm), _tok_spec(tm), _w_spec(), _tok_spec(tm),
                  _mtab_spec(layer, cfg, tm)],
        out_specs=_tok_spec(tm),
        out_shape=jax.ShapeDtypeStruct((t, D_MODEL), F32),
        compiler_params=_cparams(("parallel",)),
        name="rg_out",
    )(gx, hf, hb, w_out, x, mtab)


def rglru_layer(x, mtab, layer, j, norm_g, p, state_rglru, cfg):
    gx = mod_matmul(x, mtab, layer, 0, norm_g, p['rg_w_in'][j].astype(BF16), None, cfg)
    xc = dwconv(gx, D_MODEL, p['rg_conv_w'][j], p['rg_conv_b'][j], cfg)
    a, bx = rg_gates(xc, p['rg_w_a'][j].astype(BF16), p['rg_w_x'][j].astype(BF16),
                     p['rg_b_a'][j], p['rg_b_x'][j], p['rg_lambda'][j])
    hf, hb = rg_scan(a, bx, state_rglru[:, j].astype(F32), cfg)
    new_state = jnp.stack([hf[:cfg.tp].reshape(cfg.bp, cfg.lp, D_MODEL)[:, -1],
                           hb[:cfg.tp].reshape(cfg.bp, cfg.lp, D_MODEL)[:, 0]], axis=1)
    x = rg_out(gx, hf, hb, p['rg_w_out'][j].astype(BF16), x, mtab, layer, cfg)
    return x, new_state


def _chunk_cumsum(x, rev):
    rows = x.shape[0]
    r = lax.broadcasted_iota(jnp.int32, x.shape, 0) % HG_CHUNK
    s = 1
    while s < HG_CHUNK:
        if rev:
            x = x + jnp.where(r < HG_CHUNK - s, pltpu.roll(x, rows - s, 0), 0.0)
        else:
            x = x + jnp.where(r >= s, pltpu.roll(x, s, 0), 0.0)
        s *= 2
    return x


def _gla_kernel(fblk, bblk, first, last, seqo, s0i, has0,
                qf_ref, ff_ref, vf_ref, qb_ref, fb_ref, vb_ref, lb_ref, s0_ref,
                of_ref, ob_ref, sfin_ref, s_scr, *, layer):
    i = pl.program_id(1)
    hpb, rows = qf_ref.shape[0], qf_ref.shape[1]
    nch = rows // HG_CHUNK
    units = [(hh, d) for hh in range(hpb) for d in range(2)]

    @pl.when(first[i] == 1)
    def _():
        for hh, d in units:
            s_scr[hh, d] = jnp.where(has0[i] == 1, s0_ref[d, hh].T, 0.0)

    ri = lax.broadcasted_iota(jnp.int32, (rows, rows), 0)
    ci = lax.broadcasted_iota(jnp.int32, (rows, rows), 1)
    same = (ri // HG_CHUNK) == (ci // HG_CHUNK)
    chunks = [slice(n * HG_CHUNK, (n + 1) * HG_CHUNK) for n in range(nch)]
    in_refs = ((qf_ref, ff_ref, vf_ref), (qb_ref, fb_ref, vb_ref))
    out_refs = (of_ref, ob_ref)

    pre = {}
    for hh in range(hpb):
        lbx = lb_ref[hh]
        e = jnp.exp(lbx - jnp.max(lbx, axis=0, keepdims=True))
        sm = e / jnp.sum(e, axis=0, keepdims=True)
        lb = jnp.zeros(lbx.shape[1:], F32)
        for l in range(1, layer + 1):
            lb = lb + sm[l]
        for d in range(2):
            q_ref, f_ref, v_ref = in_refs[d]
            rev = d == 1
            v16 = v_ref[hh].astype(BF16)
            lbd = lb[d:d + 1, :]
            f = lbd + (1.0 - lbd) * jax.nn.sigmoid(f_ref[hh])
            k = 1.0 - f
            b = _chunk_cumsum(jnp.log(f), rev)
            b3 = b.reshape(nch, HG_CHUNK, HG_DK)
            bl = b3[:, 0:1, :] if rev else b3[:, HG_CHUNK - 1:HG_CHUNK, :]
            qi16 = (q_ref[hh] * jnp.exp(b)).astype(BF16)
            ki16 = (k * jnp.exp(-b)).astype(BF16)
            ks16 = (k.reshape(nch, HG_CHUNK, HG_DK) * jnp.exp(bl - b3)).reshape(rows, HG_DK).astype(BF16)
            g = jnp.exp(bl)
            mask = same & ((ci >= ri) if rev else (ci <= ri))
            att = jnp.where(mask, _dot_nt(qi16, ki16), 0.0).astype(BF16)
            ds = [_dot_tn(v16[sl], ks16[sl]) for sl in chunks]
            pre[hh, d] = (qi16, v16, att, ds, g)

    prev = {}
    for hh, d in units:
        _, _, _, ds, g = pre[hh, d]
        st = s_scr[hh, d]
        sp = [None] * nch
        for n in (range(nch - 1, -1, -1) if d == 1 else range(nch)):
            sp[n] = st.astype(BF16)
            st = st * g[n] + ds[n]
        s_scr[hh, d] = st
        prev[hh, d] = sp

    for hh, d in units:
        qi16, v16, att, _, _ = pre[hh, d]
        inter = [_dot_nt(qi16[sl], prev[hh, d][n]) for n, sl in enumerate(chunks)]
        out_refs[d][hh] = _dot(att, v16) + jnp.concatenate(inter, axis=0)

    @pl.when(last[i] == 1)
    def _():
        for hh, d in units:
            sfin_ref[d, hh] = s_scr[hh, d].T


def _seq_block_tables(cfg):
    rb = SEQ_TILE
    pp, ps = cfg.lp // rb, cfg.ls // rb
    fblk, bblk, first, last, seqo, s0i, has0 = [], [], [], [], [], [], []
    for s in range(cfg.bp):
        for c in range(pp):
            fblk.append(s * pp + c); bblk.append(s * pp + pp - 1 - c)
            first.append(int(c == 0)); last.append(int(c == pp - 1))
            seqo.append(s); s0i.append(0); has0.append(0)
    base = cfg.tp // rb
    for s in range(cfg.bs):
        for c in range(ps):
            fblk.append(base + s * ps + c); bblk.append(base + s * ps + ps - 1 - c)
            first.append(int(c == 0)); last.append(int(c == ps - 1))
            seqo.append(cfg.bp); s0i.append(s); has0.append(1)
    return [jnp.asarray(np.asarray(v, np.int32)) for v in (fblk, bblk, first, last, seqo, s0i, has0)]


def gla(proj, hgrn_lb, s0, layer, cfg):
    t = proj.shape[1]
    rb = SEQ_TILE
    tabs = _seq_block_tables(cfg)
    nslots = tabs[0].shape[0]
    h8 = HG_HEADS
    hpb = GLA_HEADS_PER_STEP
    ng = h8 // hpb

    def pspec(sec, which):
        return pl.BlockSpec((hpb, rb, HG_DK),
                            lambda h, i, fb, bb, *_: (sec * ng + h, (fb if which == 0 else bb)[i], 0))

    ospec_f = pl.BlockSpec((hpb, rb, HG_DK), lambda h, i, fb, bb, *_: (h, fb[i], 0))
    ospec_b = pl.BlockSpec((hpb, rb, HG_DK), lambda h, i, fb, bb, *_: (h, bb[i], 0))
    grid_spec = pltpu.PrefetchScalarGridSpec(
        num_scalar_prefetch=7,
        grid=(ng, nslots),
        in_specs=[
            pspec(0, 0), pspec(1, 0), pspec(3, 0),
            pspec(0, 1), pspec(2, 1), pspec(3, 1),
            pl.BlockSpec((hpb, DEPTH, 2, HG_DK), lambda h, i, *_: (h, 0, 0, 0)),
            pl.BlockSpec((None, 2, hpb, HG_DK, HG_DK),
                         lambda h, i, fb, bb, fi, la, so, s0i, *_: (s0i[i], 0, h, 0, 0)),
        ],
        out_specs=[
            ospec_f, ospec_b,
            pl.BlockSpec((None, 2, hpb, HG_DK, HG_DK),
                         lambda h, i, fb, bb, fi, la, so, *_: (so[i], 0, h, 0, 0)),
        ],
        scratch_shapes=[pltpu.VMEM((hpb, 2, HG_DK, HG_DK), F32)],
    )
    oshape = jax.ShapeDtypeStruct((h8, t, HG_DK), F32)
    of, ob, sfin = pl.pallas_call(
        functools.partial(_gla_kernel, layer=layer),
        grid_spec=grid_spec,
        out_shape=[oshape, oshape,
                   jax.ShapeDtypeStruct((cfg.bp + 1, 2, h8, HG_DK, HG_DK), F32)],
        compiler_params=_cparams(("parallel", "arbitrary")),
        name="gla",
    )(*tabs, proj, proj, proj, proj, proj, proj,
      hgrn_lb.reshape(DEPTH, 2, h8, HG_DK).transpose(2, 0, 1, 3), s0)
    return of, ob, sfin[:cfg.bp]


def _head_rms(o, g):
    ms = jnp.mean(o * o, axis=-1, keepdims=True)
    return o * lax.rsqrt(ms + EPS) * g


def _hg_out_kernel(of_ref, ob_ref, gh_ref, ng_ref, w_ref, x_ref, m_ref, o_ref):
    parts = []
    for h in range(HG_HEADS):
        o = _head_rms(of_ref[h] + ob_ref[h], ng_ref[...]) * _silu(gh_ref[h])
        parts.append(o.astype(BF16))
    y = jnp.concatenate(parts, axis=1)
    o_ref[...] = x_ref[...] + m_ref[2:3, :] * _dot(y, w_ref[...])


def hg_out(of, ob, proj, norm_g, w_out, x, mtab, layer, cfg):
    t = x.shape[0]
    tm = ROW_TILE
    hspec = pl.BlockSpec((HG_HEADS, tm, HG_DK), lambda i: (0, i, 0))
    return pl.pallas_call(
        _hg_out_kernel,
        grid=(t // tm,),
        in_specs=[hspec, hspec,
                  pl.BlockSpec((HG_HEADS, tm, HG_DK), lambda i: (4, i, 0)),
                  pl.BlockSpec((1, HG_DK), lambda i: (0, 0)),
                  _w_spec(), _tok_spec(tm), _mtab_spec(layer, cfg, tm)],
        out_specs=_tok_spec(tm),
        out_shape=jax.ShapeDtypeStruct((t, D_MODEL), F32),
        compiler_params=_cparams(("parallel",)),
        name="hg_out",
    )(of, ob, proj, norm_g.reshape(1, HG_DK), w_out, x, mtab)


def hgrn2_layer(x, mtab, layer, j, norm_g, p, state_hgrn, cfg):
    proj = mod_matmul(x, mtab, layer, 0, norm_g, p['hg_w_in'][j].astype(BF16), None, cfg, heads_out=True)
    of, ob, sfin = gla(proj, p['hgrn_lb'], state_hgrn[:, j].astype(F32), layer, cfg)
    x = hg_out(of, ob, proj, p['hg_norm_g'][j], p['hg_w_out'][j].astype(BF16), x, mtab, layer, cfg)
    return x, sfin


def _dft_tables(length):
    a, b = _dft_tables_np(length)
    return jnp.asarray(a), jnp.asarray(b), jnp.asarray(np.ascontiguousarray(b.T))


@functools.lru_cache(maxsize=None)
def _dft_tables_np(length):
    n = 2 * length
    idx = np.arange(length, dtype=np.int64)
    ang = ((idx[:, None] * idx[None, :]) % n).astype(np.float64) * (2.0 * math.pi / n)
    a = np.cos(ang)
    b = -np.sin(ang)
    b[0, :] = np.where(idx % 2 == 0, 1.0, -1.0)
    return a.astype(BF16), b.astype(BF16)


@functools.lru_cache(maxsize=None)
def _hy_features_np(length):
    t_idx = np.arange(length, dtype=np.float64)
    bands = (HY_EMB - 1) // 2
    fr = np.linspace(1e-4, bands - 1, bands)
    ang = (2.0 * math.pi * t_idx / length)[:, None] * fr[None, :]
    z = np.zeros((length, LANES), np.float32)
    z[:, 0] = t_idx / (length - 1)
    z[:, 1:1 + bands] = np.cos(ang)
    z[:, 1 + bands:HY_EMB] = -np.sin(ang)
    return z


def _hy_features(length):
    return jnp.asarray(_hy_features_np(length))


def _hy_filter_kernel(z_ref, w1_ref, b1_ref, w2_ref, b2_ref, w3f_ref, w3b_ref, fq_ref, dcf_ref, dcb_ref,
                      a_ref, b_ref, ka_ref, ki_ref, kn_ref):
    length = z_ref.shape[0]
    fq = fq_ref[...]
    h = jnp.sin(fq * (_dot_hi(z_ref[...], w1_ref[...]) + b1_ref[...]))
    h = jnp.sin(fq * (_dot_hi(h, w2_ref[...]) + b2_ref[...]))
    row = lax.broadcasted_iota(jnp.int32, (length, 1), 0)
    tt = row.astype(F32) / float(length - 1)
    hf = _dot_hi(h, w3f_ref[...]) * jnp.exp(-tt * jnp.abs(dcf_ref[...]))
    hb = _dot_hi(h, w3b_ref[...]) * jnp.exp(-tt * jnp.abs(dcb_ref[...]))
    hb = jnp.where(row == 0, 0.0, hb)
    nrm = lax.rsqrt(jnp.sum(hf * hf + hb * hb, axis=0, keepdims=True) + EPS)
    hf = hf * nrm
    hb = hb * nrm
    a = a_ref[...]
    bm = b_ref[...]
    ka_ref[...] = _dot(a, (hf + hb).astype(BF16))
    kbf = _dot(bm, hf.astype(BF16))
    kbb = _dot(bm, hb.astype(BF16))
    ki_ref[...] = jnp.where(row == 0, 0.0, kbf - kbb)
    kn_ref[...] = jnp.broadcast_to(kbf[0:1, :] + kbb[0:1, :], kn_ref.shape)


def _const_spec(shape):
    nd = len(shape)
    return pl.BlockSpec(shape, lambda *_: (0,) * nd, pipeline_mode=pl.Buffered(1))


def hy_filter_spectrum(length, a_tab, b_tab, p, j, tc=256):
    d = D_MODEL
    nct = d // tc
    w1 = jnp.zeros((LANES, HY_FW), F32).at[:HY_EMB].set(p['hy_f_w1'][j])
    w3 = p['hy_f_w3'][j]
    dec = p['hy_decay'][j].reshape(1, 4 * d)
    z = _hy_features(length)
    fwd_col = lambda o, c: (0, (o * 2) * nct + c)
    bwd_col = lambda o, c: (0, (o * 2 + 1) * nct + c)
    return pl.pallas_call(
        _hy_filter_kernel,
        grid=(2, nct),
        in_specs=[
            _const_spec((length, LANES)), _const_spec((LANES, HY_FW)), _const_spec((1, HY_FW)),
            _const_spec((HY_FW, HY_FW)), _const_spec((1, HY_FW)),
            pl.BlockSpec((HY_FW, tc), fwd_col), pl.BlockSpec((HY_FW, tc), bwd_col),
            _const_spec((1, HY_FW)),
            pl.BlockSpec((1, tc), fwd_col), pl.BlockSpec((1, tc), bwd_col),
            _const_spec((length, length)), _const_spec((length, length)),
        ],
        out_specs=[
            pl.BlockSpec((None, length, tc), lambda o, c: (o, 0, c)),
            pl.BlockSpec((None, length, tc), lambda o, c: (o, 0, c)),
            pl.BlockSpec((None, 8, tc), lambda o, c: (o, 0, c)),
        ],
        out_shape=[jax.ShapeDtypeStruct((2, length, d), F32),
                   jax.ShapeDtypeStruct((2, length, d), F32),
                   jax.ShapeDtypeStruct((2, 8, d), F32)],
        compiler_params=_cparams(("parallel", "parallel")),
        name="hy_filter",
    )(z, w1, p['hy_f_b1'][j].reshape(1, HY_FW), p['hy_f_w2'][j], p['hy_f_b2'][j].reshape(1, HY_FW),
      w3, w3, p['hy_freq'][j].reshape(1, HY_FW), dec, dec, a_tab, b_tab)


def _seq_conv(x, w, b):
    length = x.shape[0]
    taps = w.shape[0]
    left = (taps - 1) // 2
    row = lax.broadcasted_iota(jnp.int32, (length, 1), 0)
    acc = b + w[left:left + 1, :] * x
    for kk in range(taps):
        off = kk - left
        if off != 0:
            shifted = pltpu.roll(x, (-off) % length, 0)
            valid = (row + off >= 0) & (row + off < length)
            acc = acc + w[kk:kk + 1, :] * jnp.where(valid, shifted, 0.0)
    return acc


def _sconv_kernel(*refs, fb, conv_z, aliased):
    (zin_ref, xin_ref, a_ref, b_ref, bt_ref, ka_ref, ki_ref, kn_ref, bias_ref,
     cwz_ref, cbz_ref, cwx_ref, cbx_ref) = refs[:13]
    o_ref, yr_scr, yi_scr, z_ref, x_ref = refs[14:] if aliased else refs[13:]
    length = zin_ref.shape[0]
    z_ref[...] = _seq_conv(zin_ref[...], cwz_ref[...], cbz_ref[...]) if conv_z else zin_ref[...]
    x_ref[...] = _seq_conv(xin_ref[...], cwx_ref[...], cbx_ref[...])
    zb = z_ref[...].astype(BF16)
    inv = 1.0 / (2 * length)
    for kf in range(length // fb):
        rows = slice(kf * fb, (kf + 1) * fb)
        pr = _dot(a_ref[rows, :], zb)
        qi = _dot(b_ref[rows, :], zb)
        ka = ka_ref[rows, :]
        ki = ki_ref[rows, :]
        if kf == 0:
            r0 = lax.broadcasted_iota(jnp.int32, (fb, 1), 0) == 0
            kd = jnp.where(r0, kn_ref[0:1, :], ka)
            wgt = jnp.where(r0, inv, 2.0 * inv)
        else:
            kd = ka
            wgt = 2.0 * inv
        yr_scr[rows, :] = ((pr * ka - qi * ki) * wgt).astype(BF16)
        yi_scr[rows, :] = ((pr * ki + qi * kd) * wgt).astype(BF16)
    for tb in range(length // fb):
        rows = slice(tb * fb, (tb + 1) * fb)
        y = _dot(a_ref[rows, :], yr_scr[...]) + _dot(bt_ref[rows, :], yi_scr[...])
        o_ref[rows, :] = x_ref[rows, :] * (y + bias_ref[...] * z_ref[rows, :])


def sconv_group(zarr, zcol, conv_z, xarr, xcol, conv_w, conv_b, seq0, nseq, length, tabs, ka, ki, kn,
                order, bias, out_prev, tc):
    a_tab, b_tab, bt_tab = tabs
    d = D_MODEL
    t = zarr.shape[0]
    fb = min(256, length)
    zc, xc = zcol // tc, xcol // tc
    taps = conv_w.shape[0]
    cb = conv_b.reshape(1, -1)
    in_specs = [
        pl.BlockSpec((length, tc), lambda s, c: (seq0 + s, zc + c)),
        pl.BlockSpec((length, tc), lambda s, c: (seq0 + s, xc + c)),
        _const_spec((length, length)), _const_spec((length, length)), _const_spec((length, length)),
        pl.BlockSpec((None, length, tc), lambda s, c: (order, 0, c)),
        pl.BlockSpec((None, length, tc), lambda s, c: (order, 0, c)),
        pl.BlockSpec((None, 8, tc), lambda s, c: (order, 0, c)),
        pl.BlockSpec((None, 1, tc), lambda s, c: (order, 0, c)),
        pl.BlockSpec((taps, tc), lambda s, c: (0, zc + c if conv_z else c)),
        pl.BlockSpec((1, tc), lambda s, c: (0, zc + c if conv_z else c)),
        pl.BlockSpec((taps, tc), lambda s, c: (0, xc + c)),
        pl.BlockSpec((1, tc), lambda s, c: (0, xc + c)),
    ]
    args = [zarr, xarr, a_tab, b_tab, bt_tab, ka, ki, kn, bias.reshape(2, 1, d), conv_w, cb, conv_w, cb]
    aliases = {}
    if out_prev is not None:
        in_specs.append(pl.BlockSpec(memory_space=pl.ANY))
        args.append(out_prev)
        aliases = {len(args) - 1: 0}
    return pl.pallas_call(
        functools.partial(_sconv_kernel, fb=fb, conv_z=conv_z, aliased=out_prev is not None),
        grid=(nseq, d // tc),
        in_specs=in_specs,
        out_specs=pl.BlockSpec((length, tc), lambda s, c: (seq0 + s, c)),
        out_shape=jax.ShapeDtypeStruct((t, d), F32),
        scratch_shapes=[pltpu.VMEM((length, tc), BF16), pltpu.VMEM((length, tc), BF16),
                        pltpu.VMEM((length, tc), F32), pltpu.VMEM((length, tc), F32)],
        input_output_aliases=aliases,
        compiler_params=_cparams(("parallel", "parallel")),
        name="sconv",
    )(*args)


def _lin_out_kernel(a_ref, w_ref, b_ref, x_ref, m_ref, o_ref):
    y = _dot(a_ref[...].astype(BF16), w_ref[...]) + b_ref[...]
    o_ref[...] = x_ref[...] + m_ref[2:3, :] * y


def lin_out(a, w, b, x, mtab, layer, cfg):
    t = x.shape[0]
    tm = ROW_TILE
    return pl.pallas_call(
        _lin_out_kernel,
        grid=(t // tm,),
        in_specs=[_tok_spec(tm), _w_spec(), pl.BlockSpec((1, D_MODEL), lambda i: (0, 0)),
                  _tok_spec(tm), _mtab_spec(layer, cfg, tm)],
        out_specs=_tok_spec(tm),
        out_shape=jax.ShapeDtypeStruct((t, D_MODEL), F32),
        compiler_params=_cparams(("parallel",)),
        name="lin_out",
    )(a, w, b.reshape(1, D_MODEL), x, mtab)


def hyena_layer(x, mtab, layer, j, norm_g, p, cfg):
    d = D_MODEL
    pre = mod_matmul(x, mtab, layer, 0, norm_g, p['hy_w_in'][j].astype(BF16), p['hy_b_in'][j], cfg)
    cw, cb, bias = p['hy_conv_w'][j], p['hy_conv_b'][j], p['hy_bias'][j]
    groups = []
    for (row0, nseq, length) in ((0, cfg.bp, cfg.lp), (cfg.tp, cfg.bs, cfg.ls)):
        tabs = _dft_tables(length)
        spec = hy_filter_spectrum(length, tabs[0], tabs[1], p, j)
        tc = 256 if length > 512 else d
        groups.append((row0 // length, nseq, length, tabs, spec, tc))
    z1 = None
    for seq0, nseq, length, tabs, (ka, ki, kn), tc in groups:
        z1 = sconv_group(pre, 0, True, pre, d, cw, cb, seq0, nseq, length, tabs, ka, ki, kn, 0, bias, z1, tc)
    z2 = None
    for seq0, nseq, length, tabs, (ka, ki, kn), tc in groups:
        z2 = sconv_group(z1, 0, False, pre, 2 * d, cw, cb, seq0, nseq, length, tabs, ka, ki, kn, 1, bias,
                         z2, tc)
    return lin_out(z2, p['hy_w_out'][j].astype(BF16), p['hy_b_out'][j], x, mtab, layer, cfg)


def _group_rms_scale(x, g1, g1t):
    x2 = x * x
    hi = x2.astype(BF16)
    lo = (x2 - hi.astype(F32)).astype(BF16)
    s = _dot(hi, g1) + _dot(lo, g1)
    r = lax.rsqrt(s * (1.0 / DA_DH) + EPS)
    rhi = r.astype(BF16)
    rlo = (r - rhi.astype(F32)).astype(BF16)
    return _dot(rhi, g1t) + _dot(rlo, g1t)


def _rope(x, cos, sin_signed):
    lane = lax.broadcasted_iota(jnp.int32, x.shape, 1)
    w = x.shape[1]
    nf = DA_DH // 4
    rot = jnp.where(lane % (2 * nf) < nf, pltpu.roll(x, w - nf, 1), pltpu.roll(x, nf, 1))
    return x * cos + rot * sin_signed


def _qk_prep_kernel(q_ref, k_ref, v_ref, qg_ref, kg_ref, g1_ref, g1t_ref, cos_ref, sin_ref,
                    qh_ref, kh_ref, vh_ref, kc_ref, *, cfg):
    tm = q_ref.shape[0]
    i = pl.program_id(0)
    is_prompt = i < cfg.tp // tm
    g1 = g1_ref[...]
    g1t = g1t_ref[...]
    q = q_ref[...]
    k = k_ref[...]
    qn = q * _group_rms_scale(q, g1, g1t) * qg_ref[...]
    kn = k * _group_rms_scale(k, g1, g1t) * kg_ref[...]

    @pl.when(is_prompt)
    def _():
        kc_ref[...] = kn

    reps = q.shape[1] // LANES
    cos = jnp.tile(cos_ref[...], (1, reps))
    sin = jnp.tile(sin_ref[...], (1, reps))
    qo = (jnp.where(is_prompt, qn, _rope(qn, cos, sin)) * (DA_DH ** -0.5)).astype(BF16)
    ko = jnp.where(is_prompt, kn, _rope(kn, cos, sin)).astype(BF16)
    vo = v_ref[...].astype(BF16)
    for h in range(DA_HEADS):
        sl = slice(h * LANES, (h + 1) * LANES)
        qh_ref[h] = qo[:, sl]
        kh_ref[h] = ko[:, sl]
        vh_ref[h] = vo[:, sl]


@functools.lru_cache(maxsize=None)
def _rope_tables_np(length):
    nf = DA_DH // 4
    t = np.arange(length)
    pos = np.stack([t // GRID_W, t % GRID_W], axis=-1).astype(np.float64)
    inv = ROPE_BASE ** (-np.arange(nf, dtype=np.float64) / nf)
    ang = pos[:, :, None] * inv
    cos = np.broadcast_to(np.cos(ang)[:, :, None, :], (length, 2, 2, nf)).reshape(length, DA_DH)
    sin = np.sin(ang)[:, :, None, :]
    sin = np.concatenate([-sin, sin], axis=2).reshape(length, DA_DH)
    reps = LANES // DA_DH
    return (np.tile(cos, (1, reps)).astype(np.float32), np.tile(sin, (1, reps)).astype(np.float32))


def _rope_tables(length):
    cos, sin = _rope_tables_np(length)
    return jnp.asarray(cos), jnp.asarray(sin)


def qk_prep(qkv, q_g, k_g, cfg):
    t = qkv.shape[0]
    tm = ROW_TILE
    d = D_MODEL
    ngrp = d // DA_DH
    grp = jnp.arange(d) // DA_DH
    g1 = (grp[:, None] == jnp.arange(LANES)[None, :]).astype(BF16)
    g1t = g1.T
    cos, sin = _rope_tables(cfg.ls)
    npt = cfg.tp // tm
    pps = cfg.ls // tm
    tab_spec = pl.BlockSpec((tm, LANES), lambda i: (jnp.maximum(i - npt, 0) % pps, 0))
    hspec = pl.BlockSpec((DA_HEADS, tm, LANES), lambda i: (0, i, 0))
    hshape = jax.ShapeDtypeStruct((DA_HEADS, t, LANES), BF16)
    return pl.pallas_call(
        functools.partial(_qk_prep_kernel, cfg=cfg),
        grid=(t // tm,),
        in_specs=[_tok_spec(tm, 0), _tok_spec(tm, 1), _tok_spec(tm, 2),
                  pl.BlockSpec((1, d), lambda i: (0, 0)), pl.BlockSpec((1, d), lambda i: (0, 0)),
                  pl.BlockSpec((d, LANES), lambda i: (0, 0)), pl.BlockSpec((LANES, d), lambda i: (0, 0)),
                  tab_spec, tab_spec],
        out_specs=[hspec, hspec, hspec,
                   pl.BlockSpec((tm, d), lambda i: (jnp.minimum(i, npt - 1), 0))],
        out_shape=[hshape, hshape, hshape, jax.ShapeDtypeStruct((cfg.tp, d), F32)],
        compiler_params=_cparams(("arbitrary",)),
        name="qk_prep",
    )(qkv, qkv, qkv, jnp.tile(q_g, ngrp).reshape(1, d), jnp.tile(k_g, ngrp).reshape(1, d),
      g1, g1t, cos, sin)


def _dattn_kernel(*refs, has_cache, aliased, lam_init):
    n_in = 4 + (2 if has_cache else 0)
    lp_ref, q_ref, k_ref, v_ref = refs[:4]
    o_ref = refs[n_in + (1 if aliased else 0)]
    lp = lp_ref[...]
    lam = (jnp.exp(jnp.sum(lp[0:1] * lp[1:2], axis=1, keepdims=True))
           - jnp.exp(jnp.sum(lp[2:3] * lp[3:4], axis=1, keepdims=True)) + lam_init)
    lane = lax.broadcasted_iota(jnp.int32, q_ref.shape[1:], 1)
    for hh in range(q_ref.shape[0]):
        q = q_ref[hh]
        zero = jnp.zeros_like(q)
        qs = (jnp.where(lane < DA_DH, q, zero), jnp.where(lane >= DA_DH, q, zero))
        k = k_ref[hh]
        v = v_ref[hh]
        if has_cache:
            ck = refs[4][:, hh * LANES:(hh + 1) * LANES].astype(BF16)
            cv = refs[5][:, hh * LANES:(hh + 1) * LANES].astype(BF16)
        halves = []
        for half in range(2):
            s = _dot_nt(qs[half], k)
            m = jnp.max(s, axis=-1, keepdims=True)
            if has_cache:
                c = _dot_nt(qs[half], ck)
                m = jnp.maximum(m, jnp.max(c, axis=-1, keepdims=True))
            e = jnp.exp(s - m)
            z = jnp.sum(e, axis=-1, keepdims=True)
            pv = _dot(e.astype(BF16), v)
            if has_cache:
                ec = jnp.exp(c - m)
                z = z + jnp.sum(ec, axis=-1, keepdims=True)
                pv = pv + _dot(ec.astype(BF16), cv)
            halves.append(pv / z)
        o_ref[hh] = halves[0] - lam * halves[1]


def dattn_group(qh, kh, vh, da_lambda, lam_init, seq0, nseq, length, cache_k=None, cache_v=None,
                prev=None, tq=256):
    t = qh.shape[1]
    tq = min(tq, length)
    nq = length // tq
    has_cache = cache_k is not None
    hpb = DA_HEADS_PER_STEP
    in_specs = [
        pl.BlockSpec((4, DA_DH), lambda b, h, qi: (0, 0)),
        pl.BlockSpec((hpb, tq, LANES), lambda b, h, qi: (h, (seq0 + b) * nq + qi, 0)),
        pl.BlockSpec((hpb, length, LANES), lambda b, h, qi: (h, seq0 + b, 0)),
        pl.BlockSpec((hpb, length, LANES), lambda b, h, qi: (h, seq0 + b, 0)),
    ]
    args = [da_lambda, qh, kh, vh]
    if has_cache:
        past = cache_k.shape[1]
        cspec = pl.BlockSpec((None, past, hpb * LANES), lambda b, h, qi: (b, 0, h))
        in_specs += [cspec, cspec]
        args += [cache_k.reshape(nseq, past, DA_HEADS * LANES), cache_v.reshape(nseq, past, DA_HEADS * LANES)]
    aliases = {}
    if prev is not None:
        in_specs.append(pl.BlockSpec(memory_space=pl.ANY))
        args.append(prev)
        aliases = {len(args) - 1: 0}
    return pl.pallas_call(
        functools.partial(_dattn_kernel, has_cache=has_cache, aliased=prev is not None, lam_init=lam_init),
        grid=(nseq, DA_HEADS // hpb, nq),
        in_specs=in_specs,
        out_specs=pl.BlockSpec((hpb, tq, LANES), lambda b, h, qi: (h, (seq0 + b) * nq + qi, 0)),
        out_shape=jax.ShapeDtypeStruct((DA_HEADS, t, LANES), F32),
        input_output_aliases=aliases,
        compiler_params=_cparams(("parallel", "parallel", "parallel")),
        name="dattn",
    )(*args)


def _da_out_kernel(o_ref, sg_ref, w_ref, x_ref, m_ref, out_ref, *, lam_init):
    parts = []
    for h in range(DA_HEADS):
        parts.append((_head_rms(o_ref[h], sg_ref[...]) * (1.0 - lam_init)).astype(BF16))
    y = jnp.concatenate(parts, axis=1)
    out_ref[...] = x_ref[...] + m_ref[2:3, :] * _dot(y, w_ref[...])


def da_out(o, sub_g, w_out, x, mtab, layer, lam_init, cfg):
    t = x.shape[0]
    tm = ROW_TILE
    return pl.pallas_call(
        functools.partial(_da_out_kernel, lam_init=lam_init),
        grid=(t // tm,),
        in_specs=[pl.BlockSpec((DA_HEADS, tm, LANES), lambda i: (0, i, 0)),
                  pl.BlockSpec((1, LANES), lambda i: (0, 0)),
                  _w_spec(), _tok_spec(tm), _mtab_spec(layer, cfg, tm)],
        out_specs=_tok_spec(tm),
        out_shape=jax.ShapeDtypeStruct((t, D_MODEL), F32),
        compiler_params=_cparams(("parallel",)),
        name="da_out",
    )(o, sub_g.reshape(1, LANES), w_out, x, mtab)


def diffattn_layer(x, mtab, layer, j, norm_g, p, cache_k, cache_v, cfg):
    d = D_MODEL
    lam_init = 0.8 - 0.6 * math.exp(-0.3 * layer)
    qkv = mod_matmul(x, mtab, layer, 0, norm_g, p['da_w_in'][j].astype(BF16), None, cfg)
    qh, kh, vh, kc = qk_prep(qkv, p['da_q_norm'][j], p['da_k_norm'][j], cfg)
    lamp = p['da_lambda'][j].astype(F32)
    o = dattn_group(qh, kh, vh, lamp, lam_init, 0, cfg.bp, cfg.lp)
    o = dattn_group(qh, kh, vh, lamp, lam_init, cfg.tp // cfg.ls, cfg.bs, cfg.ls,
                    cache_k[:, j], cache_v[:, j], prev=o)
    x = da_out(o, p['da_sub_norm'][j], p['da_w_out'][j].astype(BF16), x, mtab, layer, lam_init, cfg)
    new_k = kc.reshape(cfg.bp, cfg.lp, DA_HEADS, 2 * DA_DH)
    new_v = qkv[:cfg.tp, 2 * d:].reshape(cfg.bp, cfg.lp, DA_HEADS, 2 * DA_DH)
    return x, new_k, new_v


def _ffn_kernel(x_ref, m_ref, g_ref, wg_ref, wh_ref, wo_ref, o_ref, u_scr, acc_scr):
    f = pl.program_id(1)

    @pl.when(f == 0)
    def _():
        u_scr[...] = _modulate(x_ref[...], g_ref[...], m_ref[...], 3).astype(BF16)
        acc_scr[...] = jnp.zeros_like(acc_scr)

    u = u_scr[...]
    a = (_silu(_dot(u, wg_ref[...])) * _dot(u, wh_ref[...])).astype(BF16)
    acc_scr[...] += _dot(a, wo_ref[...])

    @pl.when(f == pl.num_programs(1) - 1)
    def _():
        o_ref[...] = x_ref[...] + m_ref[5:6, :] * acc_scr[...]


def dense_ffn(x, mtab, layer, g, w_in, w_out, cfg, tf=1408):
    t = x.shape[0]
    tm = ROW_TILE
    nf = FF_DIM // tf
    return pl.pallas_call(
        _ffn_kernel,
        grid=(t // tm, nf),
        in_specs=[
            pl.BlockSpec((tm, D_MODEL), lambda i, f: (i, 0)),
            pl.BlockSpec((None, None, 6, D_MODEL), lambda i, f: (layer, _mod_idx(i, cfg, tm), 0, 0)),
            pl.BlockSpec((1, D_MODEL), lambda i, f: (0, 0)),
            pl.BlockSpec((D_MODEL, tf), lambda i, f: (0, f)),
            pl.BlockSpec((D_MODEL, tf), lambda i, f: (0, nf + f)),
            pl.BlockSpec((tf, D_MODEL), lambda i, f: (f, 0)),
        ],
        out_specs=pl.BlockSpec((tm, D_MODEL), lambda i, f: (i, 0)),
        out_shape=jax.ShapeDtypeStruct((t, D_MODEL), F32),
        scratch_shapes=[pltpu.VMEM((tm, D_MODEL), BF16), pltpu.VMEM((tm, D_MODEL), F32)],
        compiler_params=_cparams(("parallel", "arbitrary")),
        name="ffn",
    )(x, mtab, g.reshape(1, D_MODEL), w_in, w_in, w_out)


def _moe_pre_kernel(x_ref, m_ref, g_ref, rw_ref, rb_ref, u_ref, r_ref, c_ref, cnt_scr):
    tm = x_ref.shape[0]

    @pl.when(pl.program_id(0) == 0)
    def _():
        cnt_scr[...] = jnp.zeros_like(cnt_scr)

    u = _modulate(x_ref[...], g_ref[...], m_ref[...], 3)
    bits = pltpu.bitcast(u.astype(BF16).astype(F32), jnp.uint32)
    half = D_MODEL // 2
    u_ref[...] = pltpu.bitcast((bits[:, :half] >> 16) | (bits[:, half:] & jnp.uint32(0xFFFF0000)), F32)
    logits = _dot_hi(u, rw_ref[...]) + rb_ref[...]
    lane = lax.broadcasted_iota(jnp.int32, logits.shape, 1)
    neg = -jnp.inf
    lg = jnp.where(lane < MOE_E, logits, neg)
    m1 = jnp.max(lg, axis=-1, keepdims=True)
    i1 = jnp.min(jnp.where(lg == m1, lane, LANES), axis=-1, keepdims=True)
    lg2 = jnp.where(lane == i1, neg, lg)
    m2 = jnp.max(lg2, axis=-1, keepdims=True)
    i2 = jnp.min(jnp.where(lg2 == m2, lane, LANES), axis=-1, keepdims=True)
    e2 = jnp.exp(m2 - m1)
    w1 = 1.0 / (1.0 + e2)
    w2 = e2 / (1.0 + e2)
    hit = jnp.where((lane == i1) | (lane == i2), 1.0, 0.0)
    ri = lax.broadcasted_iota(jnp.int32, (tm, tm), 0)
    ci = lax.broadcasted_iota(jnp.int32, (tm, tm), 1)
    ahead = jnp.where(ci < ri, 1.0, 0.0).astype(BF16)
    pos = _dot(ahead, hit.astype(BF16)) + cnt_scr[...]
    r1 = jnp.sum(jnp.where(lane == i1, pos, 0.0), axis=-1, keepdims=True)
    r2 = jnp.sum(jnp.where(lane == i2, pos, 0.0), axis=-1, keepdims=True)
    cnt = cnt_scr[...] + jnp.sum(hit, axis=0, keepdims=True)
    cnt_scr[...] = cnt
    c_ref[...] = jnp.broadcast_to(cnt, c_ref.shape)
    vals = (i1.astype(F32), i2.astype(F32), w1, w2, r1, r2)
    out = jnp.zeros(logits.shape, F32)
    for col, val in enumerate(vals):
        out = jnp.where(lane == col, val, out)
    r_ref[...] = out


def moe_pre(x, mtab, layer, g, router_w, router_b, cfg):
    t = x.shape[0]
    tm = ROW_TILE
    rw = jnp.zeros((D_MODEL, LANES), F32).at[:, :MOE_E].set(router_w)
    rb = jnp.zeros((1, LANES), F32).at[0, :MOE_E].set(router_b)
    return pl.pallas_call(
        _moe_pre_kernel,
        grid=(t // tm,),
        in_specs=[_tok_spec(tm), _mtab_spec(layer, cfg, tm), pl.BlockSpec((1, D_MODEL), lambda i: (0, 0)),
                  pl.BlockSpec((D_MODEL, LANES), lambda i: (0, 0)), pl.BlockSpec((1, LANES), lambda i: (0, 0))],
        out_specs=[pl.BlockSpec((tm, D_MODEL // 2), lambda i: (i, 0)),
                   pl.BlockSpec((tm, LANES), lambda i: (i, 0)),
                   pl.BlockSpec((8, LANES), lambda i: (0, 0))],
        out_shape=[jax.ShapeDtypeStruct((t, D_MODEL // 2), F32),
                   jax.ShapeDtypeStruct((t, LANES), F32),
                   jax.ShapeDtypeStruct((8, LANES), F32)],
        scratch_shapes=[pltpu.VMEM((1, LANES), F32)],
        compiler_params=_cparams(("arbitrary",)),
        name="moe_pre",
    )(x, mtab, g.reshape(1, D_MODEL), rw, rb)


def _moe_ffn_kernel(te_ref, act_ref, x_ref, wg_ref, wh_ref, wo_ref, o_ref, xb_scr, acc_scr):
    i = pl.program_id(0)
    f = pl.program_id(1)
    active = act_ref[i] == 1

    @pl.when(f == 0)
    def _():
        acc_scr[...] = jnp.zeros_like(acc_scr)

    @pl.when(active & (f == 0))
    def _():
        w = pltpu.bitcast(x_ref[...], jnp.uint32)
        lo = pltpu.bitcast(w << 16, F32)
        hi = pltpu.bitcast(w & jnp.uint32(0xFFFF0000), F32)
        xb_scr[...] = jnp.concatenate([lo, hi], axis=1).astype(BF16)

    @pl.when(active)
    def _():
        u = xb_scr[...]
        tf = wg_ref.shape[1]
        sub = 256
        part = None
        for c in range(tf // sub):
            cs = slice(c * sub, (c + 1) * sub)
            a = (_silu(_dot(u, wg_ref[:, cs].astype(BF16)))
                 * _dot(u, wh_ref[:, cs].astype(BF16))).astype(BF16)
            y = _dot(a, wo_ref[cs, :].astype(BF16))
            part = y if part is None else part + y
        acc_scr[...] += part

    @pl.when(f == pl.num_programs(1) - 1)
    def _():
        o_ref[...] = acc_scr[...]


def moe_ffn(xs, tile_expert, tile_active, w_in, w_out, n, tf=512):
    npad = xs.shape[0]
    tm = MOE_TILE
    nf = MOE_FF // tf
    grid_spec = pltpu.PrefetchScalarGridSpec(
        num_scalar_prefetch=2,
        grid=(npad // tm, nf),
        in_specs=[
            pl.BlockSpec((tm, D_MODEL // 2), lambda i, f, te, ac: (i, 0)),
            pl.BlockSpec((None, None, D_MODEL, tf), lambda i, f, te, ac: (n, te[i], 0, f)),
            pl.BlockSpec((None, None, D_MODEL, tf), lambda i, f, te, ac: (n, te[i], 0, nf + f)),
            pl.BlockSpec((None, None, tf, D_MODEL), lambda i, f, te, ac: (n, te[i], f, 0)),
        ],
        out_specs=pl.BlockSpec((tm, D_MODEL), lambda i, f, te, ac: (i, 0)),
        scratch_shapes=[pltpu.VMEM((tm, D_MODEL), BF16), pltpu.VMEM((tm, D_MODEL), F32)],
    )
    return pl.pallas_call(
        _moe_ffn_kernel,
        grid_spec=grid_spec,
        out_shape=jax.ShapeDtypeStruct((npad, D_MODEL), F32),
        compiler_params=_cparams(("parallel", "arbitrary")),
        name="moe_ffn",
    )(tile_expert, tile_active, xs, w_in, w_in, w_out)


def _moe_combine_kernel(y1_ref, y2_ref, r_ref, x_ref, m_ref, o_ref):
    r = r_ref[...]
    y = r[:, 2:3] * y1_ref[...] + r[:, 3:4] * y2_ref[...]
    o_ref[...] = x_ref[...] + m_ref[5:6, :] * y


def moe_combine(y1, y2, route, x, mtab, layer, cfg):
    t = x.shape[0]
    tm = ROW_TILE
    return pl.pallas_call(
        _moe_combine_kernel,
        grid=(t // tm,),
        in_specs=[_tok_spec(tm), _tok_spec(tm), pl.BlockSpec((tm, LANES), lambda i: (i, 0)),
                  _tok_spec(tm), _mtab_spec(layer, cfg, tm)],
        out_specs=_tok_spec(tm),
        out_shape=jax.ShapeDtypeStruct((t, D_MODEL), F32),
        compiler_params=_cparams(("parallel",)),
        name="moe_combine",
    )(y1, y2, route, x, mtab)


def moe_layer(x, mtab, layer, g, router_w, router_b, w_in, w_out, n, cfg):
    t = x.shape[0]
    tm = MOE_TILE
    u, route, cnt = moe_pre(x, mtab, layer, g, router_w, router_b, cfg)
    counts = cnt[0, :MOE_E].astype(jnp.int32)
    padded = ((counts + tm - 1) // tm) * tm
    ends = jnp.cumsum(padded)
    starts = ends - padded
    e = route[:, 0:2].astype(jnp.int32)
    dest = starts[e] + route[:, 4:6].astype(jnp.int32)
    npad = 2 * t + MOE_E * tm
    tok = jnp.arange(t, dtype=jnp.int32)
    row_tok = jnp.zeros((npad,), jnp.int32).at[jnp.concatenate([dest[:, 0], dest[:, 1]])].set(
        jnp.concatenate([tok, tok]), unique_indices=True)
    tile_start = jnp.arange(npad // tm, dtype=jnp.int32) * tm
    tile_active = (tile_start < ends[-1]).astype(jnp.int32)
    tile_expert = jnp.minimum(jnp.sum((tile_start[:, None] >= ends[None, :]).astype(jnp.int32), axis=1),
                              MOE_E - 1)
    last_e = tile_expert[jnp.maximum(jnp.sum(tile_active) - 1, 0)]
    tile_expert = jnp.where(tile_active == 1, tile_expert, last_e)
    xs = jnp.take(u, row_tok, axis=0, mode='clip')
    ys = moe_ffn(xs, tile_expert, tile_active, w_in, w_out, n)
    y1 = jnp.take(ys, dest[:, 0], axis=0, mode='clip')
    y2 = jnp.take(ys, dest[:, 1], axis=0, mode='clip')
    return moe_combine(y1, y2, route, x, mtab, layer, cfg)


def backbone(x_prompt, x_sample, cache_k, cache_v, state_rglru, state_hgrn, c, c_ctx, p):
    bp, lp, d = x_prompt.shape
    bs, ls, _ = x_sample.shape
    cfg = Cfg(bp, lp, bs, ls)
    x = jnp.concatenate([x_prompt.reshape(bp * lp, d), x_sample.reshape(bs * ls, d)], axis=0)
    cvec = jnp.concatenate([c_ctx[None, :], c], axis=0)
    mtab = modulation_table(cvec, p['mod_w'], p['mod_b'])
    new_k, new_v, new_rg, new_hg = [], [], [], []
    for i in range(DEPTH):
        kind, j = i % 4, i // 4
        g0 = p['norm_g'][i, 0]
        if kind == 0:
            x, st = rglru_layer(x, mtab, i, j, g0, p, state_rglru, cfg)
            new_rg.append(st)
        elif kind == 1:
            x, st = hgrn2_layer(x, mtab, i, j, g0, p, state_hgrn, cfg)
            new_hg.append(st)
        elif kind == 2:
            x = hyena_layer(x, mtab, i, j, g0, p, cfg)
        else:
            x, nk, nv = diffattn_layer(x, mtab, i, j, g0, p, cache_k, cache_v, cfg)
            new_k.append(nk)
            new_v.append(nv)
        n = i // 2
        g1 = p['norm_g'][i, 1]
        if i % 2 == 0:
            x = dense_ffn(x, mtab, i, g1, p['ff_w_in'][n].astype(BF16), p['ff_w_out'][n].astype(BF16), cfg)
        else:
            x = moe_layer(x, mtab, i, g1, p['moe_router'][n], p['moe_router_b'][n],
                          p['moe_w_in'], p['moe_w_out'], n, cfg)
    y_prompt = x[:cfg.tp].reshape(bp, lp, d)
    y_sample = x[cfg.tp:].reshape(bs, ls, d)
    return (y_prompt, y_sample,
            jnp.stack(new_k, axis=1), jnp.stack(new_v, axis=1),
            jnp.stack(new_rg, axis=1), jnp.stack(new_hg, axis=1))


def kernel(x_prompt, x_sample, cache_k, cache_v, state_rglru, state_hgrn, c, c_ctx, mod_w, mod_b, norm_g, hgrn_lb, rg_w_in, rg_conv_w, rg_conv_b, rg_w_a, rg_b_a, rg_w_x, rg_b_x, rg_lambda, rg_w_out, hg_w_in, hg_norm_g, hg_w_out, hy_w_in, hy_b_in, hy_conv_w, hy_conv_b, hy_f_w1, hy_f_b1, hy_f_w2, hy_f_b2, hy_f_w3, hy_freq, hy_decay, hy_bias, hy_w_out, hy_b_out, da_w_in, da_q_norm, da_k_norm, da_lambda, da_sub_norm, da_w_out, ff_w_in, ff_w_out, moe_router, moe_router_b, moe_w_in, moe_w_out):
    p = dict(mod_w=mod_w, mod_b=mod_b, norm_g=norm_g, hgrn_lb=hgrn_lb,
             rg_w_in=rg_w_in, rg_conv_w=rg_conv_w, rg_conv_b=rg_conv_b, rg_w_a=rg_w_a, rg_b_a=rg_b_a,
             rg_w_x=rg_w_x, rg_b_x=rg_b_x, rg_lambda=rg_lambda, rg_w_out=rg_w_out,
             hg_w_in=hg_w_in, hg_norm_g=hg_norm_g, hg_w_out=hg_w_out,
             hy_w_in=hy_w_in, hy_b_in=hy_b_in, hy_conv_w=hy_conv_w, hy_conv_b=hy_conv_b,
             hy_f_w1=hy_f_w1, hy_f_b1=hy_f_b1, hy_f_w2=hy_f_w2, hy_f_b2=hy_f_b2, hy_f_w3=hy_f_w3,
             hy_freq=hy_freq, hy_decay=hy_decay, hy_bias=hy_bias, hy_w_out=hy_w_out, hy_b_out=hy_b_out,
             da_w_in=da_w_in, da_q_norm=da_q_norm, da_k_norm=da_k_norm, da_lambda=da_lambda,
             da_sub_norm=da_sub_norm, da_w_out=da_w_out,
             ff_w_in=ff_w_in, ff_w_out=ff_w_out, moe_router=moe_router, moe_router_b=moe_router_b,
             moe_w_in=moe_w_in, moe_w_out=moe_w_out)
    return backbone(x_prompt, x_sample, cache_k, cache_v, state_rglru, state_hgrn, c, c_ctx, p)
```

```python
import functools
import math
from typing import NamedTuple

import numpy as np
import jax
import jax.numpy as jnp
from jax import lax
from jax.experimental import pallas as pl
from jax.experimental.pallas import tpu as pltpu

F32 = jnp.float32
BF16 = jnp.bfloat16

D_MODEL = 1024
DEPTH = 4
EPS = 1e-6
GRID_W = 64
RG_HEADS = 4
RG_BW = D_MODEL // RG_HEADS
RG_C = 8.0
HG_HEADS = 8
HG_DK = D_MODEL // HG_HEADS
HG_CHUNK = 32
HY_EMB = 33
HY_FW = 64
DA_HEADS = 8
DA_DH = 64
ROPE_BASE = 10000.0
FF_DIM = 2816
MOE_E = 8
MOE_FF = 3584

LANES = 128
SUBLANES = 8
ROW_TILE = 512
MOE_TILE = 1024
SEQ_TILE = 256
GLA_HEADS_PER_STEP = 4
DA_HEADS_PER_STEP = 2
VMEM_LIMIT = 56 * 1024 * 1024


class Cfg(NamedTuple):
    bp: int
    lp: int
    bs: int
    ls: int

    @property
    def tp(self):
        return self.bp * self.lp

    @property
    def ts(self):
        return self.bs * self.ls

    @property
    def t(self):
        return self.tp + self.ts


def _cparams(sem):
    return pltpu.CompilerParams(dimension_semantics=sem, vmem_limit_bytes=VMEM_LIMIT)


def _mod_idx(i, cfg, tm):
    npt = cfg.tp // tm
    return jnp.where(i < npt, 0, 1 + (i - npt) // (cfg.ls // tm))


def _modulate(x, g, m, k):
    ms = jnp.mean(x * x, axis=-1, keepdims=True)
    y = x * lax.rsqrt(ms + EPS) * g
    return y * (1.0 + m[k + 1:k + 2, :]) + m[k:k + 1, :]


def _silu(x):
    return x * jax.nn.sigmoid(x)


def _dot(a, b):
    return jnp.dot(a, b, preferred_element_type=F32)


def _dot_nt(a, b):
    return lax.dot_general(a, b, (((1,), (1,)), ((), ())), preferred_element_type=F32)


def _dot_tn(a, b):
    return lax.dot_general(a, b, (((0,), (0,)), ((), ())), preferred_element_type=F32)


def _dot_hi(a, b):
    return jnp.dot(a, b, preferred_element_type=F32, precision=lax.Precision.HIGHEST)


def _modtab_kernel(c_ref, w_ref, b_ref, o_ref):
    s = _silu(c_ref[...]).astype(BF16)
    o_ref[...] = _dot(s, w_ref[...].astype(BF16)) + b_ref[...]


def modulation_table(cvec, mod_w, mod_b):
    n = cvec.shape[0]
    npad = 16
    cpad = jnp.zeros((npad, D_MODEL), F32).at[:n].set(cvec)
    tn = 1536
    out = pl.pallas_call(
        _modtab_kernel,
        grid=(DEPTH, 6 * D_MODEL // tn),
        in_specs=[
            pl.BlockSpec((npad, D_MODEL), lambda l, j: (0, 0)),
            pl.BlockSpec((None, D_MODEL, tn), lambda l, j: (l, 0, j)),
            pl.BlockSpec((None, 1, tn), lambda l, j: (l, 0, j)),
        ],
        out_specs=pl.BlockSpec((None, npad, tn), lambda l, j: (l, 0, j)),
        out_shape=jax.ShapeDtypeStruct((DEPTH, npad, 6 * D_MODEL), F32),
        compiler_params=_cparams(("parallel", "parallel")),
        name="modtab",
    )(cpad, mod_w, mod_b.reshape(DEPTH, 1, 6 * D_MODEL))
    return out[:, :n].reshape(DEPTH, n, 6, D_MODEL)


def _modmm_kernel(*refs, k, has_bias, heads_out, tn):
    if has_bias:
        x_ref, m_ref, g_ref, w_ref, b_ref, o_ref = refs
    else:
        x_ref, m_ref, g_ref, w_ref, o_ref = refs
        b_ref = None
    u = _modulate(x_ref[...], g_ref[...], m_ref[...], k).astype(BF16)
    for c in range(w_ref.shape[1] // tn):
        cs = slice(c * tn, (c + 1) * tn)
        acc = _dot(u, w_ref[:, cs])
        if has_bias:
            acc = acc + b_ref[:, cs]
        if heads_out:
            for hh in range(tn // LANES):
                o_ref[c * (tn // LANES) + hh] = acc[:, hh * LANES:(hh + 1) * LANES]
        else:
            o_ref[:, cs] = acc


def mod_matmul(x, mtab, layer, k, g, w, b, cfg, heads_out=False, tn=1024):
    t, n = x.shape[0], w.shape[1]
    tm = ROW_TILE
    in_specs = [
        _tok_spec(tm),
        _mtab_spec(layer, cfg, tm),
        _const_spec((1, D_MODEL)),
        _const_spec((D_MODEL, n)),
    ]
    args = [x, mtab, g.reshape(1, D_MODEL), w]
    if b is not None:
        in_specs.append(_const_spec((1, n)))
        args.append(b.reshape(1, n))
    if heads_out:
        out_specs = pl.BlockSpec((n // LANES, tm, LANES), lambda i: (0, i, 0))
        out_shape = jax.ShapeDtypeStruct((n // LANES, t, LANES), F32)
    else:
        out_specs = pl.BlockSpec((tm, n), lambda i: (i, 0))
        out_shape = jax.ShapeDtypeStruct((t, n), F32)
    return pl.pallas_call(
        functools.partial(_modmm_kernel, k=k, has_bias=b is not None, heads_out=heads_out, tn=tn),
        grid=(t // tm,),
        in_specs=in_specs,
        out_specs=out_specs,
        out_shape=out_shape,
        compiler_params=_cparams(("parallel",)),
        name="modmm",
    )(*args)


def _seq_edges(i, cfg, lb):
    npb = cfg.tp // lb
    pp, ps = cfg.lp // lb, cfg.ls // lb
    first = jnp.where(i < npb, i % pp == 0, (i - npb) % ps == 0)
    last = jnp.where(i < npb, i % pp == pp - 1, (i - npb) % ps == ps - 1)
    return first, last


def _halo_conv(x_ref, p_ref, n_ref, w_ref, b_ref, cfg):
    lb = x_ref.shape[0]
    taps = w_ref.shape[0]
    first, last = _seq_edges(pl.program_id(0), cfg, lb)
    prev = jnp.where(first, 0.0, p_ref[...])
    nxt = jnp.where(last, 0.0, n_ref[...])
    ext = jnp.concatenate([prev, x_ref[...], nxt], axis=0)
    left = (taps - 1) // 2
    n_ext = lb + 16
    acc = jnp.zeros(x_ref.shape, F32) + b_ref[...]
    for kk in range(taps):
        sh = (left - kk) % n_ext
        shifted = ext if sh == 0 else pltpu.roll(ext, sh, 0)
        acc = acc + w_ref[kk:kk + 1, :] * shifted[8:8 + lb]
    return acc


def _halo_specs(t, lb, cb, tc):
    r8 = lb // 8
    nblk8 = t // 8
    return [pl.BlockSpec((lb, tc), lambda i: (i, cb)),
            pl.BlockSpec((8, tc), lambda i: (jnp.maximum(i * r8 - 1, 0), cb)),
            pl.BlockSpec((8, tc), lambda i: (jnp.minimum((i + 1) * r8, nblk8 - 1), cb))]


def _rg_gates_kernel(x_ref, p_ref, n_ref, cw_ref, cb_ref, wa_ref, wx_ref, ba_ref, bx_ref, lam_ref,
                     a_ref, b_ref, *, cfg):
    xc = _halo_conv(x_ref, p_ref, n_ref, cw_ref, cb_ref, cfg)
    xb = xc.astype(BF16)
    nlam = -lam_ref[...]
    sp = jnp.maximum(nlam, 0.0) + jnp.log1p(jnp.exp(-jnp.abs(nlam)))
    for d in range(2):
        ra = jnp.concatenate(
            [_dot(xb[:, h * RG_BW:(h + 1) * RG_BW], wa_ref[d, h]) for h in range(RG_HEADS)], axis=1)
        rx = jnp.concatenate(
            [_dot(xb[:, h * RG_BW:(h + 1) * RG_BW], wx_ref[d, h]) for h in range(RG_HEADS)], axis=1)
        r = jax.nn.sigmoid(ra + ba_ref[d:d + 1, :])
        ig = jax.nn.sigmoid(rx + bx_ref[d:d + 1, :])
        log_a = (-RG_C) * sp[d:d + 1, :] * r
        a = jnp.exp(log_a)
        gain = jnp.sqrt(-jnp.tanh(log_a) * (1.0 + a * a))
        a_ref[d] = a
        b_ref[d] = gain * ig * xc


def rg_gates(gx, conv_w, conv_b, w_a, w_x, b_a, b_x, lam, cfg):
    t = gx.shape[0]
    lb = SEQ_TILE
    taps = conv_w.shape[0]
    wspec = pl.BlockSpec((2, RG_HEADS, RG_BW, RG_BW), lambda i: (0, 0, 0, 0))
    vspec = pl.BlockSpec((2, D_MODEL), lambda i: (0, 0))
    ospec = pl.BlockSpec((2, lb, D_MODEL), lambda i: (0, i, 0))
    oshape = jax.ShapeDtypeStruct((2, t, D_MODEL), F32)
    return pl.pallas_call(
        functools.partial(_rg_gates_kernel, cfg=cfg),
        grid=(t // lb,),
        in_specs=_halo_specs(t, lb, 1, D_MODEL) + [
            pl.BlockSpec((taps, D_MODEL), lambda i: (0, 0)), pl.BlockSpec((1, D_MODEL), lambda i: (0, 0)),
            wspec, wspec, vspec, vspec, vspec],
        out_specs=[ospec, ospec],
        out_shape=[oshape, oshape],
        compiler_params=_cparams(("parallel",)),
        name="rg_gates",
    )(gx, gx, gx, conv_w, conv_b.reshape(1, D_MODEL), w_a, w_x, b_a, b_x, lam)


def _scan_groups(a, b, rev):
    rows = a.shape[0]
    r = lax.broadcasted_iota(jnp.int32, (rows, 1), 0) % SUBLANES
    s = 1
    while s < SUBLANES:
        shift = rows - s if rev else s
        keep = (r < SUBLANES - s) if rev else (r >= s)
        a_s = jnp.where(keep, pltpu.roll(a, shift, 0), 1.0)
        b_s = jnp.where(keep, pltpu.roll(b, shift, 0), 0.0)
        b = b + a * b_s
        a = a * a_s
        s *= 2
    return a, b


def _rg_scan_kernel(fblk, bblk, first, last, seqo, s0i, has0,
                    af_ref, bf_ref, ab_ref, bb_ref, h0_ref, hf_ref, hb_ref, hc_scr):
    i = pl.program_id(0)
    rows = af_ref.shape[0]
    ngroups = rows // SUBLANES

    @pl.when(first[i] == 1)
    def _():
        hc_scr[...] = jnp.where(has0[i] == 1, h0_ref[...], 0.0)

    for d, (a_ref, b_ref, o_ref) in enumerate(((af_ref, bf_ref, hf_ref), (ab_ref, bb_ref, hb_ref))):
        rev = d == 1
        decay, local = _scan_groups(a_ref[...], b_ref[...], rev)
        h = hc_scr[d:d + 1, :]
        for g in (range(ngroups - 1, -1, -1) if rev else range(ngroups)):
            sl = slice(g * SUBLANES, (g + 1) * SUBLANES)
            hg = decay[sl] * h + local[sl]
            o_ref[sl, :] = hg
            h = hg[0:1, :] if rev else hg[SUBLANES - 1:SUBLANES, :]
        hc_scr[d:d + 1, :] = h


def rg_scan(a, bx, h0, cfg):
    t = a.shape[1]
    rb = SEQ_TILE
    tabs = _seq_block_tables(cfg)
    nslots = tabs[0].shape[0]
    fwd = lambda arr: pl.BlockSpec((None, rb, D_MODEL), lambda i, fb, bb, *_: (0, fb[i], 0))
    bwd = lambda arr: pl.BlockSpec((None, rb, D_MODEL), lambda i, fb, bb, *_: (1, bb[i], 0))
    grid_spec = pltpu.PrefetchScalarGridSpec(
        num_scalar_prefetch=7,
        grid=(nslots,),
        in_specs=[
            fwd(a), fwd(bx), bwd(a), bwd(bx),
            pl.BlockSpec((None, 2, D_MODEL), lambda i, fb, bb, fi, la, so, s0i, *_: (s0i[i], 0, 0)),
        ],
        out_specs=[
            pl.BlockSpec((rb, D_MODEL), lambda i, fb, bb, *_: (fb[i], 0)),
            pl.BlockSpec((rb, D_MODEL), lambda i, fb, bb, *_: (bb[i], 0)),
        ],
        scratch_shapes=[pltpu.VMEM((2, D_MODEL), F32)],
    )
    oshape = jax.ShapeDtypeStruct((t, D_MODEL), F32)
    return pl.pallas_call(
        _rg_scan_kernel,
        grid_spec=grid_spec,
        out_shape=[oshape, oshape],
        compiler_params=_cparams(("arbitrary",)),
        name="rg_scan",
    )(*tabs, a, bx, a, bx, h0)


def _gelu_tanh(x):
    return 0.5 * x * (1.0 + jnp.tanh(math.sqrt(2.0 / math.pi) * (x + 0.044715 * (x * x * x))))


def _rg_out_kernel(gate_ref, hf_ref, hb_ref, w_ref, x_ref, m_ref, o_ref):
    y = (_gelu_tanh(gate_ref[...]) * (hf_ref[...] + hb_ref[...])).astype(BF16)
    o_ref[...] = x_ref[...] + m_ref[2:3, :] * _dot(y, w_ref[...])


def _tok_spec(tm, cb=0):
    return pl.BlockSpec((tm, D_MODEL), lambda i: (i, cb))


def _mtab_spec(layer, cfg, tm):
    return pl.BlockSpec((None, None, 6, D_MODEL), lambda i: (layer, _mod_idx(i, cfg, tm), 0, 0))


def _w_spec(k=D_MODEL):
    return pl.BlockSpec((k, D_MODEL), lambda i: (0, 0))


def rg_out(gx, hf, hb, w_out, x, mtab, layer, cfg):
    t = x.shape[0]
    tm = ROW_TILE
    return pl.pallas_call(
        _rg_out_kernel,
        grid=(t // tm,),
        in_specs=[_tok_spec(tm, 0), _tok_spec(tm), _tok_spec(tm), _w_spec(), _tok_spec(tm),
                  _mtab_spec(layer, cfg, tm)],
        out_specs=_tok_spec(tm),
        out_shape=jax.ShapeDtypeStruct((t, D_MODEL), F32),
        compiler_params=_cparams(("parallel",)),
        name="rg_out",
    )(gx, hf, hb, w_out, x, mtab)


def rglru_layer(x, mtab, layer, j, norm_g, p, state_rglru, cfg):
    gx = mod_matmul(x, mtab, layer, 0, norm_g, p['rg_w_in'][j].astype(BF16), None, cfg)
    a, bx = rg_gates(gx, p['rg_conv_w'][j], p['rg_conv_b'][j],
                     p['rg_w_a'][j].astype(BF16), p['rg_w_x'][j].astype(BF16),
                     p['rg_b_a'][j], p['rg_b_x'][j], p['rg_lambda'][j], cfg)
    hf, hb = rg_scan(a, bx, state_rglru[:, j].astype(F32), cfg)
    new_state = jnp.stack([hf[:cfg.tp].reshape(cfg.bp, cfg.lp, D_MODEL)[:, -1],
                           hb[:cfg.tp].reshape(cfg.bp, cfg.lp, D_MODEL)[:, 0]], axis=1)
    x = rg_out(gx, hf, hb, p['rg_w_out'][j].astype(BF16), x, mtab, layer, cfg)
    return x, new_state


def _chunk_cumsum(x, rev):
    rows = x.shape[0]
    r = lax.broadcasted_iota(jnp.int32, x.shape, 0) % HG_CHUNK
    s = 1
    while s < HG_CHUNK:
        if rev:
            x = x + jnp.where(r < HG_CHUNK - s, pltpu.roll(x, rows - s, 0), 0.0)
        else:
            x = x + jnp.where(r >= s, pltpu.roll(x, s, 0), 0.0)
        s *= 2
    return x


def _gla_kernel(fblk, bblk, first, last, seqo, s0i, has0,
                qf_ref, ff_ref, vf_ref, qb_ref, fb_ref, vb_ref, lb_ref, s0_ref,
                of_ref, ob_ref, sfin_ref, s_scr, *, layer):
    i = pl.program_id(1)
    hpb, rows = qf_ref.shape[0], qf_ref.shape[1]
    nch = rows // HG_CHUNK
    units = [(hh, d) for hh in range(hpb) for d in range(2)]

    @pl.when(first[i] == 1)
    def _():
        for hh, d in units:
            s_scr[hh, d] = jnp.where(has0[i] == 1, s0_ref[d, hh].T, 0.0)

    ri = lax.broadcasted_iota(jnp.int32, (rows, rows), 0)
    ci = lax.broadcasted_iota(jnp.int32, (rows, rows), 1)
    same = (ri // HG_CHUNK) == (ci // HG_CHUNK)
    chunks = [slice(n * HG_CHUNK, (n + 1) * HG_CHUNK) for n in range(nch)]
    in_refs = ((qf_ref, ff_ref, vf_ref), (qb_ref, fb_ref, vb_ref))
    out_refs = (of_ref, ob_ref)

    pre = {}
    for hh in range(hpb):
        lbx = lb_ref[hh]
        e = jnp.exp(lbx - jnp.max(lbx, axis=0, keepdims=True))
        sm = e / jnp.sum(e, axis=0, keepdims=True)
        lb = jnp.zeros(lbx.shape[1:], F32)
        for l in range(1, layer + 1):
            lb = lb + sm[l]
        for d in range(2):
            q_ref, f_ref, v_ref = in_refs[d]
            rev = d == 1
            v16 = v_ref[hh].astype(BF16)
            lbd = lb[d:d + 1, :]
            f = lbd + (1.0 - lbd) * jax.nn.sigmoid(f_ref[hh])
            k = 1.0 - f
            b = _chunk_cumsum(jnp.log(f), rev)
            b3 = b.reshape(nch, HG_CHUNK, HG_DK)
            bl = b3[:, 0:1, :] if rev else b3[:, HG_CHUNK - 1:HG_CHUNK, :]
            qi16 = (q_ref[hh] * jnp.exp(b)).astype(BF16)
            ki16 = (k * jnp.exp(-b)).astype(BF16)
            ks16 = (k.reshape(nch, HG_CHUNK, HG_DK) * jnp.exp(bl - b3)).reshape(rows, HG_DK).astype(BF16)
            g = jnp.exp(bl)
            mask = same & ((ci >= ri) if rev else (ci <= ri))
            att = jnp.where(mask, _dot_nt(qi16, ki16), 0.0).astype(BF16)
            ds = [_dot_tn(v16[sl], ks16[sl]) for sl in chunks]
            pre[hh, d] = (qi16, v16, att, ds, g)

    prev = {}
    for hh, d in units:
        _, _, _, ds, g = pre[hh, d]
        st = s_scr[hh, d]
        sp = [None] * nch
        for n in (range(nch - 1, -1, -1) if d == 1 else range(nch)):
            sp[n] = st.astype(BF16)
            st = st * g[n] + ds[n]
        s_scr[hh, d] = st
        prev[hh, d] = sp

    for hh, d in units:
        qi16, v16, att, _, _ = pre[hh, d]
        inter = [_dot_nt(qi16[sl], prev[hh, d][n]) for n, sl in enumerate(chunks)]
        out_refs[d][hh] = _dot(att, v16) + jnp.concatenate(inter, axis=0)

    @pl.when((last[i] == 1) & (has0[i] == 0))
    def _():
        for hh, d in units:
            sfin_ref[d, hh] = s_scr[hh, d].T


def _seq_block_tables(cfg):
    rb = SEQ_TILE
    pp, ps = cfg.lp // rb, cfg.ls // rb
    fblk, bblk, first, last, seqo, s0i, has0 = [], [], [], [], [], [], []
    for s in range(cfg.bp):
        for c in range(pp):
            fblk.append(s * pp + c); bblk.append(s * pp + pp - 1 - c)
            first.append(int(c == 0)); last.append(int(c == pp - 1))
            seqo.append(s); s0i.append(0); has0.append(0)
    base = cfg.tp // rb
    for s in range(cfg.bs):
        for c in range(ps):
            fblk.append(base + s * ps + c); bblk.append(base + s * ps + ps - 1 - c)
            first.append(int(c == 0)); last.append(int(c == ps - 1))
            seqo.append(cfg.bp - 1); s0i.append(s); has0.append(1)
    return [jnp.asarray(np.asarray(v, np.int32)) for v in (fblk, bblk, first, last, seqo, s0i, has0)]


def gla(proj, hgrn_lb, s0, layer, cfg):
    t = proj.shape[1]
    rb = SEQ_TILE
    tabs = _seq_block_tables(cfg)
    nslots = tabs[0].shape[0]
    h8 = HG_HEADS
    hpb = GLA_HEADS_PER_STEP
    ng = h8 // hpb

    def pspec(sec, which):
        return pl.BlockSpec((hpb, rb, HG_DK),
                            lambda h, i, fb, bb, *_: (sec * ng + h, (fb if which == 0 else bb)[i], 0))

    ospec_f = pl.BlockSpec((hpb, rb, HG_DK), lambda h, i, fb, bb, *_: (h, fb[i], 0))
    ospec_b = pl.BlockSpec((hpb, rb, HG_DK), lambda h, i, fb, bb, *_: (h, bb[i], 0))
    grid_spec = pltpu.PrefetchScalarGridSpec(
        num_scalar_prefetch=7,
        grid=(ng, nslots),
        in_specs=[
            pspec(0, 0), pspec(1, 0), pspec(3, 0),
            pspec(0, 1), pspec(2, 1), pspec(3, 1),
            pl.BlockSpec((hpb, DEPTH, 2, HG_DK), lambda h, i, *_: (h, 0, 0, 0)),
            pl.BlockSpec((None, 2, hpb, HG_DK, HG_DK),
                         lambda h, i, fb, bb, fi, la, so, s0i, *_: (s0i[i], 0, h, 0, 0)),
        ],
        out_specs=[
            ospec_f, ospec_b,
            pl.BlockSpec((None, 2, hpb, HG_DK, HG_DK),
                         lambda h, i, fb, bb, fi, la, so, *_: (so[i], 0, h, 0, 0)),
        ],
        scratch_shapes=[pltpu.VMEM((hpb, 2, HG_DK, HG_DK), F32)],
    )
    oshape = jax.ShapeDtypeStruct((h8, t, HG_DK), F32)
    of, ob, sfin = pl.pallas_call(
        functools.partial(_gla_kernel, layer=layer),
        grid_spec=grid_spec,
        out_shape=[oshape, oshape,
                   jax.ShapeDtypeStruct((cfg.bp, 2, h8, HG_DK, HG_DK), F32)],
        compiler_params=_cparams(("parallel", "arbitrary")),
        name="gla",
    )(*tabs, proj, proj, proj, proj, proj, proj,
      hgrn_lb.reshape(DEPTH, 2, h8, HG_DK).transpose(2, 0, 1, 3), s0)
    return of, ob, sfin


def _head_rms(o, g):
    ms = jnp.mean(o * o, axis=-1, keepdims=True)
    return o * lax.rsqrt(ms + EPS) * g


def _hg_out_kernel(of_ref, ob_ref, gh_ref, ng_ref, w_ref, x_ref, m_ref, o_ref):
    parts = []
    for h in range(HG_HEADS):
        o = _head_rms(of_ref[h] + ob_ref[h], ng_ref[...]) * _silu(gh_ref[h])
        parts.append(o.astype(BF16))
    y = jnp.concatenate(parts, axis=1)
    o_ref[...] = x_ref[...] + m_ref[2:3, :] * _dot(y, w_ref[...])


def hg_out(of, ob, proj, norm_g, w_out, x, mtab, layer, cfg):
    t = x.shape[0]
    tm = ROW_TILE
    hspec = pl.BlockSpec((HG_HEADS, tm, HG_DK), lambda i: (0, i, 0))
    return pl.pallas_call(
        _hg_out_kernel,
        grid=(t // tm,),
        in_specs=[hspec, hspec,
                  pl.BlockSpec((HG_HEADS, tm, HG_DK), lambda i: (4, i, 0)),
                  pl.BlockSpec((1, HG_DK), lambda i: (0, 0)),
                  _w_spec(), _tok_spec(tm), _mtab_spec(layer, cfg, tm)],
        out_specs=_tok_spec(tm),
        out_shape=jax.ShapeDtypeStruct((t, D_MODEL), F32),
        compiler_params=_cparams(("parallel",)),
        name="hg_out",
    )(of, ob, proj, norm_g.reshape(1, HG_DK), w_out, x, mtab)


def hgrn2_layer(x, mtab, layer, j, norm_g, p, state_hgrn, cfg):
    proj = mod_matmul(x, mtab, layer, 0, norm_g, p['hg_w_in'][j].astype(BF16), None, cfg, heads_out=True)
    of, ob, sfin = gla(proj, p['hgrn_lb'], state_hgrn[:, j].astype(F32), layer, cfg)
    x = hg_out(of, ob, proj, p['hg_norm_g'][j], p['hg_w_out'][j].astype(BF16), x, mtab, layer, cfg)
    return x, sfin


def _dft_tables(length):
    a, b = _dft_tables_np(length)
    return jnp.asarray(a), jnp.asarray(b), jnp.asarray(np.ascontiguousarray(b.T))


@functools.lru_cache(maxsize=None)
def _dft_tables_np(length):
    n = 2 * length
    idx = np.arange(length, dtype=np.int64)
    ang = ((idx[:, None] * idx[None, :]) % n).astype(np.float64) * (2.0 * math.pi / n)
    a = np.cos(ang)
    b = -np.sin(ang)
    b[0, :] = np.where(idx % 2 == 0, 1.0, -1.0)
    return a.astype(BF16), b.astype(BF16)


@functools.lru_cache(maxsize=None)
def _hy_features_np(length):
    t_idx = np.arange(length, dtype=np.float64)
    bands = (HY_EMB - 1) // 2
    fr = np.linspace(1e-4, bands - 1, bands)
    ang = (2.0 * math.pi * t_idx / length)[:, None] * fr[None, :]
    z = np.zeros((length, LANES), np.float32)
    z[:, 0] = t_idx / (length - 1)
    z[:, 1:1 + bands] = np.cos(ang)
    z[:, 1 + bands:HY_EMB] = -np.sin(ang)
    return z


def _hy_features(length):
    return jnp.asarray(_hy_features_np(length))


def _hy_filter_kernel(z_ref, w1_ref, b1_ref, w2_ref, b2_ref, w3f_ref, w3b_ref, fq_ref, dcf_ref, dcb_ref,
                      a_ref, b_ref, ka_ref, ki_ref, kn_ref):
    length = z_ref.shape[0]
    fq = fq_ref[...]
    h = jnp.sin(fq * (_dot_hi(z_ref[...], w1_ref[...]) + b1_ref[...]))
    h = jnp.sin(fq * (_dot_hi(h, w2_ref[...]) + b2_ref[...]))
    row = lax.broadcasted_iota(jnp.int32, (length, 1), 0)
    tt = row.astype(F32) / float(length - 1)
    hf = _dot_hi(h, w3f_ref[...]) * jnp.exp(-tt * jnp.abs(dcf_ref[...]))
    hb = _dot_hi(h, w3b_ref[...]) * jnp.exp(-tt * jnp.abs(dcb_ref[...]))
    hb = jnp.where(row == 0, 0.0, hb)
    nrm = lax.rsqrt(jnp.sum(hf * hf + hb * hb, axis=0, keepdims=True) + EPS)
    hf = hf * nrm
    hb = hb * nrm
    a = a_ref[...]
    bm = b_ref[...]
    ka_ref[...] = _dot(a, (hf + hb).astype(BF16))
    kbf = _dot(bm, hf.astype(BF16))
    kbb = _dot(bm, hb.astype(BF16))
    ki_ref[...] = jnp.where(row == 0, 0.0, kbf - kbb)
    kn_ref[...] = jnp.broadcast_to(kbf[0:1, :] + kbb[0:1, :], kn_ref.shape)


def _const_spec(shape):
    nd = len(shape)
    return pl.BlockSpec(shape, lambda *_: (0,) * nd, pipeline_mode=pl.Buffered(1))


def hy_filter_spectrum(length, a_tab, b_tab, p, j, tc=256):
    d = D_MODEL
    nct = d // tc
    w1 = jnp.zeros((LANES, HY_FW), F32).at[:HY_EMB].set(p['hy_f_w1'][j])
    w3 = p['hy_f_w3'][j]
    dec = p['hy_decay'][j].reshape(1, 4 * d)
    z = _hy_features(length)
    fwd_col = lambda o, c: (0, (o * 2) * nct + c)
    bwd_col = lambda o, c: (0, (o * 2 + 1) * nct + c)
    return pl.pallas_call(
        _hy_filter_kernel,
        grid=(2, nct),
        in_specs=[
            _const_spec((length, LANES)), _const_spec((LANES, HY_FW)), _const_spec((1, HY_FW)),
            _const_spec((HY_FW, HY_FW)), _const_spec((1, HY_FW)),
            pl.BlockSpec((HY_FW, tc), fwd_col), pl.BlockSpec((HY_FW, tc), bwd_col),
            _const_spec((1, HY_FW)),
            pl.BlockSpec((1, tc), fwd_col), pl.BlockSpec((1, tc), bwd_col),
            _const_spec((length, length)), _const_spec((length, length)),
        ],
        out_specs=[
            pl.BlockSpec((None, length, tc), lambda o, c: (o, 0, c)),
            pl.BlockSpec((None, length, tc), lambda o, c: (o, 0, c)),
            pl.BlockSpec((None, 8, tc), lambda o, c: (o, 0, c)),
        ],
        out_shape=[jax.ShapeDtypeStruct((2, length, d), F32),
                   jax.ShapeDtypeStruct((2, length, d), F32),
                   jax.ShapeDtypeStruct((2, 8, d), F32)],
        compiler_params=_cparams(("parallel", "parallel")),
        name="hy_filter",
    )(z, w1, p['hy_f_b1'][j].reshape(1, HY_FW), p['hy_f_w2'][j], p['hy_f_b2'][j].reshape(1, HY_FW),
      w3, w3, p['hy_freq'][j].reshape(1, HY_FW), dec, dec, a_tab, b_tab)


def _seq_conv(x, w, b):
    length = x.shape[0]
    taps = w.shape[0]
    left = (taps - 1) // 2
    row = lax.broadcasted_iota(jnp.int32, (length, 1), 0)
    acc = b + w[left:left + 1, :] * x
    for kk in range(taps):
        off = kk - left
        if off != 0:
            shifted = pltpu.roll(x, (-off) % length, 0)
            valid = (row + off >= 0) & (row + off < length)
            acc = acc + w[kk:kk + 1, :] * jnp.where(valid, shifted, 0.0)
    return acc


def _sconv_kernel(*refs, fb, conv_z, aliased):
    (zin_ref, xin_ref, a_ref, b_ref, bt_ref, ka_ref, ki_ref, kn_ref, bias_ref,
     cwz_ref, cbz_ref, cwx_ref, cbx_ref) = refs[:13]
    o_ref, yr_scr, yi_scr, z_ref, x_ref = refs[14:] if aliased else refs[13:]
    length = zin_ref.shape[0]
    z_ref[...] = _seq_conv(zin_ref[...], cwz_ref[...], cbz_ref[...]) if conv_z else zin_ref[...]
    x_ref[...] = _seq_conv(xin_ref[...], cwx_ref[...], cbx_ref[...])
    zb = z_ref[...].astype(BF16)
    inv = 1.0 / (2 * length)
    for kf in range(length // fb):
        rows = slice(kf * fb, (kf + 1) * fb)
        pr = _dot(a_ref[rows, :], zb)
        qi = _dot(b_ref[rows, :], zb)
        ka = ka_ref[rows, :]
        ki = ki_ref[rows, :]
        if kf == 0:
            r0 = lax.broadcasted_iota(jnp.int32, (fb, 1), 0) == 0
            kd = jnp.where(r0, kn_ref[0:1, :], ka)
            wgt = jnp.where(r0, inv, 2.0 * inv)
        else:
            kd = ka
            wgt = 2.0 * inv
        yr_scr[rows, :] = ((pr * ka - qi * ki) * wgt).astype(BF16)
        yi_scr[rows, :] = ((pr * ki + qi * kd) * wgt).astype(BF16)
    for tb in range(length // fb):
        rows = slice(tb * fb, (tb + 1) * fb)
        y = _dot(a_ref[rows, :], yr_scr[...]) + _dot(bt_ref[rows, :], yi_scr[...])
        o_ref[rows, :] = x_ref[rows, :] * (y + bias_ref[...] * z_ref[rows, :])


def sconv_group(zarr, zcol, conv_z, xarr, xcol, conv_w, conv_b, seq0, nseq, length, tabs, ka, ki, kn,
                order, bias, out_prev, tc):
    a_tab, b_tab, bt_tab = tabs
    d = D_MODEL
    t = zarr.shape[0]
    fb = min(256, length)
    zc, xc = zcol // tc, xcol // tc
    taps = conv_w.shape[0]
    cb = conv_b.reshape(1, -1)
    in_specs = [
        pl.BlockSpec((length, tc), lambda s, c: (seq0 + s, zc + c)),
        pl.BlockSpec((length, tc), lambda s, c: (seq0 + s, xc + c)),
        _const_spec((length, length)), _const_spec((length, length)), _const_spec((length, length)),
        pl.BlockSpec((None, length, tc), lambda s, c: (order, 0, c)),
        pl.BlockSpec((None, length, tc), lambda s, c: (order, 0, c)),
        pl.BlockSpec((None, 8, tc), lambda s, c: (order, 0, c)),
        pl.BlockSpec((None, 1, tc), lambda s, c: (order, 0, c)),
        pl.BlockSpec((taps, tc), lambda s, c: (0, zc + c if conv_z else c)),
        pl.BlockSpec((1, tc), lambda s, c: (0, zc + c if conv_z else c)),
        pl.BlockSpec((taps, tc), lambda s, c: (0, xc + c)),
        pl.BlockSpec((1, tc), lambda s, c: (0, xc + c)),
    ]
    args = [zarr, xarr, a_tab, b_tab, bt_tab, ka, ki, kn, bias.reshape(2, 1, d), conv_w, cb, conv_w, cb]
    aliases = {}
    if out_prev is not None:
        in_specs.append(pl.BlockSpec(memory_space=pl.ANY))
        args.append(out_prev)
        aliases = {len(args) - 1: 0}
    return pl.pallas_call(
        functools.partial(_sconv_kernel, fb=fb, conv_z=conv_z, aliased=out_prev is not None),
        grid=(nseq, d // tc),
        in_specs=in_specs,
        out_specs=pl.BlockSpec((length, tc), lambda s, c: (seq0 + s, c)),
        out_shape=jax.ShapeDtypeStruct((t, d), F32),
        scratch_shapes=[pltpu.VMEM((length, tc), BF16), pltpu.VMEM((length, tc), BF16),
                        pltpu.VMEM((length, tc), F32), pltpu.VMEM((length, tc), F32)],
        input_output_aliases=aliases,
        compiler_params=_cparams(("parallel", "parallel")),
        name="sconv",
    )(*args)


def _lin_out_kernel(a_ref, w_ref, b_ref, x_ref, m_ref, o_ref):
    y = _dot(a_ref[...].astype(BF16), w_ref[...]) + b_ref[...]
    o_ref[...] = x_ref[...] + m_ref[2:3, :] * y


def lin_out(a, w, b, x, mtab, layer, cfg):
    t = x.shape[0]
    tm = ROW_TILE
    return pl.pallas_call(
        _lin_out_kernel,
        grid=(t // tm,),
        in_specs=[_tok_spec(tm), _w_spec(), pl.BlockSpec((1, D_MODEL), lambda i: (0, 0)),
                  _tok_spec(tm), _mtab_spec(layer, cfg, tm)],
        out_specs=_tok_spec(tm),
        out_shape=jax.ShapeDtypeStruct((t, D_MODEL), F32),
        compiler_params=_cparams(("parallel",)),
        name="lin_out",
    )(a, w, b.reshape(1, D_MODEL), x, mtab)


def hyena_layer(x, mtab, layer, j, norm_g, p, cfg):
    d = D_MODEL
    pre = mod_matmul(x, mtab, layer, 0, norm_g, p['hy_w_in'][j].astype(BF16), p['hy_b_in'][j], cfg)
    cw, cb, bias = p['hy_conv_w'][j], p['hy_conv_b'][j], p['hy_bias'][j]
    groups = []
    for (row0, nseq, length) in ((0, cfg.bp, cfg.lp), (cfg.tp, cfg.bs, cfg.ls)):
        tabs = _dft_tables(length)
        spec = hy_filter_spectrum(length, tabs[0], tabs[1], p, j)
        tc = 256 if length > 512 else d
        groups.append((row0 // length, nseq, length, tabs, spec, tc))
    z1 = None
    for seq0, nseq, length, tabs, (ka, ki, kn), tc in groups:
        z1 = sconv_group(pre, 0, True, pre, d, cw, cb, seq0, nseq, length, tabs, ka, ki, kn, 0, bias, z1, tc)
    z2 = None
    for seq0, nseq, length, tabs, (ka, ki, kn), tc in groups:
        z2 = sconv_group(z1, 0, False, pre, 2 * d, cw, cb, seq0, nseq, length, tabs, ka, ki, kn, 1, bias,
                         z2, tc)
    return lin_out(z2, p['hy_w_out'][j].astype(BF16), p['hy_b_out'][j], x, mtab, layer, cfg)


def _group_rms_scale(x, g1, g1t):
    x2 = x * x
    hi = x2.astype(BF16)
    lo = (x2 - hi.astype(F32)).astype(BF16)
    s = _dot(hi, g1) + _dot(lo, g1)
    r = lax.rsqrt(s * (1.0 / DA_DH) + EPS)
    rhi = r.astype(BF16)
    rlo = (r - rhi.astype(F32)).astype(BF16)
    return _dot(rhi, g1t) + _dot(rlo, g1t)


def _rope(x, cos, sin_signed):
    lane = lax.broadcasted_iota(jnp.int32, x.shape, 1)
    w = x.shape[1]
    nf = DA_DH // 4
    rot = jnp.where(lane % (2 * nf) < nf, pltpu.roll(x, w - nf, 1), pltpu.roll(x, nf, 1))
    return x * cos + rot * sin_signed


def _qk_prep_kernel(q_ref, k_ref, v_ref, qg_ref, kg_ref, g1_ref, g1t_ref, cos_ref, sin_ref,
                    qh_ref, kh_ref, vh_ref, kc_ref, vc_ref, *, cfg):
    tm = q_ref.shape[0]
    i = pl.program_id(0)
    is_prompt = i < cfg.tp // tm
    g1 = g1_ref[...]
    g1t = g1t_ref[...]
    q = q_ref[...]
    k = k_ref[...]
    qn = q * _group_rms_scale(q, g1, g1t) * qg_ref[...]
    kn = k * _group_rms_scale(k, g1, g1t) * kg_ref[...]
    vo = v_ref[...].astype(BF16)
    for h in range(DA_HEADS):
        vh_ref[h] = vo[:, h * LANES:(h + 1) * LANES]

    def emit(qv, kv):
        qo = (qv * (DA_DH ** -0.5)).astype(BF16)
        ko = kv.astype(BF16)
        for h in range(DA_HEADS):
            sl = slice(h * LANES, (h + 1) * LANES)
            qh_ref[h] = qo[:, sl]
            kh_ref[h] = ko[:, sl]

    @pl.when(is_prompt)
    def _():
        kc_ref[...] = kn
        vc_ref[...] = v_ref[...]
        emit(qn, kn)

    @pl.when(jnp.logical_not(is_prompt))
    def _():
        reps = q.shape[1] // LANES
        cos = jnp.tile(cos_ref[...], (1, reps))
        sin = jnp.tile(sin_ref[...], (1, reps))
        emit(_rope(qn, cos, sin), _rope(kn, cos, sin))


@functools.lru_cache(maxsize=None)
def _rope_tables_np(length):
    nf = DA_DH // 4
    t = np.arange(length)
    pos = np.stack([t // GRID_W, t % GRID_W], axis=-1).astype(np.float64)
    inv = ROPE_BASE ** (-np.arange(nf, dtype=np.float64) / nf)
    ang = pos[:, :, None] * inv
    cos = np.broadcast_to(np.cos(ang)[:, :, None, :], (length, 2, 2, nf)).reshape(length, DA_DH)
    sin = np.sin(ang)[:, :, None, :]
    sin = np.concatenate([-sin, sin], axis=2).reshape(length, DA_DH)
    reps = LANES // DA_DH
    return (np.tile(cos, (1, reps)).astype(np.float32), np.tile(sin, (1, reps)).astype(np.float32))


def _rope_tables(length):
    cos, sin = _rope_tables_np(length)
    return jnp.asarray(cos), jnp.asarray(sin)


def qk_prep(qkv, q_g, k_g, cfg):
    t = qkv.shape[0]
    tm = ROW_TILE
    d = D_MODEL
    ngrp = d // DA_DH
    grp = jnp.arange(d) // DA_DH
    g1 = (grp[:, None] == jnp.arange(LANES)[None, :]).astype(BF16)
    g1t = g1.T
    cos, sin = _rope_tables(cfg.ls)
    npt = cfg.tp // tm
    pps = cfg.ls // tm
    tab_spec = pl.BlockSpec((tm, LANES), lambda i: (jnp.maximum(i - npt, 0) % pps, 0))
    hspec = pl.BlockSpec((DA_HEADS, tm, LANES), lambda i: (0, i, 0))
    hshape = jax.ShapeDtypeStruct((DA_HEADS, t, LANES), BF16)
    return pl.pallas_call(
        functools.partial(_qk_prep_kernel, cfg=cfg),
        grid=(t // tm,),
        in_specs=[_tok_spec(tm, 0), _tok_spec(tm, 1), _tok_spec(tm, 2),
                  pl.BlockSpec((1, d), lambda i: (0, 0)), pl.BlockSpec((1, d), lambda i: (0, 0)),
                  pl.BlockSpec((d, LANES), lambda i: (0, 0)), pl.BlockSpec((LANES, d), lambda i: (0, 0)),
                  tab_spec, tab_spec],
        out_specs=[hspec, hspec, hspec,
                   pl.BlockSpec((tm, d), lambda i: (jnp.minimum(i, npt - 1), 0)),
                   pl.BlockSpec((tm, d), lambda i: (jnp.minimum(i, npt - 1), 0))],
        out_shape=[hshape, hshape, hshape, jax.ShapeDtypeStruct((cfg.tp, d), F32),
                   jax.ShapeDtypeStruct((cfg.tp, d), F32)],
        compiler_params=_cparams(("arbitrary",)),
        name="qk_prep",
    )(qkv, qkv, qkv, jnp.tile(q_g, ngrp).reshape(1, d), jnp.tile(k_g, ngrp).reshape(1, d),
      g1, g1t, cos, sin)


def _dattn_kernel(*refs, has_cache, aliased, lam_init):
    n_in = 4 + (2 if has_cache else 0)
    lp_ref, q_ref, k_ref, v_ref = refs[:4]
    o_ref = refs[n_in + (1 if aliased else 0)]
    lp = lp_ref[...]
    lam = (jnp.exp(jnp.sum(lp[0:1] * lp[1:2], axis=1, keepdims=True))
           - jnp.exp(jnp.sum(lp[2:3] * lp[3:4], axis=1, keepdims=True)) + lam_init)
    lane = lax.broadcasted_iota(jnp.int32, q_ref.shape[1:], 1)
    for hh in range(q_ref.shape[0]):
        q = q_ref[hh]
        zero = jnp.zeros_like(q)
        qs = (jnp.where(lane < DA_DH, q, zero), jnp.where(lane >= DA_DH, q, zero))
        k = k_ref[hh]
        v = v_ref[hh]
        if has_cache:
            ck = refs[4][:, hh * LANES:(hh + 1) * LANES].astype(BF16)
            cv = refs[5][:, hh * LANES:(hh + 1) * LANES].astype(BF16)
        halves = []
        for half in range(2):
            s = _dot_nt(qs[half], k)
            m = jnp.max(s, axis=-1, keepdims=True)
            if has_cache:
                c = _dot_nt(qs[half], ck)
                m = jnp.maximum(m, jnp.max(c, axis=-1, keepdims=True))
            e = jnp.exp(s - m)
            z = jnp.sum(e, axis=-1, keepdims=True)
            pv = _dot(e.astype(BF16), v)
            if has_cache:
                ec = jnp.exp(c - m)
                z = z + jnp.sum(ec, axis=-1, keepdims=True)
                pv = pv + _dot(ec.astype(BF16), cv)
            halves.append(pv / z)
        o_ref[hh] = halves[0] - lam * halves[1]


def dattn_group(qh, kh, vh, da_lambda, lam_init, seq0, nseq, length, cache_k=None, cache_v=None,
                prev=None, tq=256):
    t = qh.shape[1]
    tq = min(tq, length)
    nq = length // tq
    has_cache = cache_k is not None
    hpb = DA_HEADS_PER_STEP
    in_specs = [
        pl.BlockSpec((4, DA_DH), lambda b, h, qi: (0, 0)),
        pl.BlockSpec((hpb, tq, LANES), lambda b, h, qi: (h, (seq0 + b) * nq + qi, 0)),
        pl.BlockSpec((hpb, length, LANES), lambda b, h, qi: (h, seq0 + b, 0)),
        pl.BlockSpec((hpb, length, LANES), lambda b, h, qi: (h, seq0 + b, 0)),
    ]
    args = [da_lambda, qh, kh, vh]
    if has_cache:
        past = cache_k.shape[1]
        cspec = pl.BlockSpec((None, past, hpb * LANES), lambda b, h, qi: (b, 0, h))
        in_specs += [cspec, cspec]
        args += [cache_k.reshape(nseq, past, DA_HEADS * LANES), cache_v.reshape(nseq, past, DA_HEADS * LANES)]
    aliases = {}
    if prev is not None:
        in_specs.append(pl.BlockSpec(memory_space=pl.ANY))
        args.append(prev)
        aliases = {len(args) - 1: 0}
    return pl.pallas_call(
        functools.partial(_dattn_kernel, has_cache=has_cache, aliased=prev is not None, lam_init=lam_init),
        grid=(nseq, DA_HEADS // hpb, nq),
        in_specs=in_specs,
        out_specs=pl.BlockSpec((hpb, tq, LANES), lambda b, h, qi: (h, (seq0 + b) * nq + qi, 0)),
        out_shape=jax.ShapeDtypeStruct((DA_HEADS, t, LANES), F32),
        input_output_aliases=aliases,
        compiler_params=_cparams(("parallel", "parallel", "parallel")),
        name="dattn",
    )(*args)


def _da_out_kernel(o_ref, sg_ref, w_ref, x_ref, m_ref, out_ref, *, lam_init):
    parts = []
    for h in range(DA_HEADS):
        parts.append((_head_rms(o_ref[h], sg_ref[...]) * (1.0 - lam_init)).astype(BF16))
    y = jnp.concatenate(parts, axis=1)
    out_ref[...] = x_ref[...] + m_ref[2:3, :] * _dot(y, w_ref[...])


def da_out(o, sub_g, w_out, x, mtab, layer, lam_init, cfg):
    t = x.shape[0]
    tm = ROW_TILE
    return pl.pallas_call(
        functools.partial(_da_out_kernel, lam_init=lam_init),
        grid=(t // tm,),
        in_specs=[pl.BlockSpec((DA_HEADS, tm, LANES), lambda i: (0, i, 0)),
                  pl.BlockSpec((1, LANES), lambda i: (0, 0)),
                  _w_spec(), _tok_spec(tm), _mtab_spec(layer, cfg, tm)],
        out_specs=_tok_spec(tm),
        out_shape=jax.ShapeDtypeStruct((t, D_MODEL), F32),
        compiler_params=_cparams(("parallel",)),
        name="da_out",
    )(o, sub_g.reshape(1, LANES), w_out, x, mtab)


def diffattn_layer(x, mtab, layer, j, norm_g, p, cache_k, cache_v, cfg):
    d = D_MODEL
    lam_init = 0.8 - 0.6 * math.exp(-0.3 * layer)
    qkv = mod_matmul(x, mtab, layer, 0, norm_g, p['da_w_in'][j].astype(BF16), None, cfg)
    qh, kh, vh, kc, vc = qk_prep(qkv, p['da_q_norm'][j], p['da_k_norm'][j], cfg)
    lamp = p['da_lambda'][j].astype(F32)
    o = dattn_group(qh, kh, vh, lamp, lam_init, 0, cfg.bp, cfg.lp)
    o = dattn_group(qh, kh, vh, lamp, lam_init, cfg.tp // cfg.ls, cfg.bs, cfg.ls,
                    cache_k[:, j], cache_v[:, j], prev=o)
    x = da_out(o, p['da_sub_norm'][j], p['da_w_out'][j].astype(BF16), x, mtab, layer, lam_init, cfg)
    new_k = kc.reshape(cfg.bp, cfg.lp, DA_HEADS, 2 * DA_DH)
    new_v = vc.reshape(cfg.bp, cfg.lp, DA_HEADS, 2 * DA_DH)
    return x, new_k, new_v


def _ffn_kernel(x_ref, m_ref, g_ref, wg_ref, wh_ref, wo_ref, o_ref, u_scr, acc_scr):
    f = pl.program_id(1)

    @pl.when(f == 0)
    def _():
        u_scr[...] = _modulate(x_ref[...], g_ref[...], m_ref[...], 3).astype(BF16)
        acc_scr[...] = jnp.zeros_like(acc_scr)

    u = u_scr[...]
    a = (_silu(_dot(u, wg_ref[...])) * _dot(u, wh_ref[...])).astype(BF16)
    acc_scr[...] += _dot(a, wo_ref[...])

    @pl.when(f == pl.num_programs(1) - 1)
    def _():
        o_ref[...] = x_ref[...] + m_ref[5:6, :] * acc_scr[...]


def dense_ffn(x, mtab, layer, g, w_in, w_out, cfg, tf=1408):
    t = x.shape[0]
    tm = ROW_TILE
    nf = FF_DIM // tf
    return pl.pallas_call(
        _ffn_kernel,
        grid=(t // tm, nf),
        in_specs=[
            pl.BlockSpec((tm, D_MODEL), lambda i, f: (i, 0)),
            pl.BlockSpec((None, None, 6, D_MODEL), lambda i, f: (layer, _mod_idx(i, cfg, tm), 0, 0)),
            pl.BlockSpec((1, D_MODEL), lambda i, f: (0, 0)),
            pl.BlockSpec((D_MODEL, tf), lambda i, f: (0, f)),
            pl.BlockSpec((D_MODEL, tf), lambda i, f: (0, nf + f)),
            pl.BlockSpec((tf, D_MODEL), lambda i, f: (f, 0)),
        ],
        out_specs=pl.BlockSpec((tm, D_MODEL), lambda i, f: (i, 0)),
        out_shape=jax.ShapeDtypeStruct((t, D_MODEL), F32),
        scratch_shapes=[pltpu.VMEM((tm, D_MODEL), BF16), pltpu.VMEM((tm, D_MODEL), F32)],
        compiler_params=_cparams(("parallel", "arbitrary")),
        name="ffn",
    )(x, mtab, g.reshape(1, D_MODEL), w_in, w_in, w_out)


def _moe_pre_kernel(x_ref, m_ref, g_ref, rw_ref, rb_ref, u_ref, r_ref, c_ref, cnt_scr):
    tm = x_ref.shape[0]

    @pl.when(pl.program_id(0) == 0)
    def _():
        cnt_scr[...] = jnp.zeros_like(cnt_scr)

    u = _modulate(x_ref[...], g_ref[...], m_ref[...], 3)
    bits = pltpu.bitcast(u.astype(BF16).astype(F32), jnp.uint32)
    half = D_MODEL // 2
    u_ref[...] = pltpu.bitcast((bits[:, :half] >> 16) | (bits[:, half:] & jnp.uint32(0xFFFF0000)), F32)
    logits = _dot_hi(u, rw_ref[...]) + rb_ref[...]
    lane = lax.broadcasted_iota(jnp.int32, logits.shape, 1)
    neg = -jnp.inf
    lg = jnp.where(lane < MOE_E, logits, neg)
    m1 = jnp.max(lg, axis=-1, keepdims=True)
    i1 = jnp.min(jnp.where(lg == m1, lane, LANES), axis=-1, keepdims=True)
    lg2 = jnp.where(lane == i1, neg, lg)
    m2 = jnp.max(lg2, axis=-1, keepdims=True)
    i2 = jnp.min(jnp.where(lg2 == m2, lane, LANES), axis=-1, keepdims=True)
    e2 = jnp.exp(m2 - m1)
    w1 = 1.0 / (1.0 + e2)
    w2 = e2 / (1.0 + e2)
    hit = jnp.where((lane == i1) | (lane == i2), 1.0, 0.0)
    ri = lax.broadcasted_iota(jnp.int32, (tm, tm), 0)
    ci = lax.broadcasted_iota(jnp.int32, (tm, tm), 1)
    ahead = jnp.where(ci < ri, 1.0, 0.0).astype(BF16)
    pos = _dot(ahead, hit.astype(BF16)) + cnt_scr[...]
    r1 = jnp.sum(jnp.where(lane == i1, pos, 0.0), axis=-1, keepdims=True)
    r2 = jnp.sum(jnp.where(lane == i2, pos, 0.0), axis=-1, keepdims=True)
    cnt = cnt_scr[...] + jnp.sum(hit, axis=0, keepdims=True)
    cnt_scr[...] = cnt
    c_ref[...] = jnp.broadcast_to(cnt, c_ref.shape)
    vals = (i1.astype(F32), i2.astype(F32), w1, w2, r1, r2)
    out = jnp.zeros(logits.shape, F32)
    for col, val in enumerate(vals):
        out = jnp.where(lane == col, val, out)
    r_ref[...] = out


def moe_pre(x, mtab, layer, g, router_w, router_b, cfg):
    t = x.shape[0]
    tm = ROW_TILE
    rw = jnp.zeros((D_MODEL, LANES), F32).at[:, :MOE_E].set(router_w)
    rb = jnp.zeros((1, LANES), F32).at[0, :MOE_E].set(router_b)
    return pl.pallas_call(
        _moe_pre_kernel,
        grid=(t // tm,),
        in_specs=[_tok_spec(tm), _mtab_spec(layer, cfg, tm), pl.BlockSpec((1, D_MODEL), lambda i: (0, 0)),
                  pl.BlockSpec((D_MODEL, LANES), lambda i: (0, 0)), pl.BlockSpec((1, LANES), lambda i: (0, 0))],
        out_specs=[pl.BlockSpec((tm, D_MODEL // 2), lambda i: (i, 0)),
                   pl.BlockSpec((tm, LANES), lambda i: (i, 0)),
                   pl.BlockSpec((8, LANES), lambda i: (0, 0))],
        out_shape=[jax.ShapeDtypeStruct((t, D_MODEL // 2), F32),
                   jax.ShapeDtypeStruct((t, LANES), F32),
                   jax.ShapeDtypeStruct((8, LANES), F32)],
        scratch_shapes=[pltpu.VMEM((1, LANES), F32)],
        compiler_params=_cparams(("arbitrary",)),
        name="moe_pre",
    )(x, mtab, g.reshape(1, D_MODEL), rw, rb)


def _moe_ffn_kernel(te_ref, act_ref, x_ref, wg_ref, wh_ref, wo_ref, o_ref, xb_scr, acc_scr):
    i = pl.program_id(0)
    f = pl.program_id(1)
    active = act_ref[i] == 1

    @pl.when(f == 0)
    def _():
        acc_scr[...] = jnp.zeros_like(acc_scr)

    @pl.when(active & (f == 0))
    def _():
        w = pltpu.bitcast(x_ref[...], jnp.uint32)
        lo = pltpu.bitcast(w << 16, F32)
        hi = pltpu.bitcast(w & jnp.uint32(0xFFFF0000), F32)
        xb_scr[...] = jnp.concatenate([lo, hi], axis=1).astype(BF16)

    @pl.when(active)
    def _():
        u = xb_scr[...]
        tf = wg_ref.shape[1]
        sub = 256
        part = None
        for c in range(tf // sub):
            cs = slice(c * sub, (c + 1) * sub)
            a = (_silu(_dot(u, wg_ref[:, cs].astype(BF16)))
                 * _dot(u, wh_ref[:, cs].astype(BF16))).astype(BF16)
            y = _dot(a, wo_ref[cs, :].astype(BF16))
            part = y if part is None else part + y
        acc_scr[...] += part

    @pl.when(f == pl.num_programs(1) - 1)
    def _():
        o_ref[...] = acc_scr[...]


def moe_ffn(xs, tile_expert, tile_active, w_in, w_out, n, tf=512):
    npad = xs.shape[0]
    tm = MOE_TILE
    nf = MOE_FF // tf
    grid_spec = pltpu.PrefetchScalarGridSpec(
        num_scalar_prefetch=2,
        grid=(npad // tm, nf),
        in_specs=[
            pl.BlockSpec((tm, D_MODEL // 2), lambda i, f, te, ac: (i, 0)),
            pl.BlockSpec((None, None, D_MODEL, tf), lambda i, f, te, ac: (n, te[i], 0, f)),
            pl.BlockSpec((None, None, D_MODEL, tf), lambda i, f, te, ac: (n, te[i], 0, nf + f)),
            pl.BlockSpec((None, None, tf, D_MODEL), lambda i, f, te, ac: (n, te[i], f, 0)),
        ],
        out_specs=pl.BlockSpec((tm, D_MODEL), lambda i, f, te, ac: (i, 0)),
        scratch_shapes=[pltpu.VMEM((tm, D_MODEL), BF16), pltpu.VMEM((tm, D_MODEL), F32)],
    )
    return pl.pallas_call(
        _moe_ffn_kernel,
        grid_spec=grid_spec,
        out_shape=jax.ShapeDtypeStruct((npad, D_MODEL), F32),
        compiler_params=_cparams(("parallel", "arbitrary")),
        name="moe_ffn",
    )(tile_expert, tile_active, xs, w_in, w_in, w_out)


def _moe_combine_kernel(y1_ref, y2_ref, r_ref, x_ref, m_ref, *o_refs, cfg):
    r = r_ref[...]
    y = r[:, 2:3] * y1_ref[...] + r[:, 3:4] * y2_ref[...]
    out = x_ref[...] + m_ref[5:6, :] * y
    if len(o_refs) == 1:
        o_refs[0][...] = out
    else:
        is_prompt = pl.program_id(0) < cfg.tp // x_ref.shape[0]

        @pl.when(is_prompt)
        def _():
            o_refs[0][...] = out

        @pl.when(jnp.logical_not(is_prompt))
        def _():
            o_refs[1][...] = out


def moe_combine(y1, y2, route, x, mtab, layer, cfg, split):
    t = x.shape[0]
    tm = ROW_TILE
    npt = cfg.tp // tm
    if split:
        out_specs = [pl.BlockSpec((tm, D_MODEL), lambda i: (jnp.minimum(i, npt - 1), 0)),
                     pl.BlockSpec((tm, D_MODEL), lambda i: (jnp.maximum(i - npt, 0), 0))]
        out_shape = [jax.ShapeDtypeStruct((cfg.tp, D_MODEL), F32), jax.ShapeDtypeStruct((cfg.ts, D_MODEL), F32)]
    else:
        out_specs = _tok_spec(tm)
        out_shape = jax.ShapeDtypeStruct((t, D_MODEL), F32)
    return pl.pallas_call(
        functools.partial(_moe_combine_kernel, cfg=cfg),
        grid=(t // tm,),
        in_specs=[_tok_spec(tm), _tok_spec(tm), pl.BlockSpec((tm, LANES), lambda i: (i, 0)),
                  _tok_spec(tm), _mtab_spec(layer, cfg, tm)],
        out_specs=out_specs,
        out_shape=out_shape,
        compiler_params=_cparams(("arbitrary",)),
        name="moe_combine",
    )(y1, y2, route, x, mtab)


def moe_layer(x, mtab, layer, g, router_w, router_b, w_in, w_out, n, cfg, split=False):
    t = x.shape[0]
    tm = MOE_TILE
    u, route, cnt = moe_pre(x, mtab, layer, g, router_w, router_b, cfg)
    counts = cnt[0, :MOE_E].astype(jnp.int32)
    padded = ((counts + tm - 1) // tm) * tm
    ends = jnp.cumsum(padded)
    starts = ends - padded
    e = route[:, 0:2].astype(jnp.int32)
    dest = starts[e] + route[:, 4:6].astype(jnp.int32)
    npad = 2 * t + MOE_E * tm
    tok = jnp.arange(t, dtype=jnp.int32)
    row_tok = jnp.zeros((npad,), jnp.int32).at[jnp.concatenate([dest[:, 0], dest[:, 1]])].set(
        jnp.concatenate([tok, tok]), unique_indices=True)
    tile_start = jnp.arange(npad // tm, dtype=jnp.int32) * tm
    tile_active = (tile_start < ends[-1]).astype(jnp.int32)
    tile_expert = jnp.minimum(jnp.sum((tile_start[:, None] >= ends[None, :]).astype(jnp.int32), axis=1),
                              MOE_E - 1)
    last_e = tile_expert[jnp.maximum(jnp.sum(tile_active) - 1, 0)]
    tile_expert = jnp.where(tile_active == 1, tile_expert, last_e)
    xs = jnp.take(u, row_tok, axis=0, mode='clip')
    ys = moe_ffn(xs, tile_expert, tile_active, w_in, w_out, n)
    y1 = jnp.take(ys, dest[:, 0], axis=0, mode='clip')
    y2 = jnp.take(ys, dest[:, 1], axis=0, mode='clip')
    return moe_combine(y1, y2, route, x, mtab, layer, cfg, split)


def backbone(x_prompt, x_sample, cache_k, cache_v, state_rglru, state_hgrn, c, c_ctx, p):
    bp, lp, d = x_prompt.shape
    bs, ls, _ = x_sample.shape
    cfg = Cfg(bp, lp, bs, ls)
    x = jnp.concatenate([x_prompt.reshape(bp * lp, d), x_sample.reshape(bs * ls, d)], axis=0)
    cvec = jnp.concatenate([c_ctx[None, :], c], axis=0)
    mtab = modulation_table(cvec, p['mod_w'], p['mod_b'])
    new_k, new_v, new_rg, new_hg = [], [], [], []
    for i in range(DEPTH):
        kind, j = i % 4, i // 4
        g0 = p['norm_g'][i, 0]
        if kind == 0:
            x, st = rglru_layer(x, mtab, i, j, g0, p, state_rglru, cfg)
            new_rg.append(st)
        elif kind == 1:
            x, st = hgrn2_layer(x, mtab, i, j, g0, p, state_hgrn, cfg)
            new_hg.append(st)
        elif kind == 2:
            x = hyena_layer(x, mtab, i, j, g0, p, cfg)
        else:
            x, nk, nv = diffattn_layer(x, mtab, i, j, g0, p, cache_k, cache_v, cfg)
            new_k.append(nk)
            new_v.append(nv)
        n = i // 2
        g1 = p['norm_g'][i, 1]
        if i % 2 == 0:
            x = dense_ffn(x, mtab, i, g1, p['ff_w_in'][n].astype(BF16), p['ff_w_out'][n].astype(BF16), cfg)
        else:
            x = moe_layer(x, mtab, i, g1, p['moe_router'][n], p['moe_router_b'][n],
                          p['moe_w_in'], p['moe_w_out'], n, cfg, split=i == DEPTH - 1)
    xp, xs = x if isinstance(x, (list, tuple)) else (x[:cfg.tp], x[cfg.tp:])
    return (xp.reshape(bp, lp, d), xs.reshape(bs, ls, d),
            jnp.stack(new_k, axis=1), jnp.stack(new_v, axis=1),
            jnp.stack(new_rg, axis=1), jnp.stack(new_hg, axis=1))


def kernel(x_prompt, x_sample, cache_k, cache_v, state_rglru, state_hgrn, c, c_ctx, mod_w, mod_b, norm_g, hgrn_lb, rg_w_in, rg_conv_w, rg_conv_b, rg_w_a, rg_b_a, rg_w_x, rg_b_x, rg_lambda, rg_w_out, hg_w_in, hg_norm_g, hg_w_out, hy_w_in, hy_b_in, hy_conv_w, hy_conv_b, hy_f_w1, hy_f_b1, hy_f_w2, hy_f_b2, hy_f_w3, hy_freq, hy_decay, hy_bias, hy_w_out, hy_b_out, da_w_in, da_q_norm, da_k_norm, da_lambda, da_sub_norm, da_w_out, ff_w_in, ff_w_out, moe_router, moe_router_b, moe_w_in, moe_w_out):
    p = dict(mod_w=mod_w, mod_b=mod_b, norm_g=norm_g, hgrn_lb=hgrn_lb,
             rg_w_in=rg_w_in, rg_conv_w=rg_conv_w, rg_conv_b=rg_conv_b, rg_w_a=rg_w_a, rg_b_a=rg_b_a,
             rg_w_x=rg_w_x, rg_b_x=rg_b_x, rg_lambda=rg_lambda, rg_w_out=rg_w_out,
             hg_w_in=hg_w_in, hg_norm_g=hg_norm_g, hg_w_out=hg_w_out,
             hy_w_in=hy_w_in, hy_b_in=hy_b_in, hy_conv_w=hy_conv_w, hy_conv_b=hy_conv_b,
             hy_f_w1=hy_f_w1, hy_f_b1=hy_f_b1, hy_f_w2=hy_f_w2, hy_f_b2=hy_f_b2, hy_f_w3=hy_f_w3,
             hy_freq=hy_freq, hy_decay=hy_decay, hy_bias=hy_bias, hy_w_out=hy_w_out, hy_b_out=hy_b_out,
             da_w_in=da_w_in, da_q_norm=da_q_norm, da_k_norm=da_k_norm, da_lambda=da_lambda,
             da_sub_norm=da_sub_norm, da_w_out=da_w_out,
             ff_w_in=ff_w_in, ff_w_out=ff_w_out, moe_router=moe_router, moe_router_b=moe_router_b,
             moe_w_in=moe_w_in, moe_w_out=moe_w_out)
    return backbone(x_prompt, x_sample, cache_k, cache_v, state_rglru, state_hgrn, c, c_ctx, p)
```

```python
import functools
import math
from typing import NamedTuple

import numpy as np
import jax
import jax.numpy as jnp
from jax import lax
from jax.experimental import pallas as pl
from jax.experimental.pallas import tpu as pltpu

F32 = jnp.float32
BF16 = jnp.bfloat16

D_MODEL = 1024
DEPTH = 4
EPS = 1e-6
GRID_W = 64
RG_HEADS = 4
RG_BW = D_MODEL // RG_HEADS
RG_C = 8.0
HG_HEADS = 8
HG_DK = D_MODEL // HG_HEADS
HG_CHUNK = 32
HY_EMB = 33
HY_FW = 64
DA_HEADS = 8
DA_DH = 64
ROPE_BASE = 10000.0
FF_DIM = 2816
MOE_E = 8
MOE_FF = 3584

LANES = 128
SUBLANES = 8
ROW_TILE = 512
MOE_TILE = 1024
SEQ_TILE = 256
GLA_HEADS_PER_STEP = 4
DA_HEADS_PER_STEP = 4
VMEM_LIMIT = 56 * 1024 * 1024


class Cfg(NamedTuple):
    bp: int
    lp: int
    bs: int
    ls: int

    @property
    def tp(self):
        return self.bp * self.lp

    @property
    def ts(self):
        return self.bs * self.ls

    @property
    def t(self):
        return self.tp + self.ts


def _cparams(sem):
    return pltpu.CompilerParams(dimension_semantics=sem, vmem_limit_bytes=VMEM_LIMIT)


def _mod_idx(i, cfg, tm):
    npt = cfg.tp // tm
    return jnp.where(i < npt, 0, 1 + (i - npt) // (cfg.ls // tm))


def _modulate(x, g, m, k):
    ms = jnp.mean(x * x, axis=-1, keepdims=True)
    y = x * lax.rsqrt(ms + EPS) * g
    return y * (1.0 + m[k + 1:k + 2, :]) + m[k:k + 1, :]


def _silu(x):
    return x * jax.nn.sigmoid(x)


def _dot(a, b):
    return jnp.dot(a, b, preferred_element_type=F32)


def _dot_nt(a, b):
    return lax.dot_general(a, b, (((1,), (1,)), ((), ())), preferred_element_type=F32)


def _dot_tn(a, b):
    return lax.dot_general(a, b, (((0,), (0,)), ((), ())), preferred_element_type=F32)


def _dot_hi(a, b):
    return jnp.dot(a, b, preferred_element_type=F32, precision=lax.Precision.HIGHEST)


def _modtab_kernel(c_ref, w_ref, b_ref, o_ref):
    s = _silu(c_ref[...]).astype(BF16)
    o_ref[...] = _dot(s, w_ref[...].astype(BF16)) + b_ref[...]


def modulation_table(cvec, mod_w, mod_b):
    n = cvec.shape[0]
    npad = 16
    cpad = jnp.zeros((npad, D_MODEL), F32).at[:n].set(cvec)
    tn = 1536
    out = pl.pallas_call(
        _modtab_kernel,
        grid=(DEPTH, 6 * D_MODEL // tn),
        in_specs=[
            pl.BlockSpec((npad, D_MODEL), lambda l, j: (0, 0)),
            pl.BlockSpec((None, D_MODEL, tn), lambda l, j: (l, 0, j)),
            pl.BlockSpec((None, 1, tn), lambda l, j: (l, 0, j)),
        ],
        out_specs=pl.BlockSpec((None, npad, tn), lambda l, j: (l, 0, j)),
        out_shape=jax.ShapeDtypeStruct((DEPTH, npad, 6 * D_MODEL), F32),
        compiler_params=_cparams(("parallel", "parallel")),
        name="modtab",
    )(cpad, mod_w, mod_b.reshape(DEPTH, 1, 6 * D_MODEL))
    return out[:, :n].reshape(DEPTH, n, 6, D_MODEL)


def _modmm_kernel(*refs, k, has_bias, heads_out, tn):
    if has_bias:
        x_ref, m_ref, g_ref, w_ref, b_ref, o_ref = refs
    else:
        x_ref, m_ref, g_ref, w_ref, o_ref = refs
        b_ref = None
    u = _modulate(x_ref[...], g_ref[...], m_ref[...], k).astype(BF16)
    for c in range(w_ref.shape[1] // tn):
        cs = slice(c * tn, (c + 1) * tn)
        acc = _dot(u, w_ref[:, cs])
        if has_bias:
            acc = acc + b_ref[:, cs]
        if heads_out:
            for hh in range(tn // LANES):
                o_ref[c * (tn // LANES) + hh] = acc[:, hh * LANES:(hh + 1) * LANES]
        else:
            o_ref[:, cs] = acc


def mod_matmul(x, mtab, layer, k, g, w, b, cfg, heads_out=False, tn=1024):
    t, n = x.shape[0], w.shape[1]
    tm = ROW_TILE
    in_specs = [
        _tok_spec(tm),
        _mtab_spec(layer, cfg, tm),
        _const_spec((1, D_MODEL)),
        _const_spec((D_MODEL, n)),
    ]
    args = [x, mtab, g.reshape(1, D_MODEL), w]
    if b is not None:
        in_specs.append(_const_spec((1, n)))
        args.append(b.reshape(1, n))
    if heads_out:
        out_specs = pl.BlockSpec((n // LANES, tm, LANES), lambda i: (0, i, 0))
        out_shape = jax.ShapeDtypeStruct((n // LANES, t, LANES), F32)
    else:
        out_specs = pl.BlockSpec((tm, n), lambda i: (i, 0))
        out_shape = jax.ShapeDtypeStruct((t, n), F32)
    return pl.pallas_call(
        functools.partial(_modmm_kernel, k=k, has_bias=b is not None, heads_out=heads_out, tn=tn),
        grid=(t // tm,),
        in_specs=in_specs,
        out_specs=out_specs,
        out_shape=out_shape,
        compiler_params=_cparams(("parallel",)),
        name="modmm",
    )(*args)


def _seq_edges(i, cfg, lb):
    npb = cfg.tp // lb
    pp, ps = cfg.lp // lb, cfg.ls // lb
    first = jnp.where(i < npb, i % pp == 0, (i - npb) % ps == 0)
    last = jnp.where(i < npb, i % pp == pp - 1, (i - npb) % ps == ps - 1)
    return first, last


def _halo_conv(x_ref, p_ref, n_ref, w_ref, b_ref, cfg):
    lb = x_ref.shape[0]
    taps = w_ref.shape[0]
    first, last = _seq_edges(pl.program_id(0), cfg, lb)
    prev = jnp.where(first, 0.0, p_ref[...])
    nxt = jnp.where(last, 0.0, n_ref[...])
    ext = jnp.concatenate([prev, x_ref[...], nxt], axis=0)
    left = (taps - 1) // 2
    n_ext = lb + 16
    acc = jnp.zeros(x_ref.shape, F32) + b_ref[...]
    for kk in range(taps):
        sh = (left - kk) % n_ext
        shifted = ext if sh == 0 else pltpu.roll(ext, sh, 0)
        acc = acc + w_ref[kk:kk + 1, :] * shifted[8:8 + lb]
    return acc


def _halo_specs(t, lb, cb, tc):
    r8 = lb // 8
    nblk8 = t // 8
    return [pl.BlockSpec((lb, tc), lambda i: (i, cb)),
            pl.BlockSpec((8, tc), lambda i: (jnp.maximum(i * r8 - 1, 0), cb)),
            pl.BlockSpec((8, tc), lambda i: (jnp.minimum((i + 1) * r8, nblk8 - 1), cb))]


def _rg_gates_kernel(x_ref, p_ref, n_ref, cw_ref, cb_ref, wa_ref, wx_ref, ba_ref, bx_ref, lam_ref,
                     a_ref, b_ref, *, cfg):
    xc = _halo_conv(x_ref, p_ref, n_ref, cw_ref, cb_ref, cfg)
    xb = xc.astype(BF16)
    nlam = -lam_ref[...]
    sp = jnp.maximum(nlam, 0.0) + jnp.log1p(jnp.exp(-jnp.abs(nlam)))
    for d in range(2):
        ra = jnp.concatenate(
            [_dot(xb[:, h * RG_BW:(h + 1) * RG_BW], wa_ref[d, h]) for h in range(RG_HEADS)], axis=1)
        rx = jnp.concatenate(
            [_dot(xb[:, h * RG_BW:(h + 1) * RG_BW], wx_ref[d, h]) for h in range(RG_HEADS)], axis=1)
        r = jax.nn.sigmoid(ra + ba_ref[d:d + 1, :])
        ig = jax.nn.sigmoid(rx + bx_ref[d:d + 1, :])
        log_a = (-RG_C) * sp[d:d + 1, :] * r
        a = jnp.exp(log_a)
        gain = jnp.sqrt(-jnp.tanh(log_a) * (1.0 + a * a))
        a_ref[d] = a
        b_ref[d] = gain * ig * xc


def rg_gates(gx, conv_w, conv_b, w_a, w_x, b_a, b_x, lam, cfg):
    t = gx.shape[0]
    lb = SEQ_TILE
    taps = conv_w.shape[0]
    wspec = pl.BlockSpec((2, RG_HEADS, RG_BW, RG_BW), lambda i: (0, 0, 0, 0))
    vspec = pl.BlockSpec((2, D_MODEL), lambda i: (0, 0))
    ospec = pl.BlockSpec((2, lb, D_MODEL), lambda i: (0, i, 0))
    oshape = jax.ShapeDtypeStruct((2, t, D_MODEL), F32)
    return pl.pallas_call(
        functools.partial(_rg_gates_kernel, cfg=cfg),
        grid=(t // lb,),
        in_specs=_halo_specs(t, lb, 1, D_MODEL) + [
            pl.BlockSpec((taps, D_MODEL), lambda i: (0, 0)), pl.BlockSpec((1, D_MODEL), lambda i: (0, 0)),
            wspec, wspec, vspec, vspec, vspec],
        out_specs=[ospec, ospec],
        out_shape=[oshape, oshape],
        compiler_params=_cparams(("parallel",)),
        name="rg_gates",
    )(gx, gx, gx, conv_w, conv_b.reshape(1, D_MODEL), w_a, w_x, b_a, b_x, lam)


def _scan_groups(a, b, rev):
    rows = a.shape[0]
    r = lax.broadcasted_iota(jnp.int32, (rows, 1), 0) % SUBLANES
    s = 1
    while s < SUBLANES:
        shift = rows - s if rev else s
        keep = (r < SUBLANES - s) if rev else (r >= s)
        a_s = jnp.where(keep, pltpu.roll(a, shift, 0), 1.0)
        b_s = jnp.where(keep, pltpu.roll(b, shift, 0), 0.0)
        b = b + a * b_s
        a = a * a_s
        s *= 2
    return a, b


def _rg_scan_kernel(fblk, bblk, first, last, seqo, s0i, has0,
                    af_ref, bf_ref, ab_ref, bb_ref, h0_ref, hf_ref, hb_ref, hc_scr):
    i = pl.program_id(0)
    rows = af_ref.shape[0]
    ngroups = rows // SUBLANES

    @pl.when(first[i] == 1)
    def _():
        hc_scr[...] = jnp.where(has0[i] == 1, h0_ref[...], 0.0)

    for d, (a_ref, b_ref, o_ref) in enumerate(((af_ref, bf_ref, hf_ref), (ab_ref, bb_ref, hb_ref))):
        rev = d == 1
        decay, local = _scan_groups(a_ref[...], b_ref[...], rev)
        h = hc_scr[d:d + 1, :]
        for g in (range(ngroups - 1, -1, -1) if rev else range(ngroups)):
            sl = slice(g * SUBLANES, (g + 1) * SUBLANES)
            hg = decay[sl] * h + local[sl]
            o_ref[sl, :] = hg
            h = hg[0:1, :] if rev else hg[SUBLANES - 1:SUBLANES, :]
        hc_scr[d:d + 1, :] = h


def rg_scan(a, bx, h0, cfg):
    t = a.shape[1]
    rb = SEQ_TILE
    tabs = _seq_block_tables(cfg)
    nslots = tabs[0].shape[0]
    fwd = lambda arr: pl.BlockSpec((None, rb, D_MODEL), lambda i, fb, bb, *_: (0, fb[i], 0))
    bwd = lambda arr: pl.BlockSpec((None, rb, D_MODEL), lambda i, fb, bb, *_: (1, bb[i], 0))
    grid_spec = pltpu.PrefetchScalarGridSpec(
        num_scalar_prefetch=7,
        grid=(nslots,),
        in_specs=[
            fwd(a), fwd(bx), bwd(a), bwd(bx),
            pl.BlockSpec((None, 2, D_MODEL), lambda i, fb, bb, fi, la, so, s0i, *_: (s0i[i], 0, 0)),
        ],
        out_specs=[
            pl.BlockSpec((rb, D_MODEL), lambda i, fb, bb, *_: (fb[i], 0)),
            pl.BlockSpec((rb, D_MODEL), lambda i, fb, bb, *_: (bb[i], 0)),
        ],
        scratch_shapes=[pltpu.VMEM((2, D_MODEL), F32)],
    )
    oshape = jax.ShapeDtypeStruct((t, D_MODEL), F32)
    return pl.pallas_call(
        _rg_scan_kernel,
        grid_spec=grid_spec,
        out_shape=[oshape, oshape],
        compiler_params=_cparams(("arbitrary",)),
        name="rg_scan",
    )(*tabs, a, bx, a, bx, h0)


def _gelu_tanh(x):
    return 0.5 * x * (1.0 + jnp.tanh(math.sqrt(2.0 / math.pi) * (x + 0.044715 * (x * x * x))))


def _rg_out_kernel(gate_ref, hf_ref, hb_ref, w_ref, x_ref, m_ref, o_ref):
    y = (_gelu_tanh(gate_ref[...]) * (hf_ref[...] + hb_ref[...])).astype(BF16)
    o_ref[...] = x_ref[...] + m_ref[2:3, :] * _dot(y, w_ref[...])


def _tok_spec(tm, cb=0):
    return pl.BlockSpec((tm, D_MODEL), lambda i: (i, cb))


def _mtab_spec(layer, cfg, tm):
    return pl.BlockSpec((None, None, 6, D_MODEL), lambda i: (layer, _mod_idx(i, cfg, tm), 0, 0))


def _w_spec(k=D_MODEL):
    return pl.BlockSpec((k, D_MODEL), lambda i: (0, 0))


def rg_out(gx, hf, hb, w_out, x, mtab, layer, cfg):
    t = x.shape[0]
    tm = ROW_TILE
    return pl.pallas_call(
        _rg_out_kernel,
        grid=(t // tm,),
        in_specs=[_tok_spec(tm, 0), _tok_spec(tm), _tok_spec(tm), _w_spec(), _tok_spec(tm),
                  _mtab_spec(layer, cfg, tm)],
        out_specs=_tok_spec(tm),
        out_shape=jax.ShapeDtypeStruct((t, D_MODEL), F32),
        compiler_params=_cparams(("parallel",)),
        name="rg_out",
    )(gx, hf, hb, w_out, x, mtab)


def rglru_layer(x, mtab, layer, j, norm_g, p, state_rglru, cfg):
    gx = mod_matmul(x, mtab, layer, 0, norm_g, p['rg_w_in'][j].astype(BF16), None, cfg)
    a, bx = rg_gates(gx, p['rg_conv_w'][j], p['rg_conv_b'][j],
                     p['rg_w_a'][j].astype(BF16), p['rg_w_x'][j].astype(BF16),
                     p['rg_b_a'][j], p['rg_b_x'][j], p['rg_lambda'][j], cfg)
    hf, hb = rg_scan(a, bx, state_rglru[:, j].astype(F32), cfg)
    new_state = jnp.stack([hf[:cfg.tp].reshape(cfg.bp, cfg.lp, D_MODEL)[:, -1],
                           hb[:cfg.tp].reshape(cfg.bp, cfg.lp, D_MODEL)[:, 0]], axis=1)
    x = rg_out(gx, hf, hb, p['rg_w_out'][j].astype(BF16), x, mtab, layer, cfg)
    return x, new_state


def _chunk_cumsum(x, rev):
    rows = x.shape[0]
    r = lax.broadcasted_iota(jnp.int32, x.shape, 0) % HG_CHUNK
    s = 1
    while s < HG_CHUNK:
        if rev:
            x = x + jnp.where(r < HG_CHUNK - s, pltpu.roll(x, rows - s, 0), 0.0)
        else:
            x = x + jnp.where(r >= s, pltpu.roll(x, s, 0), 0.0)
        s *= 2
    return x


def _gla_kernel(fblk, bblk, first, last, seqo, s0i, has0,
                qf_ref, ff_ref, vf_ref, qb_ref, fb_ref, vb_ref, lb_ref, s0_ref,
                of_ref, ob_ref, sfin_ref, s_scr, *, layer):
    i = pl.program_id(1)
    hpb, rows = qf_ref.shape[0], qf_ref.shape[1]
    nch = rows // HG_CHUNK
    units = [(hh, d) for hh in range(hpb) for d in range(2)]

    @pl.when(first[i] == 1)
    def _():
        for hh, d in units:
            s_scr[hh, d] = jnp.where(has0[i] == 1, s0_ref[d, hh].T, 0.0)

    ri = lax.broadcasted_iota(jnp.int32, (rows, rows), 0)
    ci = lax.broadcasted_iota(jnp.int32, (rows, rows), 1)
    same = (ri // HG_CHUNK) == (ci // HG_CHUNK)
    chunks = [slice(n * HG_CHUNK, (n + 1) * HG_CHUNK) for n in range(nch)]
    in_refs = ((qf_ref, ff_ref, vf_ref), (qb_ref, fb_ref, vb_ref))
    out_refs = (of_ref, ob_ref)

    pre = {}
    for hh in range(hpb):
        lbx = lb_ref[hh]
        e = jnp.exp(lbx - jnp.max(lbx, axis=0, keepdims=True))
        sm = e / jnp.sum(e, axis=0, keepdims=True)
        lb = jnp.zeros(lbx.shape[1:], F32)
        for l in range(1, layer + 1):
            lb = lb + sm[l]
        for d in range(2):
            q_ref, f_ref, v_ref = in_refs[d]
            rev = d == 1
            v16 = v_ref[hh].astype(BF16)
            lbd = lb[d:d + 1, :]
            f = lbd + (1.0 - lbd) * jax.nn.sigmoid(f_ref[hh])
            k = 1.0 - f
            b = _chunk_cumsum(jnp.log(f), rev)
            b3 = b.reshape(nch, HG_CHUNK, HG_DK)
            bl = b3[:, 0:1, :] if rev else b3[:, HG_CHUNK - 1:HG_CHUNK, :]
            qi16 = (q_ref[hh] * jnp.exp(b)).astype(BF16)
            ki16 = (k * jnp.exp(-b)).astype(BF16)
            ks16 = (k.reshape(nch, HG_CHUNK, HG_DK) * jnp.exp(bl - b3)).reshape(rows, HG_DK).astype(BF16)
            g = jnp.exp(bl)
            mask = same & ((ci >= ri) if rev else (ci <= ri))
            att = jnp.where(mask, _dot_nt(qi16, ki16), 0.0).astype(BF16)
            ds = [_dot_tn(v16[sl], ks16[sl]) for sl in chunks]
            pre[hh, d] = (qi16, v16, att, ds, g)

    prev = {}
    for hh, d in units:
        _, _, _, ds, g = pre[hh, d]
        st = s_scr[hh, d]
        sp = [None] * nch
        for n in (range(nch - 1, -1, -1) if d == 1 else range(nch)):
            sp[n] = st.astype(BF16)
            st = st * g[n] + ds[n]
        s_scr[hh, d] = st
        prev[hh, d] = sp

    for hh, d in units:
        qi16, v16, att, _, _ = pre[hh, d]
        inter = [_dot_nt(qi16[sl], prev[hh, d][n]) for n, sl in enumerate(chunks)]
        out_refs[d][hh] = _dot(att, v16) + jnp.concatenate(inter, axis=0)

    @pl.when((last[i] == 1) & (has0[i] == 0))
    def _():
        for hh, d in units:
            sfin_ref[d, hh] = s_scr[hh, d].T


def _seq_block_tables(cfg):
    rb = SEQ_TILE
    pp, ps = cfg.lp // rb, cfg.ls // rb
    fblk, bblk, first, last, seqo, s0i, has0 = [], [], [], [], [], [], []
    for s in range(cfg.bp):
        for c in range(pp):
            fblk.append(s * pp + c); bblk.append(s * pp + pp - 1 - c)
            first.append(int(c == 0)); last.append(int(c == pp - 1))
            seqo.append(s); s0i.append(0); has0.append(0)
    base = cfg.tp // rb
    for s in range(cfg.bs):
        for c in range(ps):
            fblk.append(base + s * ps + c); bblk.append(base + s * ps + ps - 1 - c)
            first.append(int(c == 0)); last.append(int(c == ps - 1))
            seqo.append(cfg.bp - 1); s0i.append(s); has0.append(1)
    return [jnp.asarray(np.asarray(v, np.int32)) for v in (fblk, bblk, first, last, seqo, s0i, has0)]


def gla(proj, hgrn_lb, s0, layer, cfg):
    t = proj.shape[1]
    rb = SEQ_TILE
    tabs = _seq_block_tables(cfg)
    nslots = tabs[0].shape[0]
    h8 = HG_HEADS
    hpb = GLA_HEADS_PER_STEP
    ng = h8 // hpb

    def pspec(sec, which):
        return pl.BlockSpec((hpb, rb, HG_DK),
                            lambda h, i, fb, bb, *_: (sec * ng + h, (fb if which == 0 else bb)[i], 0))

    ospec_f = pl.BlockSpec((hpb, rb, HG_DK), lambda h, i, fb, bb, *_: (h, fb[i], 0))
    ospec_b = pl.BlockSpec((hpb, rb, HG_DK), lambda h, i, fb, bb, *_: (h, bb[i], 0))
    grid_spec = pltpu.PrefetchScalarGridSpec(
        num_scalar_prefetch=7,
        grid=(ng, nslots),
        in_specs=[
            pspec(0, 0), pspec(1, 0), pspec(3, 0),
            pspec(0, 1), pspec(2, 1), pspec(3, 1),
            pl.BlockSpec((hpb, DEPTH, 2, HG_DK), lambda h, i, *_: (h, 0, 0, 0)),
            pl.BlockSpec((None, 2, hpb, HG_DK, HG_DK),
                         lambda h, i, fb, bb, fi, la, so, s0i, *_: (s0i[i], 0, h, 0, 0)),
        ],
        out_specs=[
            ospec_f, ospec_b,
            pl.BlockSpec((None, 2, hpb, HG_DK, HG_DK),
                         lambda h, i, fb, bb, fi, la, so, *_: (so[i], 0, h, 0, 0)),
        ],
        scratch_shapes=[pltpu.VMEM((hpb, 2, HG_DK, HG_DK), F32)],
    )
    oshape = jax.ShapeDtypeStruct((h8, t, HG_DK), F32)
    of, ob, sfin = pl.pallas_call(
        functools.partial(_gla_kernel, layer=layer),
        grid_spec=grid_spec,
        out_shape=[oshape, oshape,
                   jax.ShapeDtypeStruct((cfg.bp, 2, h8, HG_DK, HG_DK), F32)],
        compiler_params=_cparams(("parallel", "arbitrary")),
        name="gla",
    )(*tabs, proj, proj, proj, proj, proj, proj,
      hgrn_lb.reshape(DEPTH, 2, h8, HG_DK).transpose(2, 0, 1, 3), s0)
    return of, ob, sfin


def _head_rms(o, g):
    ms = jnp.mean(o * o, axis=-1, keepdims=True)
    return o * lax.rsqrt(ms + EPS) * g


def _hg_out_kernel(of_ref, ob_ref, gh_ref, ng_ref, w_ref, x_ref, m_ref, o_ref):
    parts = []
    for h in range(HG_HEADS):
        o = _head_rms(of_ref[h] + ob_ref[h], ng_ref[...]) * _silu(gh_ref[h])
        parts.append(o.astype(BF16))
    y = jnp.concatenate(parts, axis=1)
    o_ref[...] = x_ref[...] + m_ref[2:3, :] * _dot(y, w_ref[...])


def hg_out(of, ob, proj, norm_g, w_out, x, mtab, layer, cfg):
    t = x.shape[0]
    tm = ROW_TILE
    hspec = pl.BlockSpec((HG_HEADS, tm, HG_DK), lambda i: (0, i, 0))
    return pl.pallas_call(
        _hg_out_kernel,
        grid=(t // tm,),
        in_specs=[hspec, hspec,
                  pl.BlockSpec((HG_HEADS, tm, HG_DK), lambda i: (4, i, 0)),
                  pl.BlockSpec((1, HG_DK), lambda i: (0, 0)),
                  _w_spec(), _tok_spec(tm), _mtab_spec(layer, cfg, tm)],
        out_specs=_tok_spec(tm),
        out_shape=jax.ShapeDtypeStruct((t, D_MODEL), F32),
        compiler_params=_cparams(("parallel",)),
        name="hg_out",
    )(of, ob, proj, norm_g.reshape(1, HG_DK), w_out, x, mtab)


def hgrn2_layer(x, mtab, layer, j, norm_g, p, state_hgrn, cfg):
    proj = mod_matmul(x, mtab, layer, 0, norm_g, p['hg_w_in'][j].astype(BF16), None, cfg, heads_out=True)
    of, ob, sfin = gla(proj, p['hgrn_lb'], state_hgrn[:, j].astype(F32), layer, cfg)
    x = hg_out(of, ob, proj, p['hg_norm_g'][j], p['hg_w_out'][j].astype(BF16), x, mtab, layer, cfg)
    return x, sfin


def _dft_tables(length):
    a, b = _dft_tables_np(length)
    return jnp.asarray(a), jnp.asarray(b), jnp.asarray(np.ascontiguousarray(b.T))


@functools.lru_cache(maxsize=None)
def _dft_tables_np(length):
    n = 2 * length
    idx = np.arange(length, dtype=np.int64)
    ang = ((idx[:, None] * idx[None, :]) % n).astype(np.float64) * (2.0 * math.pi / n)
    a = np.cos(ang)
    b = -np.sin(ang)
    b[0, :] = np.where(idx % 2 == 0, 1.0, -1.0)
    return a.astype(BF16), b.astype(BF16)


@functools.lru_cache(maxsize=None)
def _hy_features_np(length):
    t_idx = np.arange(length, dtype=np.float64)
    bands = (HY_EMB - 1) // 2
    fr = np.linspace(1e-4, bands - 1, bands)
    ang = (2.0 * math.pi * t_idx / length)[:, None] * fr[None, :]
    z = np.zeros((length, LANES), np.float32)
    z[:, 0] = t_idx / (length - 1)
    z[:, 1:1 + bands] = np.cos(ang)
    z[:, 1 + bands:HY_EMB] = -np.sin(ang)
    return z


def _hy_features(length):
    return jnp.asarray(_hy_features_np(length))


def _hy_filter_kernel(z_ref, w1_ref, b1_ref, w2_ref, b2_ref, w3f_ref, w3b_ref, fq_ref, dcf_ref, dcb_ref,
                      a_ref, b_ref, ka_ref, ki_ref, kn_ref):
    length = z_ref.shape[0]
    fq = fq_ref[...]
    h = jnp.sin(fq * (_dot_hi(z_ref[...], w1_ref[...]) + b1_ref[...]))
    h = jnp.sin(fq * (_dot_hi(h, w2_ref[...]) + b2_ref[...]))
    row = lax.broadcasted_iota(jnp.int32, (length, 1), 0)
    tt = row.astype(F32) / float(length - 1)
    hf = _dot_hi(h, w3f_ref[...]) * jnp.exp(-tt * jnp.abs(dcf_ref[...]))
    hb = _dot_hi(h, w3b_ref[...]) * jnp.exp(-tt * jnp.abs(dcb_ref[...]))
    hb = jnp.where(row == 0, 0.0, hb)
    nrm = lax.rsqrt(jnp.sum(hf * hf + hb * hb, axis=0, keepdims=True) + EPS)
    hf = hf * nrm
    hb = hb * nrm
    a = a_ref[...]
    bm = b_ref[...]
    ka_ref[...] = _dot(a, (hf + hb).astype(BF16))
    kbf = _dot(bm, hf.astype(BF16))
    kbb = _dot(bm, hb.astype(BF16))
    ki_ref[...] = jnp.where(row == 0, 0.0, kbf - kbb)
    kn_ref[...] = jnp.broadcast_to(kbf[0:1, :] + kbb[0:1, :], kn_ref.shape)


def _const_spec(shape):
    nd = len(shape)
    return pl.BlockSpec(shape, lambda *_: (0,) * nd, pipeline_mode=pl.Buffered(1))


def hy_filter_spectrum(length, a_tab, b_tab, p, j, tc=256):
    d = D_MODEL
    nct = d // tc
    w1 = jnp.zeros((LANES, HY_FW), F32).at[:HY_EMB].set(p['hy_f_w1'][j])
    w3 = p['hy_f_w3'][j]
    dec = p['hy_decay'][j].reshape(1, 4 * d)
    z = _hy_features(length)
    fwd_col = lambda o, c: (0, (o * 2) * nct + c)
    bwd_col = lambda o, c: (0, (o * 2 + 1) * nct + c)
    return pl.pallas_call(
        _hy_filter_kernel,
        grid=(2, nct),
        in_specs=[
            _const_spec((length, LANES)), _const_spec((LANES, HY_FW)), _const_spec((1, HY_FW)),
            _const_spec((HY_FW, HY_FW)), _const_spec((1, HY_FW)),
            pl.BlockSpec((HY_FW, tc), fwd_col), pl.BlockSpec((HY_FW, tc), bwd_col),
            _const_spec((1, HY_FW)),
            pl.BlockSpec((1, tc), fwd_col), pl.BlockSpec((1, tc), bwd_col),
            _const_spec((length, length)), _const_spec((length, length)),
        ],
        out_specs=[
            pl.BlockSpec((None, length, tc), lambda o, c: (o, 0, c)),
            pl.BlockSpec((None, length, tc), lambda o, c: (o, 0, c)),
            pl.BlockSpec((None, 8, tc), lambda o, c: (o, 0, c)),
        ],
        out_shape=[jax.ShapeDtypeStruct((2, length, d), F32),
                   jax.ShapeDtypeStruct((2, length, d), F32),
                   jax.ShapeDtypeStruct((2, 8, d), F32)],
        compiler_params=_cparams(("parallel", "parallel")),
        name="hy_filter",
    )(z, w1, p['hy_f_b1'][j].reshape(1, HY_FW), p['hy_f_w2'][j], p['hy_f_b2'][j].reshape(1, HY_FW),
      w3, w3, p['hy_freq'][j].reshape(1, HY_FW), dec, dec, a_tab, b_tab)


def _seq_conv(x, w, b):
    length = x.shape[0]
    taps = w.shape[0]
    left = (taps - 1) // 2
    row = lax.broadcasted_iota(jnp.int32, (length, 1), 0)
    acc = b + w[left:left + 1, :] * x
    for kk in range(taps):
        off = kk - left
        if off != 0:
            shifted = pltpu.roll(x, (-off) % length, 0)
            valid = (row + off >= 0) & (row + off < length)
            acc = acc + w[kk:kk + 1, :] * jnp.where(valid, shifted, 0.0)
    return acc


def _sconv_kernel(*refs, fb, conv_z, aliased):
    (zin_ref, xin_ref, a_ref, b_ref, bt_ref, ka_ref, ki_ref, kn_ref, bias_ref,
     cwz_ref, cbz_ref, cwx_ref, cbx_ref) = refs[:13]
    o_ref, yr_scr, yi_scr, z_ref, x_ref = refs[14:] if aliased else refs[13:]
    length = zin_ref.shape[0]
    z_ref[...] = _seq_conv(zin_ref[...], cwz_ref[...], cbz_ref[...]) if conv_z else zin_ref[...]
    x_ref[...] = _seq_conv(xin_ref[...], cwx_ref[...], cbx_ref[...])
    zb = z_ref[...].astype(BF16)
    inv = 1.0 / (2 * length)
    for kf in range(length // fb):
        rows = slice(kf * fb, (kf + 1) * fb)
        pr = _dot(a_ref[rows, :], zb)
        qi = _dot(b_ref[rows, :], zb)
        ka = ka_ref[rows, :]
        ki = ki_ref[rows, :]
        if kf == 0:
            r0 = lax.broadcasted_iota(jnp.int32, (fb, 1), 0) == 0
            kd = jnp.where(r0, kn_ref[0:1, :], ka)
            wgt = jnp.where(r0, inv, 2.0 * inv)
        else:
            kd = ka
            wgt = 2.0 * inv
        yr_scr[rows, :] = ((pr * ka - qi * ki) * wgt).astype(BF16)
        yi_scr[rows, :] = ((pr * ki + qi * kd) * wgt).astype(BF16)
    for tb in range(length // fb):
        rows = slice(tb * fb, (tb + 1) * fb)
        y = _dot(a_ref[rows, :], yr_scr[...]) + _dot(bt_ref[rows, :], yi_scr[...])
        o_ref[rows, :] = x_ref[rows, :] * (y + bias_ref[...] * z_ref[rows, :])


def sconv_group(zarr, zcol, conv_z, xarr, xcol, conv_w, conv_b, seq0, nseq, length, tabs, ka, ki, kn,
                order, bias, out_prev, tc):
    a_tab, b_tab, bt_tab = tabs
    d = D_MODEL
    t = zarr.shape[0]
    fb = min(256, length)
    zc, xc = zcol // tc, xcol // tc
    taps = conv_w.shape[0]
    cb = conv_b.reshape(1, -1)
    in_specs = [
        pl.BlockSpec((length, tc), lambda s, c: (seq0 + s, zc + c)),
        pl.BlockSpec((length, tc), lambda s, c: (seq0 + s, xc + c)),
        _const_spec((length, length)), _const_spec((length, length)), _const_spec((length, length)),
        pl.BlockSpec((None, length, tc), lambda s, c: (order, 0, c)),
        pl.BlockSpec((None, length, tc), lambda s, c: (order, 0, c)),
        pl.BlockSpec((None, 8, tc), lambda s, c: (order, 0, c)),
        pl.BlockSpec((None, 1, tc), lambda s, c: (order, 0, c)),
        pl.BlockSpec((taps, tc), lambda s, c: (0, zc + c if conv_z else c)),
        pl.BlockSpec((1, tc), lambda s, c: (0, zc + c if conv_z else c)),
        pl.BlockSpec((taps, tc), lambda s, c: (0, xc + c)),
        pl.BlockSpec((1, tc), lambda s, c: (0, xc + c)),
    ]
    args = [zarr, xarr, a_tab, b_tab, bt_tab, ka, ki, kn, bias.reshape(2, 1, d), conv_w, cb, conv_w, cb]
    aliases = {}
    if out_prev is not None:
        in_specs.append(pl.BlockSpec(memory_space=pl.ANY))
        args.append(out_prev)
        aliases = {len(args) - 1: 0}
    return pl.pallas_call(
        functools.partial(_sconv_kernel, fb=fb, conv_z=conv_z, aliased=out_prev is not None),
        grid=(nseq, d // tc),
        in_specs=in_specs,
        out_specs=pl.BlockSpec((length, tc), lambda s, c: (seq0 + s, c)),
        out_shape=jax.ShapeDtypeStruct((t, d), F32),
        scratch_shapes=[pltpu.VMEM((length, tc), BF16), pltpu.VMEM((length, tc), BF16),
                        pltpu.VMEM((length, tc), F32), pltpu.VMEM((length, tc), F32)],
        input_output_aliases=aliases,
        compiler_params=_cparams(("parallel", "parallel")),
        name="sconv",
    )(*args)


def _lin_out_kernel(a_ref, w_ref, b_ref, x_ref, m_ref, o_ref):
    y = _dot(a_ref[...].astype(BF16), w_ref[...]) + b_ref[...]
    o_ref[...] = x_ref[...] + m_ref[2:3, :] * y


def lin_out(a, w, b, x, mtab, layer, cfg):
    t = x.shape[0]
    tm = ROW_TILE
    return pl.pallas_call(
        _lin_out_kernel,
        grid=(t // tm,),
        in_specs=[_tok_spec(tm), _w_spec(), pl.BlockSpec((1, D_MODEL), lambda i: (0, 0)),
                  _tok_spec(tm), _mtab_spec(layer, cfg, tm)],
        out_specs=_tok_spec(tm),
        out_shape=jax.ShapeDtypeStruct((t, D_MODEL), F32),
        compiler_params=_cparams(("parallel",)),
        name="lin_out",
    )(a, w, b.reshape(1, D_MODEL), x, mtab)


def hyena_layer(x, mtab, layer, j, norm_g, p, cfg):
    d = D_MODEL
    pre = mod_matmul(x, mtab, layer, 0, norm_g, p['hy_w_in'][j].astype(BF16), p['hy_b_in'][j], cfg)
    cw, cb, bias = p['hy_conv_w'][j], p['hy_conv_b'][j], p['hy_bias'][j]
    groups = []
    for (row0, nseq, length) in ((0, cfg.bp, cfg.lp), (cfg.tp, cfg.bs, cfg.ls)):
        tabs = _dft_tables(length)
        spec = hy_filter_spectrum(length, tabs[0], tabs[1], p, j)
        tc = 256 if length > 512 else d
        groups.append((row0 // length, nseq, length, tabs, spec, tc))
    z1 = None
    for seq0, nseq, length, tabs, (ka, ki, kn), tc in groups:
        z1 = sconv_group(pre, 0, True, pre, d, cw, cb, seq0, nseq, length, tabs, ka, ki, kn, 0, bias, z1, tc)
    z2 = None
    for seq0, nseq, length, tabs, (ka, ki, kn), tc in groups:
        z2 = sconv_group(z1, 0, False, pre, 2 * d, cw, cb, seq0, nseq, length, tabs, ka, ki, kn, 1, bias,
                         z2, tc)
    return lin_out(z2, p['hy_w_out'][j].astype(BF16), p['hy_b_out'][j], x, mtab, layer, cfg)


def _group_rms_scale(x, g1, g1t):
    x2 = x * x
    hi = x2.astype(BF16)
    lo = (x2 - hi.astype(F32)).astype(BF16)
    s = _dot(hi, g1) + _dot(lo, g1)
    r = lax.rsqrt(s * (1.0 / DA_DH) + EPS)
    rhi = r.astype(BF16)
    rlo = (r - rhi.astype(F32)).astype(BF16)
    return _dot(rhi, g1t) + _dot(rlo, g1t)


def _rope(x, cos, sin_signed):
    lane = lax.broadcasted_iota(jnp.int32, x.shape, 1)
    w = x.shape[1]
    nf = DA_DH // 4
    rot = jnp.where(lane % (2 * nf) < nf, pltpu.roll(x, w - nf, 1), pltpu.roll(x, nf, 1))
    return x * cos + rot * sin_signed


def _qk_prep_kernel(q_ref, k_ref, v_ref, qg_ref, kg_ref, g1_ref, g1t_ref, cos_ref, sin_ref,
                    qh_ref, kh_ref, vh_ref, kc_ref, vc_ref, *, cfg):
    tm = q_ref.shape[0]
    i = pl.program_id(0)
    is_prompt = i < cfg.tp // tm
    g1 = g1_ref[...]
    g1t = g1t_ref[...]
    q = q_ref[...]
    k = k_ref[...]
    qn = q * _group_rms_scale(q, g1, g1t) * qg_ref[...]
    kn = k * _group_rms_scale(k, g1, g1t) * kg_ref[...]
    vo = v_ref[...].astype(BF16)
    for h in range(DA_HEADS):
        vh_ref[h] = vo[:, h * LANES:(h + 1) * LANES]

    def emit(qv, kv):
        qo = (qv * (DA_DH ** -0.5)).astype(BF16)
        ko = kv.astype(BF16)
        for h in range(DA_HEADS):
            sl = slice(h * LANES, (h + 1) * LANES)
            qh_ref[h] = qo[:, sl]
            kh_ref[h] = ko[:, sl]

    @pl.when(is_prompt)
    def _():
        kc_ref[...] = kn
        vc_ref[...] = v_ref[...]
        emit(qn, kn)

    @pl.when(jnp.logical_not(is_prompt))
    def _():
        reps = q.shape[1] // LANES
        cos = jnp.tile(cos_ref[...], (1, reps))
        sin = jnp.tile(sin_ref[...], (1, reps))
        emit(_rope(qn, cos, sin), _rope(kn, cos, sin))


@functools.lru_cache(maxsize=None)
def _rope_tables_np(length):
    nf = DA_DH // 4
    t = np.arange(length)
    pos = np.stack([t // GRID_W, t % GRID_W], axis=-1).astype(np.float64)
    inv = ROPE_BASE ** (-np.arange(nf, dtype=np.float64) / nf)
    ang = pos[:, :, None] * inv
    cos = np.broadcast_to(np.cos(ang)[:, :, None, :], (length, 2, 2, nf)).reshape(length, DA_DH)
    sin = np.sin(ang)[:, :, None, :]
    sin = np.concatenate([-sin, sin], axis=2).reshape(length, DA_DH)
    reps = LANES // DA_DH
    return (np.tile(cos, (1, reps)).astype(np.float32), np.tile(sin, (1, reps)).astype(np.float32))


def _rope_tables(length):
    cos, sin = _rope_tables_np(length)
    return jnp.asarray(cos), jnp.asarray(sin)


def qk_prep(qkv, q_g, k_g, cfg):
    t = qkv.shape[0]
    tm = ROW_TILE
    d = D_MODEL
    ngrp = d // DA_DH
    grp = jnp.arange(d) // DA_DH
    g1 = (grp[:, None] == jnp.arange(LANES)[None, :]).astype(BF16)
    g1t = g1.T
    cos, sin = _rope_tables(cfg.ls)
    npt = cfg.tp // tm
    pps = cfg.ls // tm
    tab_spec = pl.BlockSpec((tm, LANES), lambda i: (jnp.maximum(i - npt, 0) % pps, 0))
    hspec = pl.BlockSpec((DA_HEADS, tm, LANES), lambda i: (0, i, 0))
    hshape = jax.ShapeDtypeStruct((DA_HEADS, t, LANES), BF16)
    return pl.pallas_call(
        functools.partial(_qk_prep_kernel, cfg=cfg),
        grid=(t // tm,),
        in_specs=[_tok_spec(tm, 0), _tok_spec(tm, 1), _tok_spec(tm, 2),
                  pl.BlockSpec((1, d), lambda i: (0, 0)), pl.BlockSpec((1, d), lambda i: (0, 0)),
                  pl.BlockSpec((d, LANES), lambda i: (0, 0)), pl.BlockSpec((LANES, d), lambda i: (0, 0)),
                  tab_spec, tab_spec],
        out_specs=[hspec, hspec, hspec,
                   pl.BlockSpec((tm, d), lambda i: (jnp.minimum(i, npt - 1), 0)),
                   pl.BlockSpec((tm, d), lambda i: (jnp.minimum(i, npt - 1), 0))],
        out_shape=[hshape, hshape, hshape, jax.ShapeDtypeStruct((cfg.tp, d), F32),
                   jax.ShapeDtypeStruct((cfg.tp, d), F32)],
        compiler_params=_cparams(("arbitrary",)),
        name="qk_prep",
    )(qkv, qkv, qkv, jnp.tile(q_g, ngrp).reshape(1, d), jnp.tile(k_g, ngrp).reshape(1, d),
      g1, g1t, cos, sin)


def _dattn_kernel(*refs, has_cache, aliased, lam_init):
    n_in = 4 + (2 if has_cache else 0)
    lp_ref, q_ref, k_ref, v_ref = refs[:4]
    o_ref = refs[n_in + (1 if aliased else 0)]
    lp = lp_ref[...]
    lam = (jnp.exp(jnp.sum(lp[0:1] * lp[1:2], axis=1, keepdims=True))
           - jnp.exp(jnp.sum(lp[2:3] * lp[3:4], axis=1, keepdims=True)) + lam_init)
    lane = lax.broadcasted_iota(jnp.int32, q_ref.shape[1:], 1)
    for hh in range(q_ref.shape[0]):
        q = q_ref[hh]
        zero = jnp.zeros_like(q)
        qs = (jnp.where(lane < DA_DH, q, zero), jnp.where(lane >= DA_DH, q, zero))
        k = k_ref[hh]
        v = v_ref[hh]
        if has_cache:
            ck = refs[4][:, hh * LANES:(hh + 1) * LANES].astype(BF16)
            cv = refs[5][:, hh * LANES:(hh + 1) * LANES].astype(BF16)
        halves = []
        for half in range(2):
            s = _dot_nt(qs[half], k)
            m = jnp.max(s, axis=-1, keepdims=True)
            if has_cache:
                c = _dot_nt(qs[half], ck)
                m = jnp.maximum(m, jnp.max(c, axis=-1, keepdims=True))
            e = jnp.exp(s - m)
            z = jnp.sum(e, axis=-1, keepdims=True)
            pv = _dot(e.astype(BF16), v)
            if has_cache:
                ec = jnp.exp(c - m)
                z = z + jnp.sum(ec, axis=-1, keepdims=True)
                pv = pv + _dot(ec.astype(BF16), cv)
            halves.append(pv / z)
        o_ref[hh] = halves[0] - lam * halves[1]


def dattn_group(qh, kh, vh, da_lambda, lam_init, seq0, nseq, length, cache_k=None, cache_v=None,
                prev=None, tq=256):
    t = qh.shape[1]
    tq = min(tq, length)
    nq = length // tq
    has_cache = cache_k is not None
    hpb = DA_HEADS_PER_STEP
    in_specs = [
        pl.BlockSpec((4, DA_DH), lambda b, h, qi: (0, 0)),
        pl.BlockSpec((hpb, tq, LANES), lambda b, h, qi: (h, (seq0 + b) * nq + qi, 0)),
        pl.BlockSpec((hpb, length, LANES), lambda b, h, qi: (h, seq0 + b, 0)),
        pl.BlockSpec((hpb, length, LANES), lambda b, h, qi: (h, seq0 + b, 0)),
    ]
    args = [da_lambda, qh, kh, vh]
    if has_cache:
        past = cache_k.shape[1]
        cspec = pl.BlockSpec((None, past, hpb * LANES), lambda b, h, qi: (b, 0, h))
        in_specs += [cspec, cspec]
        args += [cache_k.reshape(nseq, past, DA_HEADS * LANES), cache_v.reshape(nseq, past, DA_HEADS * LANES)]
    aliases = {}
    if prev is not None:
        in_specs.append(pl.BlockSpec(memory_space=pl.ANY))
        args.append(prev)
        aliases = {len(args) - 1: 0}
    return pl.pallas_call(
        functools.partial(_dattn_kernel, has_cache=has_cache, aliased=prev is not None, lam_init=lam_init),
        grid=(nseq, DA_HEADS // hpb, nq),
        in_specs=in_specs,
        out_specs=pl.BlockSpec((hpb, tq, LANES), lambda b, h, qi: (h, (seq0 + b) * nq + qi, 0)),
        out_shape=jax.ShapeDtypeStruct((DA_HEADS, t, LANES), F32),
        input_output_aliases=aliases,
        compiler_params=_cparams(("parallel", "parallel", "parallel")),
        name="dattn",
    )(*args)


def _da_out_kernel(o_ref, sg_ref, w_ref, x_ref, m_ref, out_ref, *, lam_init):
    parts = []
    for h in range(DA_HEADS):
        parts.append((_head_rms(o_ref[h], sg_ref[...]) * (1.0 - lam_init)).astype(BF16))
    y = jnp.concatenate(parts, axis=1)
    out_ref[...] = x_ref[...] + m_ref[2:3, :] * _dot(y, w_ref[...])


def da_out(o, sub_g, w_out, x, mtab, layer, lam_init, cfg):
    t = x.shape[0]
    tm = ROW_TILE
    return pl.pallas_call(
        functools.partial(_da_out_kernel, lam_init=lam_init),
        grid=(t // tm,),
        in_specs=[pl.BlockSpec((DA_HEADS, tm, LANES), lambda i: (0, i, 0)),
                  pl.BlockSpec((1, LANES), lambda i: (0, 0)),
                  _w_spec(), _tok_spec(tm), _mtab_spec(layer, cfg, tm)],
        out_specs=_tok_spec(tm),
        out_shape=jax.ShapeDtypeStruct((t, D_MODEL), F32),
        compiler_params=_cparams(("parallel",)),
        name="da_out",
    )(o, sub_g.reshape(1, LANES), w_out, x, mtab)


def diffattn_layer(x, mtab, layer, j, norm_g, p, cache_k, cache_v, cfg):
    d = D_MODEL
    lam_init = 0.8 - 0.6 * math.exp(-0.3 * layer)
    qkv = mod_matmul(x, mtab, layer, 0, norm_g, p['da_w_in'][j].astype(BF16), None, cfg)
    qh, kh, vh, kc, vc = qk_prep(qkv, p['da_q_norm'][j], p['da_k_norm'][j], cfg)
    lamp = p['da_lambda'][j].astype(F32)
    o = dattn_group(qh, kh, vh, lamp, lam_init, 0, cfg.bp, cfg.lp)
    o = dattn_group(qh, kh, vh, lamp, lam_init, cfg.tp // cfg.ls, cfg.bs, cfg.ls,
                    cache_k[:, j], cache_v[:, j], prev=o)
    x = da_out(o, p['da_sub_norm'][j], p['da_w_out'][j].astype(BF16), x, mtab, layer, lam_init, cfg)
    new_k = kc.reshape(cfg.bp, cfg.lp, DA_HEADS, 2 * DA_DH)
    new_v = vc.reshape(cfg.bp, cfg.lp, DA_HEADS, 2 * DA_DH)
    return x, new_k, new_v


def _ffn_kernel(x_ref, m_ref, g_ref, wg_ref, wh_ref, wo_ref, o_ref, u_scr, acc_scr):
    f = pl.program_id(1)

    @pl.when(f == 0)
    def _():
        u_scr[...] = _modulate(x_ref[...], g_ref[...], m_ref[...], 3).astype(BF16)
        acc_scr[...] = jnp.zeros_like(acc_scr)

    u = u_scr[...]
    a = (_silu(_dot(u, wg_ref[...])) * _dot(u, wh_ref[...])).astype(BF16)
    acc_scr[...] += _dot(a, wo_ref[...])

    @pl.when(f == pl.num_programs(1) - 1)
    def _():
        o_ref[...] = x_ref[...] + m_ref[5:6, :] * acc_scr[...]


def dense_ffn(x, mtab, layer, g, w_in, w_out, cfg, tf=1408):
    t = x.shape[0]
    tm = ROW_TILE
    nf = FF_DIM // tf
    return pl.pallas_call(
        _ffn_kernel,
        grid=(t // tm, nf),
        in_specs=[
            pl.BlockSpec((tm, D_MODEL), lambda i, f: (i, 0)),
            pl.BlockSpec((None, None, 6, D_MODEL), lambda i, f: (layer, _mod_idx(i, cfg, tm), 0, 0)),
            pl.BlockSpec((1, D_MODEL), lambda i, f: (0, 0)),
            pl.BlockSpec((D_MODEL, tf), lambda i, f: (0, f)),
            pl.BlockSpec((D_MODEL, tf), lambda i, f: (0, nf + f)),
            pl.BlockSpec((tf, D_MODEL), lambda i, f: (f, 0)),
        ],
        out_specs=pl.BlockSpec((tm, D_MODEL), lambda i, f: (i, 0)),
        out_shape=jax.ShapeDtypeStruct((t, D_MODEL), F32),
        scratch_shapes=[pltpu.VMEM((tm, D_MODEL), BF16), pltpu.VMEM((tm, D_MODEL), F32)],
        compiler_params=_cparams(("parallel", "arbitrary")),
        name="ffn",
    )(x, mtab, g.reshape(1, D_MODEL), w_in, w_in, w_out)


def _moe_pre_kernel(x_ref, m_ref, g_ref, rw_ref, rb_ref, u_ref, r_ref, c_ref, cnt_scr):
    tm = x_ref.shape[0]

    @pl.when(pl.program_id(0) == 0)
    def _():
        cnt_scr[...] = jnp.zeros_like(cnt_scr)

    u = _modulate(x_ref[...], g_ref[...], m_ref[...], 3)
    bits = pltpu.bitcast(u.astype(BF16).astype(F32), jnp.uint32)
    half = D_MODEL // 2
    u_ref[...] = pltpu.bitcast((bits[:, :half] >> 16) | (bits[:, half:] & jnp.uint32(0xFFFF0000)), F32)
    logits = _dot_hi(u, rw_ref[...]) + rb_ref[...]
    lane = lax.broadcasted_iota(jnp.int32, logits.shape, 1)
    neg = -jnp.inf
    lg = jnp.where(lane < MOE_E, logits, neg)
    m1 = jnp.max(lg, axis=-1, keepdims=True)
    i1 = jnp.min(jnp.where(lg == m1, lane, LANES), axis=-1, keepdims=True)
    lg2 = jnp.where(lane == i1, neg, lg)
    m2 = jnp.max(lg2, axis=-1, keepdims=True)
    i2 = jnp.min(jnp.where(lg2 == m2, lane, LANES), axis=-1, keepdims=True)
    e2 = jnp.exp(m2 - m1)
    w1 = 1.0 / (1.0 + e2)
    w2 = e2 / (1.0 + e2)
    hit = jnp.where((lane == i1) | (lane == i2), 1.0, 0.0)
    ri = lax.broadcasted_iota(jnp.int32, (tm, tm), 0)
    ci = lax.broadcasted_iota(jnp.int32, (tm, tm), 1)
    ahead = jnp.where(ci < ri, 1.0, 0.0).astype(BF16)
    pos = _dot(ahead, hit.astype(BF16)) + cnt_scr[...]
    r1 = jnp.sum(jnp.where(lane == i1, pos, 0.0), axis=-1, keepdims=True)
    r2 = jnp.sum(jnp.where(lane == i2, pos, 0.0), axis=-1, keepdims=True)
    cnt = cnt_scr[...] + jnp.sum(hit, axis=0, keepdims=True)
    cnt_scr[...] = cnt
    c_ref[...] = jnp.broadcast_to(cnt, c_ref.shape)
    vals = (i1.astype(F32), i2.astype(F32), w1, w2, r1, r2)
    out = jnp.zeros(logits.shape, F32)
    for col, val in enumerate(vals):
        out = jnp.where(lane == col, val, out)
    r_ref[...] = out


def moe_pre(x, mtab, layer, g, router_w, router_b, cfg):
    t = x.shape[0]
    tm = ROW_TILE
    rw = jnp.zeros((D_MODEL, LANES), F32).at[:, :MOE_E].set(router_w)
    rb = jnp.zeros((1, LANES), F32).at[0, :MOE_E].set(router_b)
    return pl.pallas_call(
        _moe_pre_kernel,
        grid=(t // tm,),
        in_specs=[_tok_spec(tm), _mtab_spec(layer, cfg, tm), pl.BlockSpec((1, D_MODEL), lambda i: (0, 0)),
                  pl.BlockSpec((D_MODEL, LANES), lambda i: (0, 0)), pl.BlockSpec((1, LANES), lambda i: (0, 0))],
        out_specs=[pl.BlockSpec((tm, D_MODEL // 2), lambda i: (i, 0)),
                   pl.BlockSpec((tm, LANES), lambda i: (i, 0)),
                   pl.BlockSpec((8, LANES), lambda i: (0, 0))],
        out_shape=[jax.ShapeDtypeStruct((t, D_MODEL // 2), F32),
                   jax.ShapeDtypeStruct((t, LANES), F32),
                   jax.ShapeDtypeStruct((8, LANES), F32)],
        scratch_shapes=[pltpu.VMEM((1, LANES), F32)],
        compiler_params=_cparams(("arbitrary",)),
        name="moe_pre",
    )(x, mtab, g.reshape(1, D_MODEL), rw, rb)


def _moe_ffn_kernel(te_ref, act_ref, x_ref, wg_ref, wh_ref, wo_ref, o_ref, xb_scr, acc_scr):
    i = pl.program_id(0)
    f = pl.program_id(1)
    active = act_ref[i] == 1

    @pl.when(f == 0)
    def _():
        acc_scr[...] = jnp.zeros_like(acc_scr)

    @pl.when(active & (f == 0))
    def _():
        w = pltpu.bitcast(x_ref[...], jnp.uint32)
        lo = pltpu.bitcast(w << 16, F32)
        hi = pltpu.bitcast(w & jnp.uint32(0xFFFF0000), F32)
        xb_scr[...] = jnp.concatenate([lo, hi], axis=1).astype(BF16)

    @pl.when(active)
    def _():
        u = xb_scr[...]
        tf = wg_ref.shape[1]
        sub = 256
        part = None
        for c in range(tf // sub):
            cs = slice(c * sub, (c + 1) * sub)
            a = (_silu(_dot(u, wg_ref[:, cs].astype(BF16)))
                 * _dot(u, wh_ref[:, cs].astype(BF16))).astype(BF16)
            y = _dot(a, wo_ref[cs, :].astype(BF16))
            part = y if part is None else part + y
        acc_scr[...] += part

    @pl.when(f == pl.num_programs(1) - 1)
    def _():
        o_ref[...] = acc_scr[...]


def moe_ffn(xs, tile_expert, tile_active, w_in, w_out, n, tf=512):
    npad = xs.shape[0]
    tm = MOE_TILE
    nf = MOE_FF // tf
    grid_spec = pltpu.PrefetchScalarGridSpec(
        num_scalar_prefetch=2,
        grid=(npad // tm, nf),
        in_specs=[
            pl.BlockSpec((tm, D_MODEL // 2), lambda i, f, te, ac: (i, 0)),
            pl.BlockSpec((None, None, D_MODEL, tf), lambda i, f, te, ac: (n, te[i], 0, f)),
            pl.BlockSpec((None, None, D_MODEL, tf), lambda i, f, te, ac: (n, te[i], 0, nf + f)),
            pl.BlockSpec((None, None, tf, D_MODEL), lambda i, f, te, ac: (n, te[i], f, 0)),
        ],
        out_specs=pl.BlockSpec((tm, D_MODEL), lambda i, f, te, ac: (i, 0)),
        scratch_shapes=[pltpu.VMEM((tm, D_MODEL), BF16), pltpu.VMEM((tm, D_MODEL), F32)],
    )
    return pl.pallas_call(
        _moe_ffn_kernel,
        grid_spec=grid_spec,
        out_shape=jax.ShapeDtypeStruct((npad, D_MODEL), F32),
        compiler_params=_cparams(("parallel", "arbitrary")),
        name="moe_ffn",
    )(tile_expert, tile_active, xs, w_in, w_in, w_out)


def _moe_combine_kernel(y1_ref, y2_ref, r_ref, x_ref, m_ref, *o_refs, cfg):
    r = r_ref[...]
    y = r[:, 2:3] * y1_ref[...] + r[:, 3:4] * y2_ref[...]
    out = x_ref[...] + m_ref[5:6, :] * y
    if len(o_refs) == 1:
        o_refs[0][...] = out
    else:
        is_prompt = pl.program_id(0) < cfg.tp // x_ref.shape[0]

        @pl.when(is_prompt)
        def _():
            o_refs[0][...] = out

        @pl.when(jnp.logical_not(is_prompt))
        def _():
            o_refs[1][...] = out


def moe_combine(y1, y2, route, x, mtab, layer, cfg, split):
    t = x.shape[0]
    tm = ROW_TILE
    npt = cfg.tp // tm
    if split:
        out_specs = [pl.BlockSpec((tm, D_MODEL), lambda i: (jnp.minimum(i, npt - 1), 0)),
                     pl.BlockSpec((tm, D_MODEL), lambda i: (jnp.maximum(i - npt, 0), 0))]
        out_shape = [jax.ShapeDtypeStruct((cfg.tp, D_MODEL), F32), jax.ShapeDtypeStruct((cfg.ts, D_MODEL), F32)]
    else:
        out_specs = _tok_spec(tm)
        out_shape = jax.ShapeDtypeStruct((t, D_MODEL), F32)
    return pl.pallas_call(
        functools.partial(_moe_combine_kernel, cfg=cfg),
        grid=(t // tm,),
        in_specs=[_tok_spec(tm), _tok_spec(tm), pl.BlockSpec((tm, LANES), lambda i: (i, 0)),
                  _tok_spec(tm), _mtab_spec(layer, cfg, tm)],
        out_specs=out_specs,
        out_shape=out_shape,
        compiler_params=_cparams(("arbitrary",)),
        name="moe_combine",
    )(y1, y2, route, x, mtab)


def moe_layer(x, mtab, layer, g, router_w, router_b, w_in, w_out, n, cfg, split=False):
    t = x.shape[0]
    tm = MOE_TILE
    u, route, cnt = moe_pre(x, mtab, layer, g, router_w, router_b, cfg)
    counts = cnt[0, :MOE_E].astype(jnp.int32)
    padded = ((counts + tm - 1) // tm) * tm
    ends = jnp.cumsum(padded)
    starts = ends - padded
    e = route[:, 0:2].astype(jnp.int32)
    dest = starts[e] + route[:, 4:6].astype(jnp.int32)
    npad = 2 * t + MOE_E * tm
    tok = jnp.arange(t, dtype=jnp.int32)
    row_tok = jnp.zeros((npad,), jnp.int32).at[jnp.concatenate([dest[:, 0], dest[:, 1]])].set(
        jnp.concatenate([tok, tok]), unique_indices=True)
    tile_start = jnp.arange(npad // tm, dtype=jnp.int32) * tm
    tile_active = (tile_start < ends[-1]).astype(jnp.int32)
    tile_expert = jnp.minimum(jnp.sum((tile_start[:, None] >= ends[None, :]).astype(jnp.int32), axis=1),
                              MOE_E - 1)
    last_e = tile_expert[jnp.maximum(jnp.sum(tile_active) - 1, 0)]
    tile_expert = jnp.where(tile_active == 1, tile_expert, last_e)
    xs = jnp.take(u, row_tok, axis=0, mode='clip')
    ys = moe_ffn(xs, tile_expert, tile_active, w_in, w_out, n)
    y1 = jnp.take(ys, dest[:, 0], axis=0, mode='clip')
    y2 = jnp.take(ys, dest[:, 1], axis=0, mode='clip')
    return moe_combine(y1, y2, route, x, mtab, layer, cfg, split)


def backbone(x_prompt, x_sample, cache_k, cache_v, state_rglru, state_hgrn, c, c_ctx, p):
    bp, lp, d = x_prompt.shape
    bs, ls, _ = x_sample.shape
    cfg = Cfg(bp, lp, bs, ls)
    x = jnp.concatenate([x_prompt.reshape(bp * lp, d), x_sample.reshape(bs * ls, d)], axis=0)
    cvec = jnp.concatenate([c_ctx[None, :], c], axis=0)
    mtab = modulation_table(cvec, p['mod_w'], p['mod_b'])
    new_k, new_v, new_rg, new_hg = [], [], [], []
    for i in range(DEPTH):
        kind, j = i % 4, i // 4
        g0 = p['norm_g'][i, 0]
        if kind == 0:
            x, st = rglru_layer(x, mtab, i, j, g0, p, state_rglru, cfg)
            new_rg.append(st)
        elif kind == 1:
            x, st = hgrn2_layer(x, mtab, i, j, g0, p, state_hgrn, cfg)
            new_hg.append(st)
        elif kind == 2:
            x = hyena_layer(x, mtab, i, j, g0, p, cfg)
        else:
            x, nk, nv = diffattn_layer(x, mtab, i, j, g0, p, cache_k, cache_v, cfg)
            new_k.append(nk)
            new_v.append(nv)
        n = i // 2
        g1 = p['norm_g'][i, 1]
        if i % 2 == 0:
            x = dense_ffn(x, mtab, i, g1, p['ff_w_in'][n].astype(BF16), p['ff_w_out'][n].astype(BF16), cfg)
        else:
            x = moe_layer(x, mtab, i, g1, p['moe_router'][n], p['moe_router_b'][n],
                          p['moe_w_in'], p['moe_w_out'], n, cfg, split=i == DEPTH - 1)
    xp, xs = x if isinstance(x, (list, tuple)) else (x[:cfg.tp], x[cfg.tp:])
    return (xp.reshape(bp, lp, d), xs.reshape(bs, ls, d),
            jnp.stack(new_k, axis=1), jnp.stack(new_v, axis=1),
            jnp.stack(new_rg, axis=1), jnp.stack(new_hg, axis=1))


def kernel(x_prompt, x_sample, cache_k, cache_v, state_rglru, state_hgrn, c, c_ctx, mod_w, mod_b, norm_g, hgrn_lb, rg_w_in, rg_conv_w, rg_conv_b, rg_w_a, rg_b_a, rg_w_x, rg_b_x, rg_lambda, rg_w_out, hg_w_in, hg_norm_g, hg_w_out, hy_w_in, hy_b_in, hy_conv_w, hy_conv_b, hy_f_w1, hy_f_b1, hy_f_w2, hy_f_b2, hy_f_w3, hy_freq, hy_decay, hy_bias, hy_w_out, hy_b_out, da_w_in, da_q_norm, da_k_norm, da_lambda, da_sub_norm, da_w_out, ff_w_in, ff_w_out, moe_router, moe_router_b, moe_w_in, moe_w_out):
    p = dict(mod_w=mod_w, mod_b=mod_b, norm_g=norm_g, hgrn_lb=hgrn_lb,
             rg_w_in=rg_w_in, rg_conv_w=rg_conv_w, rg_conv_b=rg_conv_b, rg_w_a=rg_w_a, rg_b_a=rg_b_a,
             rg_w_x=rg_w_x, rg_b_x=rg_b_x, rg_lambda=rg_lambda, rg_w_out=rg_w_out,
             hg_w_in=hg_w_in, hg_norm_g=hg_norm_g, hg_w_out=hg_w_out,
             hy_w_in=hy_w_in, hy_b_in=hy_b_in, hy_conv_w=hy_conv_w, hy_conv_b=hy_conv_b,
             hy_f_w1=hy_f_w1, hy_f_b1=hy_f_b1, hy_f_w2=hy_f_w2, hy_f_b2=hy_f_b2, hy_f_w3=hy_f_w3,
             hy_freq=hy_freq, hy_decay=hy_decay, hy_bias=hy_bias, hy_w_out=hy_w_out, hy_b_out=hy_b_out,
             da_w_in=da_w_in, da_q_norm=da_q_norm, da_k_norm=da_k_norm, da_lambda=da_lambda,
             da_sub_norm=da_sub_norm, da_w_out=da_w_out,
             ff_w_in=ff_w_in, ff_w_out=ff_w_out, moe_router=moe_router, moe_router_b=moe_router_b,
             moe_w_in=moe_w_in, moe_w_out=moe_w_out)
    return backbone(x_prompt, x_sample, cache_k, cache_v, state_rglru, state_hgrn, c, c_ctx, p)
```
